```python
import math
import jax
import jax.numpy as jnp
from jax import lax
import numpy as np

D_MODEL = 2048
BATCH = 1
SEQ = 8192
DEPTH = 1

MLSTM_HEADS = 4
MLSTM_DQK = 128
MLSTM_DV = 256
MLSTM_CHUNK = 64
CONV_WIDTH = 4
DIFF_HEADS = 8
DIFF_DH = 64
DIFF_DV = 2 * DIFF_DH
Q_BLOCK = 128
REL_BUCKETS = 32
REL_MAX_DIST = 128
N_EXPERTS = 256
TOP_K = 8
N_GROUPS = 8
TOPK_GROUPS = 4
D_EXPERT = 512
ROUTED_SCALE = 2.5
EXPERT_BLOCK = 128

LN_EPS = 1e-5
ALPHA = (2 * DEPTH) ** 0.25
BETA = (8 * DEPTH) ** -0.25

M_QK = MLSTM_HEADS * MLSTM_DQK
M_V = MLSTM_HEADS * MLSTM_DV
D_QK = DIFF_HEADS * 2 * DIFF_DH
D_VW = DIFF_HEADS * DIFF_DV
IN_SPLITS = (M_QK, M_QK, M_V, M_V, MLSTM_HEADS, MLSTM_HEADS, D_QK, D_QK, D_VW)
IN_WIDTH = sum(IN_SPLITS)
SPLIT_IDX = tuple(int(c) for c in np.cumsum(IN_SPLITS)[:-1])
V_COLS = (2, 8)
MIX_WIDTH = M_V + D_VW

kernel_name = 'hybrid_mlstm_diffattn_moe'


def layer_norm(x, g, b):
    xf = x.astype(jnp.float32)
    mu = jnp.mean(xf, -1, keepdims=True)
    var = jnp.mean(jnp.square(xf - mu), -1, keepdims=True)
    return ((xf - mu) * lax.rsqrt(var + LN_EPS) * g + b).astype(x.dtype)


def causal_short_conv(u, w, b):
    s = u.shape[1]
    up = jnp.pad(u, ((0, 0), (CONV_WIDTH - 1, 0), (0, 0)))
    out = b
    for j in range(CONV_WIDTH):
        out = out + w[j] * up[:, j:j + s]
    return out


def t5_bucket(rel):
    n = jnp.maximum(rel, 0)
    max_exact = REL_BUCKETS // 2
    large = max_exact + (jnp.log(jnp.maximum(n, 1).astype(jnp.float32) / max_exact)
                         / math.log(REL_MAX_DIST / max_exact)
                         * (REL_BUCKETS - max_exact)).astype(jnp.int32)
    large = jnp.minimum(large, REL_BUCKETS - 1)
    return jnp.where(n < max_exact, n, large)


def mlstm_chunkwise(q, k, v, i_pre, log_f):
    bsz, s, nh, dk = q.shape
    dv = v.shape[-1]
    L = MLSTM_CHUNK
    nc = s // L

    def chunks(a):
        a = a.reshape((bsz, nc, L, nh) + a.shape[3:])
        return jnp.moveaxis(a, (1, 3), (0, 2))

    tri = jnp.tril(jnp.ones((L, L), dtype=bool))

    def step(carry, inp):
        C, n, m = carry
        qc, kc, vc, ic, fc = inp
        b = jnp.cumsum(fc, axis=-1)
        dmat = b[..., :, None] - b[..., None, :] + ic[..., None, :]
        dmat = jnp.where(tri, dmat, -jnp.inf)
        inter = b + m[..., None]
        m_t = jnp.maximum(inter, jnp.max(dmat, -1))
        wts = jnp.exp(dmat - m_t[..., None])
        g = jnp.exp(inter - m_t)
        sqk = jnp.einsum('bhtd,bhsd->bhts', qc, kc) * wts
        num = g[..., None] * jnp.einsum('bhtd,bhde->bhte', qc, C) + jnp.einsum('bhts,bhse->bhte', sqk, vc)
        den = g * jnp.einsum('bhtd,bhd->bht', qc, n) + jnp.sum(sqk, -1)
        h = num / jnp.maximum(jnp.abs(den), jnp.exp(-m_t))[..., None]
        b_last = b[..., -1]
        w_last = b_last[..., None] - b + ic
        m_new = jnp.maximum(b_last + m, jnp.max(w_last, -1))
        decay = jnp.exp(b_last + m - m_new)
        ws = jnp.exp(w_last - m_new[..., None])
        C_new = decay[..., None, None] * C + jnp.einsum('bhsd,bhse->bhde', kc * ws[..., None], vc)
        n_new = decay[..., None] * n + jnp.einsum('bhs,bhsd->bhd', ws, kc)
        return (C_new, n_new, m_new), h

    init = (jnp.zeros((bsz, nh, dk, dv), jnp.float32),
            jnp.zeros((bsz, nh, dk), jnp.float32),
            jnp.zeros((bsz, nh), jnp.float32))
    _, hs = lax.scan(step, init, (chunks(q), chunks(k), chunks(v), chunks(i_pre), chunks(log_f)))
    hs = jnp.moveaxis(hs, (0, 2), (1, 3))
    return hs.reshape(bsz, s, nh, dv)


def diff_attention(q, k, v, lam, lam_init, rel_bias):
    bsz, s, nh, _, dh = q.shape
    dv = v.shape[-1]
    nqb = s // Q_BLOCK
    qf = q.astype(jnp.float32) * (DIFF_DH ** -0.5)
    kf = k.astype(jnp.float32)
    vf = v.astype(jnp.float32)
    qb = jnp.moveaxis(qf.reshape(bsz, nqb, Q_BLOCK, nh, 2, dh), 1, 0)
    kpos = jnp.arange(s)
    table = rel_bias.astype(jnp.float32)

    def one_block(args):
        qblk, bi = args
        qpos = bi * Q_BLOCK + jnp.arange(Q_BLOCK)
        rel = qpos[:, None] - kpos[None, :]
        bias = jnp.moveaxis(table[t5_bucket(rel)], -1, 0)
        sc = jnp.einsum('bqhcd,bkhcd->bhcqk', qblk, kf) + bias[None, :, None]
        sc = jnp.where(rel >= 0, sc, -jnp.inf)
        p = jax.nn.softmax(sc, axis=-1)
        a = p[:, :, 0] - lam * p[:, :, 1]
        return jnp.einsum('bhqk,bkhe->bqhe', a, vf)

    o = lax.map(one_block, (qb, jnp.arange(nqb)))
    o = jnp.moveaxis(o, 0, 1).reshape(bsz, s, nh, dv)
    o = o * lax.rsqrt(jnp.mean(jnp.square(o), -1, keepdims=True) + LN_EPS)
    return o * (1.0 - lam_init)


def swiglu(x, wg, wu, wd):
    return (jax.nn.silu(x @ wg) * (x @ wu)) @ wd


def route(xt, w_router, router_bias):
    t = xt.shape[0]
    sc = jax.nn.sigmoid((xt @ w_router).astype(jnp.float32))
    sel = sc + router_bias.astype(jnp.float32)
    grp = sel.reshape(t, N_GROUPS, N_EXPERTS // N_GROUPS)
    gscore = jnp.sum(lax.top_k(grp, 2)[0], -1)
    _, gidx = lax.top_k(gscore, TOPK_GROUPS)
    gmask = jnp.sum(jax.nn.one_hot(gidx, N_GROUPS, dtype=jnp.float32), 1) > 0
    emask = jnp.repeat(gmask, N_EXPERTS // N_GROUPS, axis=1)
    _, topi = lax.top_k(jnp.where(emask, sel, -jnp.inf), TOP_K)
    topw = jnp.take_along_axis(sc, topi, -1)
    topw = topw / jnp.sum(topw, -1, keepdims=True) * ROUTED_SCALE
    return topi, topw


def routed_experts(xt, topi, topw, w_gate, w_up, w_down):
    t, _ = xt.shape
    a = t * TOP_K
    n_rows = -(-a // EXPERT_BLOCK) * EXPERT_BLOCK + N_EXPERTS * EXPERT_BLOCK
    nb = n_rows // EXPERT_BLOCK
    flat_e = topi.reshape(a)
    flat_t = jnp.repeat(jnp.arange(t, dtype=jnp.int32), TOP_K)
    flat_w = topw.reshape(a)
    order = jnp.argsort(flat_e, stable=True)
    e_s, t_s, w_s = flat_e[order], flat_t[order], flat_w[order]
    counts = jnp.bincount(flat_e, length=N_EXPERTS)
    padded = (counts + EXPERT_BLOCK - 1) // EXPERT_BLOCK * EXPERT_BLOCK
    pad_end = jnp.cumsum(padded)
    grp_start = jnp.cumsum(counts) - counts
    dest = (pad_end - padded)[e_s] + jnp.arange(a) - grp_start[e_s]
    row_tok = jnp.zeros((n_rows,), jnp.int32).at[dest].set(t_s)
    row_w = jnp.zeros((n_rows,), xt.dtype).at[dest].set(w_s.astype(xt.dtype))
    block_e = jnp.minimum(jnp.searchsorted(pad_end, jnp.arange(nb) * EXPERT_BLOCK, side='right'),
                          N_EXPERTS - 1)

    def step(out, blk):
        tok, wt, e = blk
        xb = xt[tok]
        yb = swiglu(xb, w_gate[e], w_up[e], w_down[e])
        return out.at[tok].add(yb * wt[:, None]), None

    out, _ = lax.scan(step, jnp.zeros_like(xt),
                      (row_tok.reshape(nb, EXPERT_BLOCK), row_w.reshape(nb, EXPERT_BLOCK), block_e))
    return out


def setup_inputs(seed: int = 0) -> dict:
    key = jax.random.key(seed)
    ks = jax.random.split(key, 32)
    f32 = jnp.float32

    def nrm(k, shape, scale):
        return jax.random.normal(k, shape, f32) * scale

    col_scale = np.concatenate([np.full((w,), BETA if i in V_COLS else 1.0, np.float32)
                                for i, w in enumerate(IN_SPLITS)]) * (D_MODEL ** -0.5)
    inp = {}
    inp['x'] = nrm(ks[0], (BATCH, SEQ, D_MODEL), 1.0)
    inp['ln_emb_g'] = 1.0 + nrm(ks[1], (D_MODEL,), 0.02)
    inp['ln_emb_b'] = nrm(ks[2], (D_MODEL,), 0.02)
    inp['w_in'] = nrm(ks[3], (DEPTH, D_MODEL, IN_WIDTH), 1.0) * jnp.asarray(col_scale)
    inp['conv_w'] = nrm(ks[4], (DEPTH, CONV_WIDTH, 2 * M_QK), CONV_WIDTH ** -0.5)
    inp['conv_b'] = nrm(ks[5], (DEPTH, 2 * M_QK), 0.02)
    inp['b_igate'] = nrm(ks[6], (DEPTH, MLSTM_HEADS), 0.1)
    inp['b_fgate'] = jnp.linspace(3.0, 6.0, MLSTM_HEADS, dtype=f32)[None] + nrm(ks[7], (DEPTH, MLSTM_HEADS), 0.1)
    inp['mlstm_norm_g'] = 1.0 + nrm(ks[8], (DEPTH, M_V), 0.02)
    inp['lambda_q1'] = nrm(ks[9], (DEPTH, DIFF_DH), 0.1)
    inp['lambda_k1'] = nrm(ks[10], (DEPTH, DIFF_DH), 0.1)
    inp['lambda_q2'] = nrm(ks[11], (DEPTH, DIFF_DH), 0.1)
    inp['lambda_k2'] = nrm(ks[12], (DEPTH, DIFF_DH), 0.1)
    inp['rel_bias'] = nrm(ks[13], (REL_BUCKETS, DIFF_HEADS), 0.5)
    inp['w_out'] = nrm(ks[14], (DEPTH, MIX_WIDTH, D_MODEL), (MIX_WIDTH ** -0.5) * BETA)
    inp['ln1_g'] = 1.0 + nrm(ks[15], (DEPTH, D_MODEL), 0.02)
    inp['ln1_b'] = nrm(ks[16], (DEPTH, D_MODEL), 0.02)
    inp['w_router'] = nrm(ks[17], (DEPTH, D_MODEL, N_EXPERTS), D_MODEL ** -0.5)
    inp['router_bias'] = nrm(ks[18], (DEPTH, N_EXPERTS), 0.01)
    inp['w_gate'] = nrm(ks[19], (DEPTH, N_EXPERTS, D_MODEL, D_EXPERT), D_MODEL ** -0.5)
    inp['w_up'] = nrm(ks[20], (DEPTH, N_EXPERTS, D_MODEL, D_EXPERT), D_MODEL ** -0.5)
    inp['w_down'] = nrm(ks[21], (DEPTH, N_EXPERTS, D_EXPERT, D_MODEL), (D_EXPERT ** -0.5) * BETA)
    inp['ws_gate'] = nrm(ks[22], (DEPTH, D_MODEL, D_EXPERT), D_MODEL ** -0.5)
    inp['ws_up'] = nrm(ks[23], (DEPTH, D_MODEL, D_EXPERT), D_MODEL ** -0.5)
    inp['ws_down'] = nrm(ks[24], (DEPTH, D_EXPERT, D_MODEL), (D_EXPERT ** -0.5) * BETA)
    inp['ln2_g'] = 1.0 + nrm(ks[25], (DEPTH, D_MODEL), 0.02)
    inp['ln2_b'] = nrm(ks[26], (DEPTH, D_MODEL), 0.02)
    return inp


def reference(x, ln_emb_g, ln_emb_b, w_in, conv_w, conv_b, b_igate, b_fgate, mlstm_norm_g,
              lambda_q1, lambda_k1, lambda_q2, lambda_k2, rel_bias, w_out, ln1_g, ln1_b,
              w_router, router_bias, w_gate, w_up, w_down, ws_gate, ws_up, ws_down, ln2_g, ln2_b):
    f32 = jnp.float32
    bsz, s, d = x.shape
    h = layer_norm(x, ln_emb_g, ln_emb_b)
    for l in range(DEPTH):
        lam_init = 0.8 - 0.6 * math.exp(-0.3 * l)
        proj = jnp.einsum('bsd,de->bse', h, w_in[l])
        mq, mk, mv, mo, mi, mf, dq, dk, dvv = jnp.split(proj, SPLIT_IDX, axis=-1)
        qk = jax.nn.silu(causal_short_conv(jnp.concatenate([mq, mk], -1), conv_w[l], conv_b[l]))
        mq, mk = jnp.split(qk, 2, axis=-1)
        q_m = mq.reshape(bsz, s, MLSTM_HEADS, MLSTM_DQK).astype(f32)
        k_m = mk.reshape(bsz, s, MLSTM_HEADS, MLSTM_DQK).astype(f32) * (MLSTM_DQK ** -0.5)
        v_m = mv.reshape(bsz, s, MLSTM_HEADS, MLSTM_DV).astype(f32)
        i_pre = (mi + b_igate[l]).astype(f32)
        log_f = jax.nn.log_sigmoid((mf + b_fgate[l]).astype(f32))
        hm = mlstm_chunkwise(q_m, k_m, v_m, i_pre, log_f)
        mu = jnp.mean(hm, -1, keepdims=True)
        var = jnp.mean(jnp.square(hm - mu), -1, keepdims=True)
        hm = (hm - mu) * lax.rsqrt(var + LN_EPS) * mlstm_norm_g[l].reshape(MLSTM_HEADS, MLSTM_DV)
        hm = hm.reshape(bsz, s, M_V).astype(h.dtype) * jax.nn.sigmoid(mo)
        lam = (jnp.exp(jnp.sum(lambda_q1[l].astype(f32) * lambda_k1[l].astype(f32)))
               - jnp.exp(jnp.sum(lambda_q2[l].astype(f32) * lambda_k2[l].astype(f32))) + lam_init)
        hd = diff_attention(dq.reshape(bsz, s, DIFF_HEADS, 2, DIFF_DH),
                            dk.reshape(bsz, s, DIFF_HEADS, 2, DIFF_DH),
                            dvv.reshape(bsz, s, DIFF_HEADS, DIFF_DV), lam, lam_init, rel_bias)
        hd = hd.reshape(bsz, s, D_VW).astype(h.dtype)
        mix = jnp.einsum('bse,ed->bsd', jnp.concatenate([hm, hd], -1), w_out[l])
        h = layer_norm(ALPHA * h + mix, ln1_g[l], ln1_b[l])
        ht = h.reshape(bsz * s, d)
        topi, topw = route(ht, w_router[l], router_bias[l])
        y = routed_experts(ht, topi, topw, w_gate[l], w_up[l], w_down[l])
        y = y + swiglu(ht, ws_gate[l], ws_up[l], ws_down[l])
        h = layer_norm(ALPHA * h + y.reshape(bsz, s, d), ln2_g[l], ln2_b[l])
    return h
```

```python
import functools
import math

import jax
import jax.numpy as jnp
from jax import lax
from jax.experimental import pallas as pl
from jax.experimental.pallas import tpu as pltpu

F32 = jnp.float32
BF16 = jnp.bfloat16

DEPTH = 1
MLSTM_HEADS = 4
MLSTM_DQK = 128
MLSTM_DV = 256
CONV_WIDTH = 4
DIFF_HEADS = 8
DIFF_DH = 64
DIFF_DV = 2 * DIFF_DH
REL_BUCKETS = 32
REL_MAX_DIST = 128
N_EXPERTS = 256
TOP_K = 8
TOP_K_SHIFT = 3
N_GROUPS = 8
TOPK_GROUPS = 4
D_EXPERT = 512
ROUTED_SCALE = 2.5
LN_EPS = 1e-5
ALPHA = (2 * DEPTH) ** 0.25
LAM_INIT = 0.8 - 0.6 * math.exp(-0.3 * 0)

M_QK = MLSTM_HEADS * MLSTM_DQK
M_V = MLSTM_HEADS * MLSTM_DV
D_QK = DIFF_HEADS * 2 * DIFF_DH
D_VW = DIFF_HEADS * DIFF_DV
PROJ_W = 2 * M_QK + 2 * M_V + 2 * D_QK + D_VW

V7X_LANES = 128
V7X_SUBLANES = 8
V7X_VMEM_LIMIT_BYTES = 56 * 1024 * 1024

PROJ_TM = 512
PROJ_TN = 512
MLSTM_L = 256
ATT_B = 1024
ATT_STRIP = 256
OUT_TM = 512
ROUTE_TM = 256
MOE_TR = 256
MOVE_TM = 256

NEG = -1e30
LOG2E = 1.4426950408889634


def _params(semantics):
    return pltpu.CompilerParams(dimension_semantics=semantics, vmem_limit_bytes=V7X_VMEM_LIMIT_BYTES)


def _layer_norm(x, g, b):
    mu = jnp.mean(x, -1, keepdims=True)
    xc = x - mu
    var = jnp.mean(xc * xc, -1, keepdims=True)
    return xc * lax.rsqrt(var + LN_EPS) * g + b


def _dot(a, b):
    return jnp.dot(a, b, preferred_element_type=F32)


def _dot_nt(a, b):
    return lax.dot_general(a, b, (((1,), (1,)), ((), ())), preferred_element_type=F32)


def _split3(a):
    a1 = a.astype(BF16)
    r1 = a - a1.astype(F32)
    a2 = r1.astype(BF16)
    a3 = (r1 - a2.astype(F32)).astype(BF16)
    return a1, a2, a3


def _log_sigmoid(x):
    return jnp.minimum(x, 0.0) - jnp.log(1.0 + jnp.exp(-jnp.abs(x)))


def _ln_proj_kernel(x_ref, g_ref, b_ref, w_ref, wg_ref, wgt_ref, o_ref, gates_ref, gatest_ref, hb_ref):
    @pl.when(pl.program_id(1) == 0)
    def _():
        hb = _layer_norm(x_ref[...], g_ref[...], b_ref[...]).astype(BF16)
        hb_ref[...] = hb
        gates_ref[...] = _dot(hb, wg_ref[...])
        gatest_ref[...] = _dot_nt(wgt_ref[...], hb)

    o_ref[...] = _dot(hb_ref[...], w_ref[...]).astype(o_ref.dtype)


def _ln_proj(x2, g, b, w_main, w_gates, w_gates_t):
    s, d = x2.shape
    tm, tn = min(PROJ_TM, s), PROJ_TN
    ng = w_gates.shape[1]
    return pl.pallas_call(
        _ln_proj_kernel,
        grid=(s // tm, PROJ_W // tn),
        in_specs=[
            pl.BlockSpec((tm, d), lambda i, j: (i, 0)),
            pl.BlockSpec((1, d), lambda i, j: (0, 0)),
            pl.BlockSpec((1, d), lambda i, j: (0, 0)),
            pl.BlockSpec((d, tn), lambda i, j: (0, j)),
            pl.BlockSpec((d, ng), lambda i, j: (0, 0)),
            pl.BlockSpec((w_gates_t.shape[0], d), lambda i, j: (0, 0)),
        ],
        out_specs=[
            pl.BlockSpec((tm, tn), lambda i, j: (i, j)),
            pl.BlockSpec((tm, ng), lambda i, j: (i, 0)),
            pl.BlockSpec((w_gates_t.shape[0], tm), lambda i, j: (0, i)),
        ],
        out_shape=[
            jax.ShapeDtypeStruct((s, PROJ_W), BF16),
            jax.ShapeDtypeStruct((s, ng), F32),
            jax.ShapeDtypeStruct((w_gates_t.shape[0], s), F32),
        ],
        scratch_shapes=[pltpu.VMEM((tm, d), BF16)],
        compiler_params=_params(("arbitrary", "arbitrary")),
        name="ln_proj",
    )(x2, g, b, w_main, w_gates, w_gates_t)


def _mlstm_kernel(bi_ref, bf_ref, mq_ref, mk_ref, mv_ref, mo_ref, gates_ref, gatest_ref,
                  cw_ref, cb_ref, ng_ref, o_ref, c_ref, n_ref, m_ref, tail_ref):
    L = mq_ref.shape[0]
    dk, dv = MLSTM_DQK, MLSTM_DV

    @pl.when(pl.program_id(0) == 0)
    def _():
        c_ref[...] = jnp.zeros_like(c_ref)
        n_ref[...] = jnp.zeros_like(n_ref)
        m_ref[...] = jnp.zeros_like(m_ref)
        tail_ref[...] = jnp.zeros_like(tail_ref)

    u = jnp.concatenate([mq_ref[...], mk_ref[...]], axis=1).astype(F32)
    tail = tail_ref[...]
    row8 = lax.broadcasted_iota(jnp.int32, (V7X_SUBLANES, u.shape[1]), 0)
    conv = cb_ref[...] + cw_ref[CONV_WIDTH - 1:CONV_WIDTH, :] * u
    for back in range(1, CONV_WIDTH):
        ur = pltpu.roll(u, back, 0)
        head = jnp.where(row8 < back, pltpu.roll(tail, back, 0), ur[:V7X_SUBLANES])
        shifted = jnp.concatenate([head, ur[V7X_SUBLANES:]], axis=0)
        conv = conv + cw_ref[CONV_WIDTH - 1 - back:CONV_WIDTH - back, :] * shifted
    tail_ref[...] = u[L - V7X_SUBLANES:]
    qk = conv * jax.nn.sigmoid(conv)
    q_all = qk[:, :M_QK]
    k_all = qk[:, M_QK:] * (dk ** -0.5)

    r_i = lax.broadcasted_iota(jnp.int32, (L, L), 0)
    c_i = lax.broadcasted_iota(jnp.int32, (L, L), 1)
    causal = c_i <= r_i
    tril = jnp.where(causal, 1.0, 0.0).astype(BF16)
    triu = jnp.where(r_i <= c_i, 1.0, 0.0).astype(BF16)

    for h in range(MLSTM_HEADS):
        q = q_all[:, h * dk:(h + 1) * dk]
        k = k_all[:, h * dk:(h + 1) * dk]
        qb, kb = q.astype(BF16), k.astype(BF16)
        vb = mv_ref[:, h * dv:(h + 1) * dv]
        b_i, b_f = bi_ref[h], bf_ref[h]

        gblk = gates_ref[:, h * V7X_LANES:(h + 1) * V7X_LANES]
        i_col = gblk[:, 0:1] + b_i
        lf_blk = _log_sigmoid(gblk + b_f)
        f1, f2, f3 = _split3(lf_blk)
        bcum_blk = _dot(tril, f1) + _dot(tril, f2) + _dot(tril, f3)
        b_col = bcum_blk[:, 1:2]
        gt = gatest_ref[h * V7X_SUBLANES:(h + 1) * V7X_SUBLANES, :]
        i_row = gt[0:1, :] + b_i
        lf_rows = _log_sigmoid(gt + b_f)
        g1, g2, g3 = _split3(lf_rows)
        b_row = (_dot(g1, triu) + _dot(g2, triu) + _dot(g3, triu))[1:2, :]

        m_prev = m_ref[h, 0:1, 0:1]
        dmat = jnp.where(causal, b_col - b_row + i_row, NEG)
        inter = b_col + m_prev
        m_t = jnp.maximum(inter, jnp.max(dmat, -1, keepdims=True))
        wts = jnp.exp(dmat - m_t)
        g = jnp.exp(inter - m_t)
        sqk = _dot_nt(qb, kb) * wts
        c_prev = c_ref[h]
        n_prev = n_ref[h, 0:1, :]
        num = g * _dot(qb, c_prev.astype(BF16)) + _dot(sqk.astype(BF16), vb)
        den = g * jnp.sum(q * n_prev, -1, keepdims=True) + jnp.sum(sqk, -1, keepdims=True)
        hh = num / jnp.maximum(jnp.abs(den), jnp.exp(-m_t))

        b_last = b_col[L - 1:L, :]
        w_last_row = b_last - b_row + i_row
        m_new = jnp.maximum(b_last + m_prev, jnp.max(w_last_row, -1, keepdims=True))
        decay = jnp.exp(b_last + m_prev - m_new)
        ws_col = jnp.exp(b_last - b_col + i_col - m_new)
        kw = k * ws_col
        c_ref[h] = decay * c_prev + _dot(kw.T.astype(BF16), vb)
        n_ref[h, 0:1, :] = decay * n_prev + jnp.sum(kw, 0, keepdims=True)
        m_ref[h] = jnp.broadcast_to(m_new, m_ref.shape[1:])

        mu = jnp.mean(hh, -1, keepdims=True)
        hc = hh - mu
        var = jnp.mean(hc * hc, -1, keepdims=True)
        hn = hc * lax.rsqrt(var + LN_EPS) * ng_ref[:, h * dv:(h + 1) * dv]
        og = jax.nn.sigmoid(mo_ref[:, h * dv:(h + 1) * dv].astype(F32))
        o_ref[:, h * dv:(h + 1) * dv] = (hn * og).astype(o_ref.dtype)


def _mlstm(proj, gates, gates_t, conv_w, conv_b, b_i, b_f, norm_g):
    s = proj.shape[0]
    L = min(MLSTM_L, s)
    smem = pl.BlockSpec(memory_space=pltpu.SMEM)
    return pl.pallas_call(
        _mlstm_kernel,
        grid=(s // L,),
        in_specs=[
            smem, smem,
            pl.BlockSpec((L, M_QK), lambda c: (c, 0)),
            pl.BlockSpec((L, M_QK), lambda c: (c, 1)),
            pl.BlockSpec((L, M_V), lambda c: (c, 1)),
            pl.BlockSpec((L, M_V), lambda c: (c, 2)),
            pl.BlockSpec((L, gates.shape[1]), lambda c: (c, 0)),
            pl.BlockSpec((gates_t.shape[0], L), lambda c: (0, c)),
            pl.BlockSpec((CONV_WIDTH, 2 * M_QK), lambda c: (0, 0)),
            pl.BlockSpec((1, 2 * M_QK), lambda c: (0, 0)),
            pl.BlockSpec((1, M_V), lambda c: (0, 0)),
        ],
        out_specs=pl.BlockSpec((L, M_V), lambda c: (c, 0)),
        out_shape=jax.ShapeDtypeStruct((s, M_V), BF16),
        scratch_shapes=[
            pltpu.VMEM((MLSTM_HEADS, MLSTM_DQK, MLSTM_DV), F32),
            pltpu.VMEM((MLSTM_HEADS, V7X_SUBLANES, MLSTM_DQK), F32),
            pltpu.VMEM((MLSTM_HEADS, V7X_SUBLANES, V7X_LANES), F32),
            pltpu.VMEM((V7X_SUBLANES, 2 * M_QK), F32),
        ],
        compiler_params=_params(("arbitrary",)),
        name="mlstm",
    )(b_i, b_f, proj, proj, proj, proj, gates, gates_t, conv_w, conv_b, norm_g)


def _t5_bias_tile(table_ref, h, offset):
    n_t = V7X_LANES
    r_i = lax.broadcasted_iota(jnp.int32, (n_t, n_t), 0)
    c_i = lax.broadcasted_iota(jnp.int32, (n_t, n_t), 1)
    n = jnp.maximum(offset + r_i - c_i, 0)
    max_exact = REL_BUCKETS // 2
    large = max_exact + (jnp.log(jnp.maximum(n, 1).astype(F32) / max_exact)
                         / math.log(REL_MAX_DIST / max_exact) * (REL_BUCKETS - max_exact)).astype(jnp.int32)
    large = jnp.minimum(large, REL_BUCKETS - 1)
    bucket = jnp.where(n < max_exact, n, large)
    out = jnp.zeros((n_t, n_t), F32)
    for b in range(REL_BUCKETS):
        out = jnp.where(bucket == b, table_ref[b * DIFF_HEADS + h], out)
    return out * LOG2E


def _diff_attn_kernel(qi_ref, kj_ref, q_ref, k_ref, v_ref, lam_ref, table_ref, o_ref,
                      q1_ref, q2_ref, m_ref, l_ref, acc_ref, bdiag_ref, bsub_ref):
    h = pl.program_id(0)
    p = pl.program_id(1)
    i = qi_ref[p]
    j = kj_ref[p]
    B = q_ref.shape[0]
    SR = min(ATT_STRIP, B)
    n_strip = B // SR
    n_sub = B // V7X_LANES
    c_far = table_ref[(REL_BUCKETS - 1) * DIFF_HEADS + h] * LOG2E

    @pl.when(p == 0)
    def _():
        p0 = _t5_bias_tile(table_ref, h, 0)
        p0 = jnp.where(lax.broadcasted_iota(jnp.int32, p0.shape, 1) <= lax.broadcasted_iota(jnp.int32, p0.shape, 0),
                       p0, NEG)
        p1 = _t5_bias_tile(table_ref, h, V7X_LANES)
        far = jnp.full((V7X_LANES, V7X_LANES), c_far, F32)
        neg = jnp.full((V7X_LANES, V7X_LANES), NEG, F32)
        for a in range(n_sub):
            for b in range(n_sub):
                tile = p0 if a == b else p1 if a == b + 1 else far if a > b else neg
                bdiag_ref[a * V7X_LANES:(a + 1) * V7X_LANES, b * V7X_LANES:(b + 1) * V7X_LANES] = tile
        bsub_ref[...] = jnp.full(bsub_ref.shape, c_far, F32)
        bsub_ref[0:V7X_LANES, B - V7X_LANES:B] = p1

    @pl.when(j == 0)
    def _():
        q = q_ref[...].astype(F32) * ((DIFF_DH ** -0.5) * LOG2E)
        lane = lax.broadcasted_iota(jnp.int32, q.shape, 1)
        q1_ref[...] = jnp.where(lane < DIFF_DH, q, 0.0).astype(BF16)
        q2_ref[...] = jnp.where(lane >= DIFF_DH, q, 0.0).astype(BF16)
        m_ref[...] = jnp.full(m_ref.shape, NEG, F32)
        l_ref[...] = jnp.zeros_like(l_ref)
        acc_ref[...] = jnp.zeros_like(acc_ref)

    def strip(r, kl, bias):
        rows = slice(r * SR, (r + 1) * SR)
        kb = k_ref[0:kl, :]
        vb = v_ref[0:kl, :]
        for a, qz_ref in enumerate((q1_ref, q2_ref)):
            s = _dot_nt(qz_ref[rows, :], kb) + bias
            m_old = m_ref[a, rows, :]
            m_new = jnp.maximum(m_old, jnp.max(s, -1, keepdims=True))
            alpha = jnp.exp2(m_old - m_new)
            pm = jnp.exp2(s - m_new)
            l_ref[a, rows, :] = alpha * l_ref[a, rows, :] + jnp.sum(pm, -1, keepdims=True)
            acc_ref[a, rows, :] = alpha * acc_ref[a, rows, :] + _dot(pm.astype(BF16), vb)
            m_ref[a, rows, :] = m_new

    @pl.when(j == i)
    def _():
        for r in range(n_strip):
            kl = (r + 1) * SR
            strip(r, kl, bdiag_ref[r * SR:(r + 1) * SR, 0:kl])
        lq1, lk1, lq2, lk2 = (lam_ref[t:t + 1, :] for t in range(4))
        lam = (jnp.exp(jnp.sum(lq1 * lk1, -1, keepdims=True)) - jnp.exp(jnp.sum(lq2 * lk2, -1, keepdims=True))
               + LAM_INIT)
        o = acc_ref[0] / l_ref[0] - lam * (acc_ref[1] / l_ref[1])
        o = o * lax.rsqrt(jnp.mean(o * o, -1, keepdims=True) + LN_EPS)
        o_ref[...] = (o * (1.0 - LAM_INIT)).astype(o_ref.dtype)

    @pl.when(j == i - 1)
    def _():
        strip(0, B, bsub_ref[...])
        for r in range(1, n_strip):
            strip(r, B, c_far)

    @pl.when(j < i - 1)
    def _():
        for r in range(n_strip):
            strip(r, B, c_far)


def _diff_attn(proj, lam_vecs, table_flat):
    s = proj.shape[0]
    B = min(ATT_B, s)
    nq = s // B
    pairs = [(i, j) for i in range(nq) for j in range(i + 1)]
    qi = jnp.asarray([a for a, _ in pairs], jnp.int32)
    kj = jnp.asarray([b for _, b in pairs], jnp.int32)
    q_blk = (2 * M_QK + 2 * M_V) // V7X_LANES
    k_blk = q_blk + D_QK // V7X_LANES
    v_blk = k_blk + D_QK // V7X_LANES
    SR = min(ATT_STRIP, B)
    grid_spec = pltpu.PrefetchScalarGridSpec(
        num_scalar_prefetch=2,
        grid=(DIFF_HEADS, len(pairs)),
        in_specs=[
            pl.BlockSpec((B, V7X_LANES), lambda h, p, qi, kj: (qi[p], q_blk + h)),
            pl.BlockSpec((B, V7X_LANES), lambda h, p, qi, kj: (kj[p], k_blk + h)),
            pl.BlockSpec((B, V7X_LANES), lambda h, p, qi, kj: (kj[p], v_blk + h)),
            pl.BlockSpec((4, DIFF_DH), lambda h, p, qi, kj: (0, 0)),
            pl.BlockSpec(memory_space=pltpu.SMEM),
        ],
        out_specs=pl.BlockSpec((B, V7X_LANES), lambda h, p, qi, kj: (qi[p], h)),
        scratch_shapes=[
            pltpu.VMEM((B, V7X_LANES), BF16),
            pltpu.VMEM((B, V7X_LANES), BF16),
            pltpu.VMEM((2, B, 1), F32),
            pltpu.VMEM((2, B, 1), F32),
            pltpu.VMEM((2, B, DIFF_DV), F32),
            pltpu.VMEM((B, B), F32),
            pltpu.VMEM((SR, B), F32),
        ],
    )
    return pl.pallas_call(
        _diff_attn_kernel,
        grid_spec=grid_spec,
        out_shape=jax.ShapeDtypeStruct((s, D_VW), BF16),
        compiler_params=_params(("arbitrary", "arbitrary")),
        name="diff_attn",
    )(qi, kj, proj, proj, proj, lam_vecs, table_flat)


def _out_ln_kernel(hm_ref, hd_ref, x_ref, g0_ref, b0_ref, wo1_ref, wo2_ref, g1_ref, b1_ref, wr_ref,
                   h1_ref, logit_ref):
    mix = _dot(hm_ref[...], wo1_ref[...]) + _dot(hd_ref[...], wo2_ref[...])
    h0 = _layer_norm(x_ref[...], g0_ref[...], b0_ref[...])
    h1 = _layer_norm(ALPHA * h0 + mix, g1_ref[...], b1_ref[...])
    h1_ref[...] = h1
    logit_ref[...] = _dot(h1.astype(BF16), wr_ref[...])


def _out_ln(hm, hd, x2, g0, b0, wo1, wo2, g1, b1, wr):
    s, d = x2.shape
    tm = min(OUT_TM, s)
    row = lambda i: (i, 0)
    fixed = lambda i: (0, 0)
    return pl.pallas_call(
        _out_ln_kernel,
        grid=(s // tm,),
        in_specs=[
            pl.BlockSpec((tm, M_V), row), pl.BlockSpec((tm, D_VW), row), pl.BlockSpec((tm, d), row),
            pl.BlockSpec((1, d), fixed), pl.BlockSpec((1, d), fixed),
            pl.BlockSpec((M_V, d), fixed), pl.BlockSpec((D_VW, d), fixed),
            pl.BlockSpec((1, d), fixed), pl.BlockSpec((1, d), fixed),
            pl.BlockSpec((d, N_EXPERTS), fixed),
        ],
        out_specs=[pl.BlockSpec((tm, d), row), pl.BlockSpec((tm, N_EXPERTS), row)],
        out_shape=[jax.ShapeDtypeStruct((s, d), F32), jax.ShapeDtypeStruct((s, N_EXPERTS), F32)],
        compiler_params=_params(("arbitrary",)),
        name="out_ln",
    )(hm, hd, x2, g0, b0, wo1, wo2, g1, b1, wr)


def _route_kernel(logit_ref, rb_ref, topi_ref, pos_ref, topw_ref, cnt_ref, carry_ref):
    tm = logit_ref.shape[0]
    gsz = N_EXPERTS // N_GROUPS

    @pl.when(pl.program_id(0) == 0)
    def _():
        carry_ref[...] = jnp.zeros_like(carry_ref)

    sc = jax.nn.sigmoid(logit_ref[...])
    sel = sc + rb_ref[...]
    lane = lax.broadcasted_iota(jnp.int32, sel.shape, 1)
    grp = lax.shift_right_logical(lane, int(math.log2(gsz)))
    ninf = -jnp.inf

    def first_argmax(v):
        m = jnp.max(v, -1, keepdims=True)
        idx = jnp.min(jnp.where(v == m, lane, N_EXPERTS), -1, keepdims=True)
        return m, idx

    gscore = []
    for g in range(N_GROUPS):
        v = jnp.where(grp == g, sel, ninf)
        m1, i1 = first_argmax(v)
        m2 = jnp.max(jnp.where(lane == i1, ninf, v), -1, keepdims=True)
        gscore.append(m1 + m2)
    emask = jnp.zeros(sel.shape, F32)
    for g in range(N_GROUPS):
        beaten = jnp.zeros(gscore[g].shape, F32)
        for o in range(N_GROUPS):
            if o == g:
                continue
            wins = (gscore[o] > gscore[g]) if o > g else (gscore[o] >= gscore[g])
            beaten = beaten + jnp.where(wins, 1.0, 0.0)
        keep = jnp.where(beaten < TOPK_GROUPS, 1.0, 0.0)
        emask = jnp.where(grp == g, keep, emask)

    masked = jnp.where(emask > 0.5, sel, ninf)
    onehots, idxs, ws = [], [], []
    for _ in range(TOP_K):
        _, idx = first_argmax(masked)
        oh = lane == idx
        onehots.append(oh)
        idxs.append(idx)
        ws.append(jnp.sum(jnp.where(oh, sc, 0.0), -1, keepdims=True))
        masked = jnp.where(oh, ninf, masked)
    wsum = ws[0]
    for w in ws[1:]:
        wsum = wsum + w

    chosen = jnp.zeros(sel.shape, F32)
    for oh in onehots:
        chosen = jnp.where(oh, 1.0, chosen)
    r_i = lax.broadcasted_iota(jnp.int32, (tm, tm), 0)
    c_i = lax.broadcasted_iota(jnp.int32, (tm, tm), 1)
    before = jnp.where(c_i < r_i, 1.0, 0.0).astype(BF16)
    rank = _dot(before, chosen.astype(BF16)) + carry_ref[...]
    carry_new = carry_ref[...] + jnp.sum(chosen, 0, keepdims=True)
    carry_ref[...] = carry_new
    cnt_ref[...] = carry_new

    lane_k = lax.broadcasted_iota(jnp.int32, (tm, TOP_K), 1)
    topi = jnp.zeros((tm, TOP_K), jnp.int32)
    pos = jnp.zeros((tm, TOP_K), jnp.int32)
    topw = jnp.zeros((tm, TOP_K), F32)
    for k in range(TOP_K):
        pk = jnp.sum(jnp.where(onehots[k], rank, 0.0), -1, keepdims=True)
        topi = jnp.where(lane_k == k, idxs[k], topi)
        pos = jnp.where(lane_k == k, pk.astype(jnp.int32), pos)
        topw = jnp.where(lane_k == k, ws[k] / wsum * ROUTED_SCALE, topw)
    topi_ref[...] = topi
    pos_ref[...] = pos
    topw_ref[...] = topw


def _route(logits, router_bias):
    t = logits.shape[0]
    tm = min(ROUTE_TM, t)
    row = lambda i: (i, 0)
    fixed = lambda i: (0, 0)
    return pl.pallas_call(
        _route_kernel,
        grid=(t // tm,),
        in_specs=[pl.BlockSpec((tm, N_EXPERTS), row), pl.BlockSpec((1, N_EXPERTS), fixed)],
        out_specs=[pl.BlockSpec((tm, TOP_K), row), pl.BlockSpec((tm, TOP_K), row),
                   pl.BlockSpec((tm, TOP_K), row), pl.BlockSpec((1, N_EXPERTS), fixed)],
        out_shape=[jax.ShapeDtypeStruct((t, TOP_K), jnp.int32), jax.ShapeDtypeStruct((t, TOP_K), jnp.int32),
                   jax.ShapeDtypeStruct((t, TOP_K), F32), jax.ShapeDtypeStruct((1, N_EXPERTS), F32)],
        scratch_shapes=[pltpu.VMEM((1, N_EXPERTS), F32)],
        compiler_params=_params(("arbitrary",)),
        name="route",
    )(logits, router_bias)


def _plan_kernel(cnt_ref, off_ref, vexp_ref, vblk_ref, nvis_ref):
    nv = vexp_ref.shape[0]
    tr = float(MOE_TR)
    cnt = jnp.broadcast_to(cnt_ref[...], (V7X_SUBLANES, N_EXPERTS))
    r_i = lax.broadcasted_iota(jnp.int32, (N_EXPERTS, N_EXPERTS), 0)
    c_i = lax.broadcasted_iota(jnp.int32, (N_EXPERTS, N_EXPERTS), 1)
    upper = jnp.where(r_i <= c_i, 1.0, 0.0).astype(BF16)

    def cumsum_lanes(a):
        a1, a2, a3 = _split3(a)
        return _dot(a1, upper) + _dot(a2, upper) + _dot(a3, upper)

    end = cumsum_lanes(cnt)
    start = end - cnt
    first_blk = jnp.floor(start / tr)
    last_blk = jnp.floor((end - 1.0) / tr)
    nvis_e = jnp.where(cnt > 0.0, last_blk - first_blk + 1.0, 0.0)
    vend = cumsum_lanes(nvis_e)
    vstart = vend - nvis_e
    off_ref[...] = start[0:1, :].astype(jnp.int32)
    nvis_ref[...] = vend[0:1, N_EXPERTS - 1:N_EXPERTS].astype(jnp.int32)

    v = lax.broadcasted_iota(jnp.int32, (nv, N_EXPERTS), 0).astype(F32)
    ve = jnp.broadcast_to(vend[0:1, :], (nv, N_EXPERTS))
    expert = jnp.sum(jnp.where(ve <= v, 1.0, 0.0), -1, keepdims=True)
    expert = jnp.minimum(expert, N_EXPERTS - 1.0)
    lane = lax.broadcasted_iota(jnp.int32, (nv, N_EXPERTS), 1).astype(F32)
    mine = lane == expert
    fb = jnp.sum(jnp.where(mine, jnp.broadcast_to(first_blk[0:1, :], (nv, N_EXPERTS)), 0.0), -1, keepdims=True)
    vs = jnp.sum(jnp.where(mine, jnp.broadcast_to(vstart[0:1, :], (nv, N_EXPERTS)), 0.0), -1, keepdims=True)
    vexp_ref[...] = expert.astype(jnp.int32)
    vblk_ref[...] = (fb + (v[:, 0:1] - vs)).astype(jnp.int32)


def _plan(counts, n_rows):
    nv = n_rows // MOE_TR + N_EXPERTS
    return pl.pallas_call(
        _plan_kernel,
        out_shape=[jax.ShapeDtypeStruct((1, N_EXPERTS), jnp.int32), jax.ShapeDtypeStruct((nv, 1), jnp.int32),
                   jax.ShapeDtypeStruct((nv, 1), jnp.int32), jax.ShapeDtypeStruct((1, 1), jnp.int32)],
        name="plan",
    )(counts)


def _dispatch_kernel(topi_ref, pos_ref, off_ref, x_ref, xs_ref, sem):
    tm = x_ref.shape[0]

    def issue(a, carry):
        d = off_ref[topi_ref[a]] + pos_ref[a]
        pltpu.make_async_copy(x_ref.at[pl.ds(lax.shift_right_logical(a, TOP_K_SHIFT), 1)], xs_ref.at[pl.ds(d, 1)], sem).start()
        return carry

    lax.fori_loop(0, tm * TOP_K, issue, 0, unroll=8)
    for _ in range(TOP_K):
        pltpu.make_async_copy(x_ref, xs_ref.at[pl.ds(0, tm)], sem).wait()


def _dispatch(topi_flat, pos_flat, off, h1):
    t, d = h1.shape
    tm = min(MOVE_TM, t)
    return pl.pallas_call(
        _dispatch_kernel,
        grid=(t // tm,),
        in_specs=[
            pl.BlockSpec((tm * TOP_K,), lambda i: (i,), memory_space=pltpu.SMEM),
            pl.BlockSpec((tm * TOP_K,), lambda i: (i,), memory_space=pltpu.SMEM),
            pl.BlockSpec(memory_space=pltpu.SMEM),
            pl.BlockSpec((tm, d), lambda i: (i, 0)),
        ],
        out_specs=pl.BlockSpec(memory_space=pl.ANY),
        out_shape=jax.ShapeDtypeStruct((t * TOP_K, d), F32),
        scratch_shapes=[pltpu.SemaphoreType.DMA(())],
        compiler_params=_params(("arbitrary",)),
        name="dispatch",
    )(topi_flat, pos_flat, off, h1)


def _experts_kernel(vexp_ref, vblk_ref, off_ref, nvis_ref, xs_ref, wg_ref, wu_ref, wd_ref, ys_ref):
    v = pl.program_id(0)
    tr = xs_ref.shape[0]

    @pl.when(v < nvis_ref[0])
    def _():
        e = vexp_ref[v]
        blk = vblk_ref[v]
        lo = off_ref[e]
        hi = off_ref[e + 1]
        row = blk * tr + lax.broadcasted_iota(jnp.int32, (tr, 1), 0)
        mine = (row >= lo) & (row < hi)
        xb = xs_ref[...].astype(BF16)
        gate = _dot(xb, wg_ref[...].astype(BF16))
        up = _dot(xb, wu_ref[...].astype(BF16))
        act = (gate * jax.nn.sigmoid(gate) * up).astype(BF16)
        y = jnp.where(mine, _dot(act, wd_ref[...].astype(BF16)), 0.0)
        first = jnp.logical_or(v == 0, vblk_ref[jnp.maximum(v - 1, 0)] != blk)

        @pl.when(first)
        def _():
            ys_ref[...] = y

        @pl.when(jnp.logical_not(first))
        def _():
            ys_ref[...] += y


def _experts(vexp, vblk, off_ext, nvis, xs, w_gate, w_up, w_down):
    n_rows, d = xs.shape
    nv = vexp.shape[0]

    def visit(v, nvis):
        return jnp.minimum(v, nvis[0] - 1)

    grid_spec = pltpu.PrefetchScalarGridSpec(
        num_scalar_prefetch=4,
        grid=(nv,),
        in_specs=[
            pl.BlockSpec((MOE_TR, d), lambda v, ve, vb, off, nvis: (vb[visit(v, nvis)], 0)),
            pl.BlockSpec((None, d, D_EXPERT), lambda v, ve, vb, off, nvis: (ve[visit(v, nvis)], 0, 0)),
            pl.BlockSpec((None, d, D_EXPERT), lambda v, ve, vb, off, nvis: (ve[visit(v, nvis)], 0, 0)),
            pl.BlockSpec((None, D_EXPERT, d), lambda v, ve, vb, off, nvis: (ve[visit(v, nvis)], 0, 0)),
        ],
        out_specs=pl.BlockSpec((MOE_TR, d), lambda v, ve, vb, off, nvis: (vb[visit(v, nvis)], 0)),
    )
    return pl.pallas_call(
        _experts_kernel,
        grid_spec=grid_spec,
        out_shape=jax.ShapeDtypeStruct((n_rows, d), F32),
        compiler_params=_params(("arbitrary",)),
        name="experts",
    )(vexp, vblk, off_ext, nvis, xs, w_gate, w_up, w_down)


def _combine_kernel(topi_ref, pos_ref, off_ref, topw_ref, h1_ref, ys_ref, sg_ref, su_ref, sd_ref,
                    g2_ref, b2_ref, o_ref, buf_ref, sem):
    tm = h1_ref.shape[0]

    def issue(a, carry):
        d = off_ref[topi_ref[a]] + pos_ref[a]
        pltpu.make_async_copy(ys_ref.at[pl.ds(d, 1)], buf_ref.at[a & (TOP_K - 1),pl.ds(lax.shift_right_logical(a, TOP_K_SHIFT), 1)], sem).start()
        return carry

    lax.fori_loop(0, tm * TOP_K, issue, 0, unroll=8)

    h1 = h1_ref[...]
    hb = h1.astype(BF16)
    gate = _dot(hb, sg_ref[...])
    up = _dot(hb, su_ref[...])
    y = _dot((gate * jax.nn.sigmoid(gate) * up).astype(BF16), sd_ref[...])

    for k in range(TOP_K):
        pltpu.make_async_copy(ys_ref.at[pl.ds(0, tm)], buf_ref.at[k], sem).wait()
    w = topw_ref[...]
    for k in range(TOP_K):
        y = y + buf_ref[k] * w[:, k:k + 1]
    o_ref[...] = _layer_norm(ALPHA * h1 + y, g2_ref[...], b2_ref[...])


def _combine(topi_flat, pos_flat, off, topw, h1, ys, sg, su, sd, g2, b2):
    t, d = h1.shape
    tm = min(MOVE_TM, t)
    row = lambda i: (i, 0)
    fixed = lambda i: (0, 0)
    return pl.pallas_call(
        _combine_kernel,
        grid=(t // tm,),
        in_specs=[
            pl.BlockSpec((tm * TOP_K,), lambda i: (i,), memory_space=pltpu.SMEM),
            pl.BlockSpec((tm * TOP_K,), lambda i: (i,), memory_space=pltpu.SMEM),
            pl.BlockSpec(memory_space=pltpu.SMEM),
            pl.BlockSpec((tm, TOP_K), row),
            pl.BlockSpec((tm, d), row),
            pl.BlockSpec(memory_space=pl.ANY),
            pl.BlockSpec((d, D_EXPERT), fixed), pl.BlockSpec((d, D_EXPERT), fixed),
            pl.BlockSpec((D_EXPERT, d), fixed),
            pl.BlockSpec((1, d), fixed), pl.BlockSpec((1, d), fixed),
        ],
        out_specs=pl.BlockSpec((tm, d), row),
        out_shape=jax.ShapeDtypeStruct((t, d), F32),
        scratch_shapes=[pltpu.VMEM((TOP_K, tm, d), F32), pltpu.SemaphoreType.DMA(())],
        compiler_params=_params(("arbitrary",)),
        name="combine",
    )(topi_flat, pos_flat, off, topw, h1, ys, sg, su, sd, g2, b2)


def kernel(x, ln_emb_g, ln_emb_b, w_in, conv_w, conv_b, b_igate, b_fgate, mlstm_norm_g, lambda_q1, lambda_k1,
           lambda_q2, lambda_k2, rel_bias, w_out, ln1_g, ln1_b, w_router, router_bias, w_gate, w_up, w_down,
           ws_gate, ws_up, ws_down, ln2_g, ln2_b):
    bsz, s, d = x.shape
    assert bsz == 1 and w_in.shape[0] == DEPTH == 1
    x2 = x.reshape(s, d)
    row = lambda a: a.reshape(1, -1).astype(F32)

    w = w_in[0]
    c0 = 2 * M_QK + 2 * M_V
    c1 = c0 + 2 * MLSTM_HEADS
    w_main = jnp.concatenate([w[:, :c0], w[:, c1:]], axis=1).astype(BF16)
    wi, wf = w[:, c0:c0 + MLSTM_HEADS], w[:, c0 + MLSTM_HEADS:c1]
    w_gates = jnp.zeros((d, MLSTM_HEADS, V7X_LANES), F32).at[:, :, 0].set(wi).at[:, :, 1].set(wf)
    w_gates = w_gates.reshape(d, MLSTM_HEADS * V7X_LANES).astype(BF16)
    w_gates_t = jnp.zeros((MLSTM_HEADS, V7X_SUBLANES, d), F32).at[:, 0].set(wi.T).at[:, 1].set(wf.T)
    w_gates_t = w_gates_t.reshape(MLSTM_HEADS * V7X_SUBLANES, d).astype(BF16)

    proj, gates, gates_t = _ln_proj(x2, row(ln_emb_g), row(ln_emb_b), w_main, w_gates, w_gates_t)

    hm = _mlstm(proj, gates, gates_t, conv_w[0], row(conv_b[0]), b_igate[0].astype(F32),
                b_fgate[0].astype(F32), row(mlstm_norm_g[0]))

    lam_vecs = jnp.stack([lambda_q1[0], lambda_k1[0], lambda_q2[0], lambda_k2[0]]).astype(F32)
    hd = _diff_attn(proj, lam_vecs, rel_bias.astype(F32).reshape(-1))

    wo = w_out[0].astype(BF16)
    h1, logits = _out_ln(hm, hd, x2, row(ln_emb_g), row(ln_emb_b), wo[:M_V], wo[M_V:], row(ln1_g[0]),
                         row(ln1_b[0]), w_router[0].astype(BF16))

    topi, pos, topw, counts = _route(logits, row(router_bias[0]))
    off, vexp, vblk, nvis = _plan(counts, s * TOP_K)
    off_ext = jnp.concatenate([off.reshape(-1), jnp.full((1,), s * TOP_K, jnp.int32)])
    topi_flat, pos_flat = topi.reshape(-1), pos.reshape(-1)

    xs = _dispatch(topi_flat, pos_flat, off_ext, h1)
    ys = _experts(vexp.reshape(-1), vblk.reshape(-1), off_ext, nvis.reshape(-1), xs, w_gate[0], w_up[0], w_down[0])
    out = _combine(topi_flat, pos_flat, off_ext, topw, h1, ys, ws_gate[0].astype(BF16), ws_up[0].astype(BF16),
                   ws_down[0].astype(BF16), row(ln2_g[0]), row(ln2_b[0]))
    return out.reshape(bsz, s, d)
```

```python
import functools
import math

import jax
import jax.numpy as jnp
from jax import lax
from jax.experimental import pallas as pl
from jax.experimental.pallas import tpu as pltpu

F32 = jnp.float32
BF16 = jnp.bfloat16

DEPTH = 1
MLSTM_HEADS = 4
MLSTM_DQK = 128
MLSTM_DV = 256
CONV_WIDTH = 4
DIFF_HEADS = 8
DIFF_DH = 64
DIFF_DV = 2 * DIFF_DH
REL_BUCKETS = 32
REL_MAX_DIST = 128
N_EXPERTS = 256
TOP_K = 8
N_GROUPS = 8
TOPK_GROUPS = 4
D_EXPERT = 512
ROUTED_SCALE = 2.5
LN_EPS = 1e-5
ALPHA = (2 * DEPTH) ** 0.25
LAM_INIT = 0.8 - 0.6 * math.exp(-0.3 * 0)

M_QK = MLSTM_HEADS * MLSTM_DQK
M_V = MLSTM_HEADS * MLSTM_DV
D_QK = DIFF_HEADS * 2 * DIFF_DH
D_VW = DIFF_HEADS * DIFF_DV
PROJ_W = 2 * M_QK + 2 * M_V + D_QK + D_VW

V7X_LANES = 128
V7X_SUBLANES = 8
V7X_VMEM_LIMIT_BYTES = 56 * 1024 * 1024

PROJ_TM = 512
PROJ_TN = 512
MLSTM_L = 256
ATT_B = 1024
ATT_STRIP = 256
OUT_TM = 512
ROUTE_TM = 256
MOE_TR = 256
MOVE_TM = 256

NEG = -1e30
LOG2E = 1.4426950408889634


def _params(semantics):
    return pltpu.CompilerParams(dimension_semantics=semantics, vmem_limit_bytes=V7X_VMEM_LIMIT_BYTES)


def _layer_norm(x, g, b):
    mu = jnp.mean(x, -1, keepdims=True)
    xc = x - mu
    var = jnp.mean(xc * xc, -1, keepdims=True)
    return xc * lax.rsqrt(var + LN_EPS) * g + b


def _dot(a, b):
    return jnp.dot(a, b, preferred_element_type=F32)


def _dot_nt(a, b):
    return lax.dot_general(a, b, (((1,), (1,)), ((), ())), preferred_element_type=F32)


def _split3(a):
    a1 = a.astype(BF16)
    r1 = a - a1.astype(F32)
    a2 = r1.astype(BF16)
    a3 = (r1 - a2.astype(F32)).astype(BF16)
    return a1, a2, a3


def _log_sigmoid(x):
    return jnp.minimum(x, 0.0) - jnp.log(1.0 + jnp.exp(-jnp.abs(x)))


def _ln_proj_kernel(x_ref, g_ref, b_ref, w_ref, wg_ref, wgt_ref, wkt_ref, o_ref, gates_ref, gatest_ref, kt_ref,
                    hb_ref):
    @pl.when(pl.program_id(1) == 0)
    def _():
        hb = _layer_norm(x_ref[...], g_ref[...], b_ref[...]).astype(BF16)
        hb_ref[...] = hb
        gates_ref[...] = _dot(hb, wg_ref[...])
        gatest_ref[...] = _dot_nt(wgt_ref[...], hb)
        kt_ref[...] = _dot_nt(wkt_ref[...], hb).astype(kt_ref.dtype)

    o_ref[...] = _dot(hb_ref[...], w_ref[...]).astype(o_ref.dtype)


def _ln_proj(x2, g, b, w_main, w_gates, w_gates_t, w_k_t):
    s, d = x2.shape
    tm, tn = min(PROJ_TM, s), PROJ_TN
    ng = w_gates.shape[1]
    fixed = lambda i, j: (0, 0)
    return pl.pallas_call(
        _ln_proj_kernel,
        grid=(s // tm, PROJ_W // tn),
        in_specs=[
            pl.BlockSpec((tm, d), lambda i, j: (i, 0)),
            pl.BlockSpec((1, d), fixed),
            pl.BlockSpec((1, d), fixed),
            pl.BlockSpec((d, tn), lambda i, j: (0, j)),
            pl.BlockSpec((d, ng), fixed),
            pl.BlockSpec((w_gates_t.shape[0], d), fixed),
            pl.BlockSpec((D_QK, d), fixed),
        ],
        out_specs=[
            pl.BlockSpec((tm, tn), lambda i, j: (i, j)),
            pl.BlockSpec((tm, ng), lambda i, j: (i, 0)),
            pl.BlockSpec((w_gates_t.shape[0], tm), lambda i, j: (0, i)),
            pl.BlockSpec((D_QK, tm), lambda i, j: (0, i)),
        ],
        out_shape=[
            jax.ShapeDtypeStruct((s, PROJ_W), BF16),
            jax.ShapeDtypeStruct((s, ng), F32),
            jax.ShapeDtypeStruct((w_gates_t.shape[0], s), F32),
            jax.ShapeDtypeStruct((D_QK, s), BF16),
        ],
        scratch_shapes=[pltpu.VMEM((tm, d), BF16)],
        compiler_params=_params(("arbitrary", "arbitrary")),
        name="ln_proj",
    )(x2, g, b, w_main, w_gates, w_gates_t, w_k_t)


def _mlstm_kernel(bi_ref, bf_ref, mq_ref, mk_ref, mv_ref, mo_ref, gates_ref, gatest_ref,
                  cw_ref, cb_ref, ng_ref, o_ref, c_ref, n_ref, m_ref, tail_ref):
    L = mq_ref.shape[0]
    dk, dv = MLSTM_DQK, MLSTM_DV

    @pl.when(pl.program_id(0) == 0)
    def _():
        c_ref[...] = jnp.zeros_like(c_ref)
        n_ref[...] = jnp.zeros_like(n_ref)
        m_ref[...] = jnp.zeros_like(m_ref)
        tail_ref[...] = jnp.zeros_like(tail_ref)

    u = jnp.concatenate([mq_ref[...], mk_ref[...]], axis=1).astype(F32)
    tail = tail_ref[...]
    row8 = lax.broadcasted_iota(jnp.int32, (V7X_SUBLANES, u.shape[1]), 0)
    conv = cb_ref[...] + cw_ref[CONV_WIDTH - 1:CONV_WIDTH, :] * u
    for back in range(1, CONV_WIDTH):
        ur = pltpu.roll(u, back, 0)
        head = jnp.where(row8 < back, pltpu.roll(tail, back, 0), ur[:V7X_SUBLANES])
        shifted = jnp.concatenate([head, ur[V7X_SUBLANES:]], axis=0)
        conv = conv + cw_ref[CONV_WIDTH - 1 - back:CONV_WIDTH - back, :] * shifted
    tail_ref[...] = u[L - V7X_SUBLANES:]
    qk = conv * jax.nn.sigmoid(conv)
    q_all = qk[:, :M_QK]
    k_all = qk[:, M_QK:] * (dk ** -0.5)

    r_i = lax.broadcasted_iota(jnp.int32, (L, L), 0)
    c_i = lax.broadcasted_iota(jnp.int32, (L, L), 1)
    causal = c_i <= r_i
    tril = jnp.where(causal, 1.0, 0.0).astype(BF16)
    triu = jnp.where(r_i <= c_i, 1.0, 0.0).astype(BF16)

    for h in range(MLSTM_HEADS):
        q = q_all[:, h * dk:(h + 1) * dk]
        k = k_all[:, h * dk:(h + 1) * dk]
        qb, kb = q.astype(BF16), k.astype(BF16)
        vb = mv_ref[:, h * dv:(h + 1) * dv]
        b_i, b_f = bi_ref[h], bf_ref[h]

        gblk = gates_ref[:, h * V7X_LANES:(h + 1) * V7X_LANES]
        i_col = gblk[:, 0:1] + b_i
        lf_blk = _log_sigmoid(gblk + b_f)
        f1, f2, f3 = _split3(lf_blk)
        bcum_blk = _dot(tril, f1) + _dot(tril, f2) + _dot(tril, f3)
        b_col = bcum_blk[:, 1:2]
        gt = gatest_ref[h * V7X_SUBLANES:(h + 1) * V7X_SUBLANES, :]
        i_row = gt[0:1, :] + b_i
        lf_rows = _log_sigmoid(gt + b_f)
        g1, g2, g3 = _split3(lf_rows)
        b_row = (_dot(g1, triu) + _dot(g2, triu) + _dot(g3, triu))[1:2, :]

        m_prev = m_ref[h, 0:1, 0:1]
        dmat = jnp.where(causal, b_col - b_row + i_row, NEG)
        inter = b_col + m_prev
        m_t = jnp.maximum(inter, jnp.max(dmat, -1, keepdims=True))
        wts = jnp.exp(dmat - m_t)
        g = jnp.exp(inter - m_t)
        sqk = _dot_nt(qb, kb) * wts
        c_prev = c_ref[h]
        n_prev = n_ref[h, 0:1, :]
        num = g * _dot(qb, c_prev.astype(BF16)) + _dot(sqk.astype(BF16), vb)
        den = g * jnp.sum(q * n_prev, -1, keepdims=True) + jnp.sum(sqk, -1, keepdims=True)
        hh = num / jnp.maximum(jnp.abs(den), jnp.exp(-m_t))

        b_last = b_col[L - 1:L, :]
        w_last_row = b_last - b_row + i_row
        m_new = jnp.maximum(b_last + m_prev, jnp.max(w_last_row, -1, keepdims=True))
        decay = jnp.exp(b_last + m_prev - m_new)
        ws_col = jnp.exp(b_last - b_col + i_col - m_new)
        kw = k * ws_col
        c_ref[h] = decay * c_prev + _dot(kw.T.astype(BF16), vb)
        n_ref[h, 0:1, :] = decay * n_prev + jnp.sum(kw, 0, keepdims=True)
        m_ref[h] = jnp.broadcast_to(m_new, m_ref.shape[1:])

        mu = jnp.mean(hh, -1, keepdims=True)
        hc = hh - mu
        var = jnp.mean(hc * hc, -1, keepdims=True)
        hn = hc * lax.rsqrt(var + LN_EPS) * ng_ref[:, h * dv:(h + 1) * dv]
        og = jax.nn.sigmoid(mo_ref[:, h * dv:(h + 1) * dv].astype(F32))
        o_ref[:, h * dv:(h + 1) * dv] = (hn * og).astype(o_ref.dtype)


def _mlstm(proj, gates, gates_t, conv_w, conv_b, b_i, b_f, norm_g):
    s = proj.shape[0]
    L = min(MLSTM_L, s)
    smem = pl.BlockSpec(memory_space=pltpu.SMEM)
    return pl.pallas_call(
        _mlstm_kernel,
        grid=(s // L,),
        in_specs=[
            smem, smem,
            pl.BlockSpec((L, M_QK), lambda c: (c, 0)),
            pl.BlockSpec((L, M_QK), lambda c: (c, 1)),
            pl.BlockSpec((L, M_V), lambda c: (c, 1)),
            pl.BlockSpec((L, M_V), lambda c: (c, 2)),
            pl.BlockSpec((L, gates.shape[1]), lambda c: (c, 0)),
            pl.BlockSpec((gates_t.shape[0], L), lambda c: (0, c)),
            pl.BlockSpec((CONV_WIDTH, 2 * M_QK), lambda c: (0, 0)),
            pl.BlockSpec((1, 2 * M_QK), lambda c: (0, 0)),
            pl.BlockSpec((1, M_V), lambda c: (0, 0)),
        ],
        out_specs=pl.BlockSpec((L, M_V), lambda c: (c, 0)),
        out_shape=jax.ShapeDtypeStruct((s, M_V), BF16),
        scratch_shapes=[
            pltpu.VMEM((MLSTM_HEADS, MLSTM_DQK, MLSTM_DV), F32),
            pltpu.VMEM((MLSTM_HEADS, V7X_SUBLANES, MLSTM_DQK), F32),
            pltpu.VMEM((MLSTM_HEADS, V7X_SUBLANES, V7X_LANES), F32),
            pltpu.VMEM((V7X_SUBLANES, 2 * M_QK), F32),
        ],
        compiler_params=_params(("arbitrary",)),
        name="mlstm",
    )(b_i, b_f, proj, proj, proj, proj, gates, gates_t, conv_w, conv_b, norm_g)


def _t5_bias_tile(table_ref, h, offset):
    n_t = V7X_LANES
    r_i = lax.broadcasted_iota(jnp.int32, (n_t, n_t), 0)
    c_i = lax.broadcasted_iota(jnp.int32, (n_t, n_t), 1)
    n = jnp.maximum(offset + r_i - c_i, 0)
    max_exact = REL_BUCKETS // 2
    large = max_exact + (jnp.log(jnp.maximum(n, 1).astype(F32) / max_exact)
                         / math.log(REL_MAX_DIST / max_exact) * (REL_BUCKETS - max_exact)).astype(jnp.int32)
    large = jnp.minimum(large, REL_BUCKETS - 1)
    bucket = jnp.where(n < max_exact, n, large)
    out = jnp.zeros((n_t, n_t), F32)
    for b in range(REL_BUCKETS):
        out = jnp.where(bucket == b, table_ref[b * DIFF_HEADS + h], out)
    return out * LOG2E


def _diff_attn_kernel(qi_ref, kj_ref, q_ref, kt_ref, v_ref, lam_ref, table_ref, o_ref,
                      q1_ref, q2_ref, vx_ref, m_ref, acc_ref, bdiag_ref, bsub_ref):
    h = pl.program_id(0)
    p = pl.program_id(1)
    i = qi_ref[p]
    j = kj_ref[p]
    B = q_ref.shape[0]
    SR = min(ATT_STRIP, B)
    n_strip = B // SR
    n_sub = B // V7X_LANES
    c_far = table_ref[(REL_BUCKETS - 1) * DIFF_HEADS + h] * LOG2E

    @pl.when(p == 0)
    def _():
        p0 = _t5_bias_tile(table_ref, h, 0)
        p0 = jnp.where(lax.broadcasted_iota(jnp.int32, p0.shape, 1) <= lax.broadcasted_iota(jnp.int32, p0.shape, 0),
                       p0, NEG)
        p1 = _t5_bias_tile(table_ref, h, V7X_LANES)
        far = jnp.full((V7X_LANES, V7X_LANES), c_far, F32)
        neg = jnp.full((V7X_LANES, V7X_LANES), NEG, F32)
        for a in range(n_sub):
            for b in range(n_sub):
                tile = p0 if a == b else p1 if a == b + 1 else far if a > b else neg
                bdiag_ref[a * V7X_LANES:(a + 1) * V7X_LANES, b * V7X_LANES:(b + 1) * V7X_LANES] = tile
        bsub_ref[...] = jnp.full(bsub_ref.shape, c_far, F32)
        bsub_ref[0:V7X_LANES, B - V7X_LANES:B] = p1

    @pl.when(j == 0)
    def _():
        q = q_ref[...].astype(F32) * ((DIFF_DH ** -0.5) * LOG2E)
        lane = lax.broadcasted_iota(jnp.int32, q.shape, 1)
        q1_ref[...] = jnp.where(lane < DIFF_DH, q, 0.0).astype(BF16)
        q2_ref[...] = jnp.where(lane >= DIFF_DH, q, 0.0).astype(BF16)
        m_ref[...] = jnp.full(m_ref.shape, NEG, F32)
        acc_ref[...] = jnp.zeros_like(acc_ref)

    vx_ref[:, 0:DIFF_DV] = v_ref[...]
    vx_ref[:, DIFF_DV:] = jnp.ones((B, DIFF_DV), BF16)

    def strip(r, kl, bias_tile=None):
        rows = slice(r * SR, (r + 1) * SR)
        kb = kt_ref[:, 0:kl]
        vb = vx_ref[0:kl, :]
        for a, qz_ref in enumerate((q1_ref, q2_ref)):
            s = _dot(qz_ref[rows, :], kb)
            m_old = m_ref[a, rows, :]
            if bias_tile is None:
                m_new = jnp.maximum(m_old, jnp.max(s, -1, keepdims=True) + c_far)
                pm = jnp.exp2(s - (m_new - c_far))
            else:
                s = s + bias_tile
                m_new = jnp.maximum(m_old, jnp.max(s, -1, keepdims=True))
                pm = jnp.exp2(s - m_new)
            alpha = jnp.exp2(m_old - m_new)
            acc_ref[a, rows, :] = alpha * acc_ref[a, rows, :] + _dot(pm.astype(BF16), vb)
            m_ref[a, rows, :] = m_new

    @pl.when(j == i)
    def _():
        for r in range(n_strip):
            kl = (r + 1) * SR
            strip(r, kl, bdiag_ref[r * SR:(r + 1) * SR, 0:kl])
        lq1, lk1, lq2, lk2 = (lam_ref[t:t + 1, :] for t in range(4))
        lam = (jnp.exp(jnp.sum(lq1 * lk1, -1, keepdims=True)) - jnp.exp(jnp.sum(lq2 * lk2, -1, keepdims=True))
               + LAM_INIT)
        o = (acc_ref[0, :, 0:DIFF_DV] / acc_ref[0, :, DIFF_DV:DIFF_DV + 1]
             - lam * (acc_ref[1, :, 0:DIFF_DV] / acc_ref[1, :, DIFF_DV:DIFF_DV + 1]))
        o = o * lax.rsqrt(jnp.mean(o * o, -1, keepdims=True) + LN_EPS)
        o_ref[...] = (o * (1.0 - LAM_INIT)).astype(o_ref.dtype)

    @pl.when(j == i - 1)
    def _():
        strip(0, B, bsub_ref[...])
        for r in range(1, n_strip):
            strip(r, B)

    @pl.when(j < i - 1)
    def _():
        for r in range(n_strip):
            strip(r, B)


def _diff_attn(proj, k_t, lam_vecs, table_flat):
    s = proj.shape[0]
    B = min(ATT_B, s)
    nq = s // B
    pairs = [(i, j) for i in range(nq) for j in range(i + 1)]
    qi = jnp.asarray([a for a, _ in pairs], jnp.int32)
    kj = jnp.asarray([b for _, b in pairs], jnp.int32)
    q_blk = (2 * M_QK + 2 * M_V) // V7X_LANES
    v_blk = q_blk + D_QK // V7X_LANES
    SR = min(ATT_STRIP, B)
    grid_spec = pltpu.PrefetchScalarGridSpec(
        num_scalar_prefetch=2,
        grid=(DIFF_HEADS, len(pairs)),
        in_specs=[
            pl.BlockSpec((B, V7X_LANES), lambda h, p, qi, kj: (qi[p], q_blk + h)),
            pl.BlockSpec((2 * DIFF_DH, B), lambda h, p, qi, kj: (h, kj[p])),
            pl.BlockSpec((B, V7X_LANES), lambda h, p, qi, kj: (kj[p], v_blk + h)),
            pl.BlockSpec((4, DIFF_DH), lambda h, p, qi, kj: (0, 0)),
            pl.BlockSpec(memory_space=pltpu.SMEM),
        ],
        out_specs=pl.BlockSpec((B, V7X_LANES), lambda h, p, qi, kj: (qi[p], h)),
        scratch_shapes=[
            pltpu.VMEM((B, V7X_LANES), BF16),
            pltpu.VMEM((B, V7X_LANES), BF16),
            pltpu.VMEM((B, 2 * DIFF_DV), BF16),
            pltpu.VMEM((2, B, 1), F32),
            pltpu.VMEM((2, B, 2 * DIFF_DV), F32),
            pltpu.VMEM((B, B), F32),
            pltpu.VMEM((SR, B), F32),
        ],
    )
    return pl.pallas_call(
        _diff_attn_kernel,
        grid_spec=grid_spec,
        out_shape=jax.ShapeDtypeStruct((s, D_VW), BF16),
        compiler_params=_params(("arbitrary", "arbitrary")),
        name="diff_attn",
    )(qi, kj, proj, k_t, proj, lam_vecs, table_flat)


def _out_ln_kernel(hm_ref, hd_ref, x_ref, g0_ref, b0_ref, wo1_ref, wo2_ref, g1_ref, b1_ref, wr_ref,
                   h1_ref, logit_ref):
    mix = _dot(hm_ref[...], wo1_ref[...]) + _dot(hd_ref[...], wo2_ref[...])
    h0 = _layer_norm(x_ref[...], g0_ref[...], b0_ref[...])
    h1 = _layer_norm(ALPHA * h0 + mix, g1_ref[...], b1_ref[...])
    h1_ref[...] = h1
    logit_ref[...] = _dot(h1.astype(BF16), wr_ref[...])


def _out_ln(hm, hd, x2, g0, b0, wo1, wo2, g1, b1, wr):
    s, d = x2.shape
    tm = min(OUT_TM, s)
    row = lambda i: (i, 0)
    fixed = lambda i: (0, 0)
    return pl.pallas_call(
        _out_ln_kernel,
        grid=(s // tm,),
        in_specs=[
            pl.BlockSpec((tm, M_V), row), pl.BlockSpec((tm, D_VW), row), pl.BlockSpec((tm, d), row),
            pl.BlockSpec((1, d), fixed), pl.BlockSpec((1, d), fixed),
            pl.BlockSpec((M_V, d), fixed), pl.BlockSpec((D_VW, d), fixed),
            pl.BlockSpec((1, d), fixed), pl.BlockSpec((1, d), fixed),
            pl.BlockSpec((d, N_EXPERTS), fixed),
        ],
        out_specs=[pl.BlockSpec((tm, d), row), pl.BlockSpec((tm, N_EXPERTS), row)],
        out_shape=[jax.ShapeDtypeStruct((s, d), F32), jax.ShapeDtypeStruct((s, N_EXPERTS), F32)],
        compiler_params=_params(("arbitrary",)),
        name="out_ln",
    )(hm, hd, x2, g0, b0, wo1, wo2, g1, b1, wr)


def _route_kernel(logit_ref, rb_ref, topi_ref, pos_ref, topw_ref, cnt_ref, carry_ref):
    tm = logit_ref.shape[0]
    gsz = N_EXPERTS // N_GROUPS

    @pl.when(pl.program_id(0) == 0)
    def _():
        carry_ref[...] = jnp.zeros_like(carry_ref)

    sc = jax.nn.sigmoid(logit_ref[...])
    sel = sc + rb_ref[...]
    lane = lax.broadcasted_iota(jnp.int32, sel.shape, 1)
    grp = lax.shift_right_logical(lane, int(math.log2(gsz)))
    ninf = -jnp.inf

    def first_argmax(v):
        m = jnp.max(v, -1, keepdims=True)
        idx = jnp.min(jnp.where(v == m, lane, N_EXPERTS), -1, keepdims=True)
        return m, idx

    gscore = []
    for g in range(N_GROUPS):
        v = jnp.where(grp == g, sel, ninf)
        m1, i1 = first_argmax(v)
        m2 = jnp.max(jnp.where(lane == i1, ninf, v), -1, keepdims=True)
        gscore.append(m1 + m2)
    emask = jnp.zeros(sel.shape, F32)
    for g in range(N_GROUPS):
        beaten = jnp.zeros(gscore[g].shape, F32)
        for o in range(N_GROUPS):
            if o == g:
                continue
            wins = (gscore[o] > gscore[g]) if o > g else (gscore[o] >= gscore[g])
            beaten = beaten + jnp.where(wins, 1.0, 0.0)
        keep = jnp.where(beaten < TOPK_GROUPS, 1.0, 0.0)
        emask = jnp.where(grp == g, keep, emask)

    masked = jnp.where(emask > 0.5, sel, ninf)
    onehots, idxs, ws = [], [], []
    for _ in range(TOP_K):
        _, idx = first_argmax(masked)
        oh = lane == idx
        onehots.append(oh)
        idxs.append(idx)
        ws.append(jnp.sum(jnp.where(oh, sc, 0.0), -1, keepdims=True))
        masked = jnp.where(oh, ninf, masked)
    wsum = ws[0]
    for w in ws[1:]:
        wsum = wsum + w

    chosen = jnp.zeros(sel.shape, F32)
    for oh in onehots:
        chosen = jnp.where(oh, 1.0, chosen)
    r_i = lax.broadcasted_iota(jnp.int32, (tm, tm), 0)
    c_i = lax.broadcasted_iota(jnp.int32, (tm, tm), 1)
    before = jnp.where(c_i < r_i, 1.0, 0.0).astype(BF16)
    rank = _dot(before, chosen.astype(BF16)) + carry_ref[...]
    carry_new = carry_ref[...] + jnp.sum(chosen, 0, keepdims=True)
    carry_ref[...] = carry_new
    cnt_ref[...] = carry_new

    lane_k = lax.broadcasted_iota(jnp.int32, (tm, TOP_K), 1)
    topi = jnp.zeros((tm, TOP_K), jnp.int32)
    pos = jnp.zeros((tm, TOP_K), jnp.int32)
    topw = jnp.zeros((tm, TOP_K), F32)
    for k in range(TOP_K):
        pk = jnp.sum(jnp.where(onehots[k], rank, 0.0), -1, keepdims=True)
        topi = jnp.where(lane_k == k, idxs[k], topi)
        pos = jnp.where(lane_k == k, pk.astype(jnp.int32), pos)
        topw = jnp.where(lane_k == k, ws[k] / wsum * ROUTED_SCALE, topw)
    topi_ref[...] = topi
    pos_ref[...] = pos
    topw_ref[...] = topw


def _route(logits, router_bias):
    t = logits.shape[0]
    tm = min(ROUTE_TM, t)
    row = lambda i: (i, 0)
    fixed = lambda i: (0, 0)
    return pl.pallas_call(
        _route_kernel,
        grid=(t // tm,),
        in_specs=[pl.BlockSpec((tm, N_EXPERTS), row), pl.BlockSpec((1, N_EXPERTS), fixed)],
        out_specs=[pl.BlockSpec((tm, TOP_K), row), pl.BlockSpec((tm, TOP_K), row),
                   pl.BlockSpec((tm, TOP_K), row), pl.BlockSpec((1, N_EXPERTS), fixed)],
        out_shape=[jax.ShapeDtypeStruct((t, TOP_K), jnp.int32), jax.ShapeDtypeStruct((t, TOP_K), jnp.int32),
                   jax.ShapeDtypeStruct((t, TOP_K), F32), jax.ShapeDtypeStruct((1, N_EXPERTS), F32)],
        scratch_shapes=[pltpu.VMEM((1, N_EXPERTS), F32)],
        compiler_params=_params(("arbitrary",)),
        name="route",
    )(logits, router_bias)


def _plan_kernel(cnt_ref, off_ref, vexp_ref, vblk_ref, vcnt_ref, nvis_ref):
    nv = vexp_ref.shape[0]
    tr = float(MOE_TR)
    cnt = jnp.broadcast_to(cnt_ref[...], (V7X_SUBLANES, N_EXPERTS))
    r_i = lax.broadcasted_iota(jnp.int32, (N_EXPERTS, N_EXPERTS), 0)
    c_i = lax.broadcasted_iota(jnp.int32, (N_EXPERTS, N_EXPERTS), 1)
    upper = jnp.where(r_i <= c_i, 1.0, 0.0).astype(BF16)

    def cumsum_lanes(a):
        a1, a2, a3 = _split3(a)
        return _dot(a1, upper) + _dot(a2, upper) + _dot(a3, upper)

    end = cumsum_lanes(cnt)
    start = end - cnt
    first_blk = jnp.floor(start / tr)
    last_blk = jnp.floor((end - 1.0) / tr)
    nvis_e = jnp.where(cnt > 0.0, last_blk - first_blk + 1.0, 0.0)
    vend = cumsum_lanes(nvis_e)
    vstart = vend - nvis_e
    off_ref[...] = start[0:1, :].astype(jnp.int32)
    nvis_ref[...] = vend[0:1, N_EXPERTS - 1:N_EXPERTS].astype(jnp.int32)

    v = lax.broadcasted_iota(jnp.int32, (nv, N_EXPERTS), 0).astype(F32)
    ve = jnp.broadcast_to(vend[0:1, :], (nv, N_EXPERTS))
    expert = jnp.sum(jnp.where(ve <= v, 1.0, 0.0), -1, keepdims=True)
    expert = jnp.minimum(expert, N_EXPERTS - 1.0)
    lane = lax.broadcasted_iota(jnp.int32, (nv, N_EXPERTS), 1).astype(F32)
    mine = lane == expert
    fb = jnp.sum(jnp.where(mine, jnp.broadcast_to(first_blk[0:1, :], (nv, N_EXPERTS)), 0.0), -1, keepdims=True)
    vs = jnp.sum(jnp.where(mine, jnp.broadcast_to(vstart[0:1, :], (nv, N_EXPERTS)), 0.0), -1, keepdims=True)
    vc = jnp.sum(jnp.where(mine, jnp.broadcast_to(nvis_e[0:1, :], (nv, N_EXPERTS)), 0.0), -1, keepdims=True)
    vexp_ref[...] = expert.astype(jnp.int32)
    vblk_ref[...] = (fb + (v[:, 0:1] - vs)).astype(jnp.int32)
    vcnt_ref[...] = vc.astype(jnp.int32)


def _plan(counts, n_rows):
    nv = n_rows // MOE_TR + N_EXPERTS
    return pl.pallas_call(
        _plan_kernel,
        out_shape=[jax.ShapeDtypeStruct((1, N_EXPERTS), jnp.int32), jax.ShapeDtypeStruct((nv, 1), jnp.int32),
                   jax.ShapeDtypeStruct((nv, 1), jnp.int32), jax.ShapeDtypeStruct((nv, 1), jnp.int32),
                   jax.ShapeDtypeStruct((1, 1), jnp.int32)],
        name="plan",
    )(counts)


def _dispatch_kernel(topi_ref, pos_ref, off_ref, x_ref, xs_ref, sem):
    tm = x_ref.shape[0]

    def issue(t, carry):
        for k in range(TOP_K):
            a = t * TOP_K + k
            d = off_ref[topi_ref[a]] + pos_ref[a]
            pltpu.make_async_copy(x_ref.at[pl.ds(t, 1)], xs_ref.at[pl.ds(d, 1)], sem).start(priority=k % 2)
        return carry

    lax.fori_loop(0, tm, issue, 0)
    for _ in range(TOP_K):
        pltpu.make_async_copy(x_ref, xs_ref.at[pl.ds(0, tm)], sem).wait()


def _dispatch(topi_flat, pos_flat, off, h1):
    t, d = h1.shape
    tm = min(MOVE_TM, t)
    return pl.pallas_call(
        _dispatch_kernel,
        grid=(t // tm,),
        in_specs=[
            pl.BlockSpec((tm * TOP_K,), lambda i: (i,), memory_space=pltpu.SMEM),
            pl.BlockSpec((tm * TOP_K,), lambda i: (i,), memory_space=pltpu.SMEM),
            pl.BlockSpec(memory_space=pltpu.SMEM),
            pl.BlockSpec((tm, d), lambda i: (i, 0)),
        ],
        out_specs=pl.BlockSpec(memory_space=pl.ANY),
        out_shape=jax.ShapeDtypeStruct((t * TOP_K, d), F32),
        scratch_shapes=[pltpu.SemaphoreType.DMA(())],
        compiler_params=_params(("arbitrary",)),
        name="dispatch",
    )(topi_flat, pos_flat, off, h1)


def _experts_kernel(vexp_ref, vblk_ref, vcnt_ref, off_ref, nvis_ref, xs_ref, wg_hbm, wu_hbm, wd_hbm, ys_ref,
                    wg_buf, wu_buf, wd_buf, slot_ref, sem):
    v = pl.program_id(0)
    tr = xs_ref.shape[0]
    nvis = nvis_ref[0]

    def weight_copies(e, slot):
        return (pltpu.make_async_copy(wg_hbm.at[e], wg_buf.at[slot], sem.at[slot, 0]),
                pltpu.make_async_copy(wu_hbm.at[e], wu_buf.at[slot], sem.at[slot, 1]),
                pltpu.make_async_copy(wd_hbm.at[e], wd_buf.at[slot], sem.at[slot, 2]))

    @pl.when(v < nvis)
    def _():
        e = vexp_ref[v]
        blk = vblk_ref[v]
        first_of_expert = jnp.logical_or(v == 0, vexp_ref[jnp.maximum(v - 1, 0)] != e)

        @pl.when(v == 0)
        def _():
            slot_ref[0] = 0
            for c in weight_copies(e, 0):
                c.start()

        @pl.when(jnp.logical_and(first_of_expert, v > 0))
        def _():
            slot_ref[0] = 1 - slot_ref[0]

        slot = slot_ref[0]

        @pl.when(first_of_expert)
        def _():
            for c in weight_copies(e, slot):
                c.wait()
            nxt = v + vcnt_ref[v]

            @pl.when(nxt < nvis)
            def _():
                for c in weight_copies(vexp_ref[jnp.minimum(nxt, vexp_ref.shape[0] - 1)], 1 - slot):
                    c.start()

        lo = off_ref[e]
        hi = off_ref[e + 1]
        row = blk * tr + lax.broadcasted_iota(jnp.int32, (tr, 1), 0)
        mine = (row >= lo) & (row < hi)
        xb = xs_ref[...].astype(BF16)
        gate = _dot(xb, wg_buf[slot].astype(BF16))
        up = _dot(xb, wu_buf[slot].astype(BF16))
        act = (gate * jax.nn.sigmoid(gate) * up).astype(BF16)
        y = jnp.where(mine, _dot(act, wd_buf[slot].astype(BF16)), 0.0)
        first_of_block = jnp.logical_or(v == 0, vblk_ref[jnp.maximum(v - 1, 0)] != blk)

        @pl.when(first_of_block)
        def _():
            ys_ref[...] = y

        @pl.when(jnp.logical_not(first_of_block))
        def _():
            ys_ref[...] += y


def _experts(vexp, vblk, vcnt, off_ext, nvis, xs, w_gate, w_up, w_down):
    n_rows, d = xs.shape
    nv = vexp.shape[0]

    def block_of(v, ve, vb, vc, off, nvis):
        return (vb[jnp.minimum(v, nvis[0] - 1)], 0)

    grid_spec = pltpu.PrefetchScalarGridSpec(
        num_scalar_prefetch=5,
        grid=(nv,),
        in_specs=[
            pl.BlockSpec((MOE_TR, d), block_of),
            pl.BlockSpec(memory_space=pl.ANY),
            pl.BlockSpec(memory_space=pl.ANY),
            pl.BlockSpec(memory_space=pl.ANY),
        ],
        out_specs=pl.BlockSpec((MOE_TR, d), block_of),
        scratch_shapes=[
            pltpu.VMEM((2, d, D_EXPERT), F32),
            pltpu.VMEM((2, d, D_EXPERT), F32),
            pltpu.VMEM((2, D_EXPERT, d), F32),
            pltpu.SMEM((1,), jnp.int32),
            pltpu.SemaphoreType.DMA((2, 3)),
        ],
    )
    return pl.pallas_call(
        _experts_kernel,
        grid_spec=grid_spec,
        out_shape=jax.ShapeDtypeStruct((n_rows, d), F32),
        compiler_params=_params(("arbitrary",)),
        name="experts",
    )(vexp, vblk, vcnt, off_ext, nvis, xs, w_gate, w_up, w_down)


def _combine_kernel(topi_ref, pos_ref, off_ref, topw_ref, h1_ref, ys_ref, sg_ref, su_ref, sd_ref,
                    g2_ref, b2_ref, o_ref, buf_ref, sem):
    tm = h1_ref.shape[0]

    def issue(t, carry):
        for k in range(TOP_K):
            a = t * TOP_K + k
            d = off_ref[topi_ref[a]] + pos_ref[a]
            pltpu.make_async_copy(ys_ref.at[pl.ds(d, 1)], buf_ref.at[k, pl.ds(t, 1)], sem).start(priority=k % 2)
        return carry

    lax.fori_loop(0, tm, issue, 0)

    h1 = h1_ref[...]
    hb = h1.astype(BF16)
    gate = _dot(hb, sg_ref[...])
    up = _dot(hb, su_ref[...])
    y = _dot((gate * jax.nn.sigmoid(gate) * up).astype(BF16), sd_ref[...])

    for k in range(TOP_K):
        pltpu.make_async_copy(ys_ref.at[pl.ds(0, tm)], buf_ref.at[k], sem).wait()
    w = topw_ref[...]
    for k in range(TOP_K):
        y = y + buf_ref[k] * w[:, k:k + 1]
    o_ref[...] = _layer_norm(ALPHA * h1 + y, g2_ref[...], b2_ref[...])


def _combine(topi_flat, pos_flat, off, topw, h1, ys, sg, su, sd, g2, b2):
    t, d = h1.shape
    tm = min(MOVE_TM, t)
    row = lambda i: (i, 0)
    fixed = lambda i: (0, 0)
    return pl.pallas_call(
        _combine_kernel,
        grid=(t // tm,),
        in_specs=[
            pl.BlockSpec((tm * TOP_K,), lambda i: (i,), memory_space=pltpu.SMEM),
            pl.BlockSpec((tm * TOP_K,), lambda i: (i,), memory_space=pltpu.SMEM),
            pl.BlockSpec(memory_space=pltpu.SMEM),
            pl.BlockSpec((tm, TOP_K), row),
            pl.BlockSpec((tm, d), row),
            pl.BlockSpec(memory_space=pl.ANY),
            pl.BlockSpec((d, D_EXPERT), fixed), pl.BlockSpec((d, D_EXPERT), fixed),
            pl.BlockSpec((D_EXPERT, d), fixed),
            pl.BlockSpec((1, d), fixed), pl.BlockSpec((1, d), fixed),
        ],
        out_specs=pl.BlockSpec((tm, d), row),
        out_shape=jax.ShapeDtypeStruct((t, d), F32),
        scratch_shapes=[pltpu.VMEM((TOP_K, tm, d), F32), pltpu.SemaphoreType.DMA(())],
        compiler_params=_params(("arbitrary",)),
        name="combine",
    )(topi_flat, pos_flat, off, topw, h1, ys, sg, su, sd, g2, b2)


def kernel(x, ln_emb_g, ln_emb_b, w_in, conv_w, conv_b, b_igate, b_fgate, mlstm_norm_g, lambda_q1, lambda_k1,
           lambda_q2, lambda_k2, rel_bias, w_out, ln1_g, ln1_b, w_router, router_bias, w_gate, w_up, w_down,
           ws_gate, ws_up, ws_down, ln2_g, ln2_b):
    bsz, s, d = x.shape
    assert bsz == 1 and w_in.shape[0] == DEPTH == 1
    x2 = x.reshape(s, d)
    row = lambda a: a.reshape(1, -1).astype(F32)

    w = w_in[0]
    c0 = 2 * M_QK + 2 * M_V
    c1 = c0 + 2 * MLSTM_HEADS
    c2 = c1 + D_QK
    c3 = c2 + D_QK
    w_main = jnp.concatenate([w[:, :c0], w[:, c1:c2], w[:, c3:]], axis=1).astype(BF16)
    w_k_t = w[:, c2:c3].T.astype(BF16)
    wi, wf = w[:, c0:c0 + MLSTM_HEADS], w[:, c0 + MLSTM_HEADS:c1]
    w_gates = jnp.zeros((d, MLSTM_HEADS, V7X_LANES), F32).at[:, :, 0].set(wi).at[:, :, 1].set(wf)
    w_gates = w_gates.reshape(d, MLSTM_HEADS * V7X_LANES).astype(BF16)
    w_gates_t = jnp.zeros((MLSTM_HEADS, V7X_SUBLANES, d), F32).at[:, 0].set(wi.T).at[:, 1].set(wf.T)
    w_gates_t = w_gates_t.reshape(MLSTM_HEADS * V7X_SUBLANES, d).astype(BF16)

    proj, gates, gates_t, k_t = _ln_proj(x2, row(ln_emb_g), row(ln_emb_b), w_main, w_gates, w_gates_t, w_k_t)

    hm = _mlstm(proj, gates, gates_t, conv_w[0], row(conv_b[0]), b_igate[0].astype(F32),
                b_fgate[0].astype(F32), row(mlstm_norm_g[0]))

    lam_vecs = jnp.stack([lambda_q1[0], lambda_k1[0], lambda_q2[0], lambda_k2[0]]).astype(F32)
    hd = _diff_attn(proj, k_t, lam_vecs, rel_bias.astype(F32).reshape(-1))

    wo = w_out[0].astype(BF16)
    h1, logits = _out_ln(hm, hd, x2, row(ln_emb_g), row(ln_emb_b), wo[:M_V], wo[M_V:], row(ln1_g[0]),
                         row(ln1_b[0]), w_router[0].astype(BF16))

    topi, pos, topw, counts = _route(logits, row(router_bias[0]))
    off, vexp, vblk, vcnt, nvis = _plan(counts, s * TOP_K)
    off_ext = jnp.concatenate([off.reshape(-1), jnp.full((1,), s * TOP_K, jnp.int32)])
    topi_flat, pos_flat = topi.reshape(-1), pos.reshape(-1)

    xs = _dispatch(topi_flat, pos_flat, off_ext, h1)
    ys = _experts(vexp.reshape(-1), vblk.reshape(-1), vcnt.reshape(-1), off_ext, nvis.reshape(-1), xs, w_gate[0],
                  w_up[0], w_down[0])
    out = _combine(topi_flat, pos_flat, off_ext, topw, h1, ys, ws_gate[0].astype(BF16), ws_up[0].astype(BF16),
                   ws_down[0].astype(BF16), row(ln2_g[0]), row(ln2_b[0]))
    return out.reshape(bsz, s, d)
```

```python
import functools
import math

import jax
import jax.numpy as jnp
from jax import lax
from jax.experimental import pallas as pl
from jax.experimental.pallas import tpu as pltpu

F32 = jnp.float32
BF16 = jnp.bfloat16

DEPTH = 1
MLSTM_HEADS = 4
MLSTM_DQK = 128
MLSTM_DV = 256
CONV_WIDTH = 4
DIFF_HEADS = 8
DIFF_DH = 64
DIFF_DV = 2 * DIFF_DH
REL_BUCKETS = 32
REL_MAX_DIST = 128
N_EXPERTS = 256
TOP_K = 8
N_GROUPS = 8
TOPK_GROUPS = 4
D_EXPERT = 512
ROUTED_SCALE = 2.5
LN_EPS = 1e-5
ALPHA = (2 * DEPTH) ** 0.25
LAM_INIT = 0.8 - 0.6 * math.exp(-0.3 * 0)

M_QK = MLSTM_HEADS * MLSTM_DQK
M_V = MLSTM_HEADS * MLSTM_DV
D_QK = DIFF_HEADS * 2 * DIFF_DH
D_VW = DIFF_HEADS * DIFF_DV
PROJ_W = 2 * M_QK + 2 * M_V + D_QK + D_VW

V7X_LANES = 128
V7X_SUBLANES = 8
V7X_VMEM_LIMIT_BYTES = 56 * 1024 * 1024

PROJ_TM = 1024
PROJ_TN = 512
MLSTM_L = 256
ATT_B = 1024
ATT_STRIP = 128
ATT_KCHUNK = 1024
OUT_TM = 512
ROUTE_TM = 256
MOE_TR = 256
MOVE_TM = 256
PLAN_TB = 512

NEG = -1e30
LOG2E = 1.4426950408889634


def _params(semantics):
    return pltpu.CompilerParams(dimension_semantics=semantics, vmem_limit_bytes=V7X_VMEM_LIMIT_BYTES)


def _layer_norm(x, g, b):
    mu = jnp.mean(x, -1, keepdims=True)
    xc = x - mu
    var = jnp.mean(xc * xc, -1, keepdims=True)
    return xc * lax.rsqrt(var + LN_EPS) * g + b


def _dot(a, b):
    return jnp.dot(a, b, preferred_element_type=F32)


def _dot_nt(a, b):
    return lax.dot_general(a, b, (((1,), (1,)), ((), ())), preferred_element_type=F32)


def _split3(a):
    a1 = a.astype(BF16)
    r1 = a - a1.astype(F32)
    a2 = r1.astype(BF16)
    a3 = (r1 - a2.astype(F32)).astype(BF16)
    return a1, a2, a3


def _log_sigmoid(x):
    return jnp.minimum(x, 0.0) - jnp.log(1.0 + jnp.exp(-jnp.abs(x)))


def _ln_proj_kernel(x_ref, g_ref, b_ref, w_ref, wg_ref, wgt_ref, wkt_ref, o_ref, gates_ref, gatest_ref, kt_ref,
                    hb_ref):
    @pl.when(pl.program_id(1) == 0)
    def _():
        hb = _layer_norm(x_ref[...], g_ref[...], b_ref[...]).astype(BF16)
        hb_ref[...] = hb
        gates_ref[...] = _dot(hb, wg_ref[...])
        gatest_ref[...] = _dot_nt(wgt_ref[...], hb)
        kt_ref[...] = _dot_nt(wkt_ref[...], hb).astype(kt_ref.dtype)

    o_ref[...] = _dot(hb_ref[...], w_ref[...]).astype(o_ref.dtype)


def _ln_proj(x2, g, b, w_main, w_gates, w_gates_t, w_k_t):
    s, d = x2.shape
    tm, tn = min(PROJ_TM, s), PROJ_TN
    ng = w_gates.shape[1]
    fixed = lambda i, j: (0, 0)
    return pl.pallas_call(
        _ln_proj_kernel,
        grid=(s // tm, PROJ_W // tn),
        in_specs=[
            pl.BlockSpec((tm, d), lambda i, j: (i, 0)),
            pl.BlockSpec((1, d), fixed),
            pl.BlockSpec((1, d), fixed),
            pl.BlockSpec((d, tn), lambda i, j: (0, j)),
            pl.BlockSpec((d, ng), fixed),
            pl.BlockSpec((w_gates_t.shape[0], d), fixed),
            pl.BlockSpec((D_QK, d), fixed),
        ],
        out_specs=[
            pl.BlockSpec((tm, tn), lambda i, j: (i, j)),
            pl.BlockSpec((tm, ng), lambda i, j: (i, 0)),
            pl.BlockSpec((w_gates_t.shape[0], tm), lambda i, j: (0, i)),
            pl.BlockSpec((D_QK, tm), lambda i, j: (0, i)),
        ],
        out_shape=[
            jax.ShapeDtypeStruct((s, PROJ_W), BF16),
            jax.ShapeDtypeStruct((s, ng), F32),
            jax.ShapeDtypeStruct((w_gates_t.shape[0], s), F32),
            jax.ShapeDtypeStruct((D_QK, s), BF16),
        ],
        scratch_shapes=[pltpu.VMEM((tm, d), BF16)],
        compiler_params=_params(("arbitrary", "arbitrary")),
        name="ln_proj",
    )(x2, g, b, w_main, w_gates, w_gates_t, w_k_t)


def _mlstm_kernel(bi_ref, bf_ref, mq_ref, mk_ref, mv_ref, mo_ref, gates_ref, gatest_ref,
                  cw_ref, cb_ref, ng_ref, o_ref, c_ref, n_ref, m_ref, tail_ref):
    L = mq_ref.shape[0]
    dk, dv = MLSTM_DQK, MLSTM_DV

    @pl.when(pl.program_id(0) == 0)
    def _():
        c_ref[...] = jnp.zeros_like(c_ref)
        n_ref[...] = jnp.zeros_like(n_ref)
        m_ref[...] = jnp.zeros_like(m_ref)
        tail_ref[...] = jnp.zeros_like(tail_ref)

    u = jnp.concatenate([mq_ref[...], mk_ref[...]], axis=1).astype(F32)
    tail = tail_ref[...]
    row8 = lax.broadcasted_iota(jnp.int32, (V7X_SUBLANES, u.shape[1]), 0)
    conv = cb_ref[...] + cw_ref[CONV_WIDTH - 1:CONV_WIDTH, :] * u
    for back in range(1, CONV_WIDTH):
        ur = pltpu.roll(u, back, 0)
        head = jnp.where(row8 < back, pltpu.roll(tail, back, 0), ur[:V7X_SUBLANES])
        shifted = jnp.concatenate([head, ur[V7X_SUBLANES:]], axis=0)
        conv = conv + cw_ref[CONV_WIDTH - 1 - back:CONV_WIDTH - back, :] * shifted
    tail_ref[...] = u[L - V7X_SUBLANES:]
    qk = conv * jax.nn.sigmoid(conv)
    q_all = qk[:, :M_QK]
    k_all = qk[:, M_QK:] * (dk ** -0.5)

    r_i = lax.broadcasted_iota(jnp.int32, (L, L), 0)
    c_i = lax.broadcasted_iota(jnp.int32, (L, L), 1)
    causal = c_i <= r_i
    tril = jnp.where(causal, 1.0, 0.0).astype(BF16)
    triu = jnp.where(r_i <= c_i, 1.0, 0.0).astype(BF16)

    for h in range(MLSTM_HEADS):
        q = q_all[:, h * dk:(h + 1) * dk]
        k = k_all[:, h * dk:(h + 1) * dk]
        qb, kb = q.astype(BF16), k.astype(BF16)
        vb = mv_ref[:, h * dv:(h + 1) * dv]
        b_i, b_f = bi_ref[h], bf_ref[h]

        gblk = gates_ref[:, h * V7X_LANES:(h + 1) * V7X_LANES]
        i_col = gblk[:, 0:1] + b_i
        lf_blk = _log_sigmoid(gblk + b_f)
        f1, f2, f3 = _split3(lf_blk)
        bcum_blk = _dot(tril, f1) + _dot(tril, f2) + _dot(tril, f3)
        b_col = bcum_blk[:, 1:2]
        gt = gatest_ref[h * V7X_SUBLANES:(h + 1) * V7X_SUBLANES, :]
        i_row = gt[0:1, :] + b_i
        lf_rows = _log_sigmoid(gt + b_f)
        g1, g2, g3 = _split3(lf_rows)
        b_row = (_dot(g1, triu) + _dot(g2, triu) + _dot(g3, triu))[1:2, :]

        m_prev = m_ref[h, 0:1, 0:1]
        dmat = jnp.where(causal, b_col - b_row + i_row, NEG)
        inter = b_col + m_prev
        m_t = jnp.maximum(inter, jnp.max(dmat, -1, keepdims=True))
        wts = jnp.exp(dmat - m_t)
        g = jnp.exp(inter - m_t)
        sqk = _dot_nt(qb, kb) * wts
        c_prev = c_ref[h]
        n_prev = n_ref[h, 0:1, :]
        num = g * _dot(qb, c_prev.astype(BF16)) + _dot(sqk.astype(BF16), vb)
        den = g * jnp.sum(q * n_prev, -1, keepdims=True) + jnp.sum(sqk, -1, keepdims=True)
        hh = num / jnp.maximum(jnp.abs(den), jnp.exp(-m_t))

        b_last = b_col[L - 1:L, :]
        w_last_row = b_last - b_row + i_row
        m_new = jnp.maximum(b_last + m_prev, jnp.max(w_last_row, -1, keepdims=True))
        decay = jnp.exp(b_last + m_prev - m_new)
        ws_col = jnp.exp(b_last - b_col + i_col - m_new)
        kw = k * ws_col
        c_ref[h] = decay * c_prev + _dot(kw.T.astype(BF16), vb)
        n_ref[h, 0:1, :] = decay * n_prev + jnp.sum(kw, 0, keepdims=True)
        m_ref[h] = jnp.broadcast_to(m_new, m_ref.shape[1:])

        mu = jnp.mean(hh, -1, keepdims=True)
        hc = hh - mu
        var = jnp.mean(hc * hc, -1, keepdims=True)
        hn = hc * lax.rsqrt(var + LN_EPS) * ng_ref[:, h * dv:(h + 1) * dv]
        og = jax.nn.sigmoid(mo_ref[:, h * dv:(h + 1) * dv].astype(F32))
        o_ref[:, h * dv:(h + 1) * dv] = (hn * og).astype(o_ref.dtype)


def _mlstm(proj, gates, gates_t, conv_w, conv_b, b_i, b_f, norm_g):
    s = proj.shape[0]
    L = min(MLSTM_L, s)
    smem = pl.BlockSpec(memory_space=pltpu.SMEM)
    return pl.pallas_call(
        _mlstm_kernel,
        grid=(s // L,),
        in_specs=[
            smem, smem,
            pl.BlockSpec((L, M_QK), lambda c: (c, 0)),
            pl.BlockSpec((L, M_QK), lambda c: (c, 1)),
            pl.BlockSpec((L, M_V), lambda c: (c, 1)),
            pl.BlockSpec((L, M_V), lambda c: (c, 2)),
            pl.BlockSpec((L, gates.shape[1]), lambda c: (c, 0)),
            pl.BlockSpec((gates_t.shape[0], L), lambda c: (0, c)),
            pl.BlockSpec((CONV_WIDTH, 2 * M_QK), lambda c: (0, 0)),
            pl.BlockSpec((1, 2 * M_QK), lambda c: (0, 0)),
            pl.BlockSpec((1, M_V), lambda c: (0, 0)),
        ],
        out_specs=pl.BlockSpec((L, M_V), lambda c: (c, 0)),
        out_shape=jax.ShapeDtypeStruct((s, M_V), BF16),
        scratch_shapes=[
            pltpu.VMEM((MLSTM_HEADS, MLSTM_DQK, MLSTM_DV), F32),
            pltpu.VMEM((MLSTM_HEADS, V7X_SUBLANES, MLSTM_DQK), F32),
            pltpu.VMEM((MLSTM_HEADS, V7X_SUBLANES, V7X_LANES), F32),
            pltpu.VMEM((V7X_SUBLANES, 2 * M_QK), F32),
        ],
        compiler_params=_params(("arbitrary",)),
        name="mlstm",
    )(b_i, b_f, proj, proj, proj, proj, gates, gates_t, conv_w, conv_b, norm_g)


def _t5_bias_tile(table_ref, h, offset):
    n_t = V7X_LANES
    r_i = lax.broadcasted_iota(jnp.int32, (n_t, n_t), 0)
    c_i = lax.broadcasted_iota(jnp.int32, (n_t, n_t), 1)
    n = jnp.maximum(offset + r_i - c_i, 0)
    max_exact = REL_BUCKETS // 2
    large = max_exact + (jnp.log(jnp.maximum(n, 1).astype(F32) / max_exact)
                         / math.log(REL_MAX_DIST / max_exact) * (REL_BUCKETS - max_exact)).astype(jnp.int32)
    large = jnp.minimum(large, REL_BUCKETS - 1)
    bucket = jnp.where(n < max_exact, n, large)
    out = jnp.zeros((n_t, n_t), F32)
    for b in range(REL_BUCKETS):
        out = jnp.where(bucket == b, table_ref[b * DIFF_HEADS + h], out)
    return out * LOG2E


def _diff_attn_kernel(qi_ref, kj_ref, q_ref, kt_ref, v_ref, lam_ref, table_ref, o_ref,
                      q1_ref, q2_ref, vx_ref, m_ref, acc_ref, bdiag_ref, bsub_ref):
    h = pl.program_id(0)
    p = pl.program_id(1)
    i = qi_ref[p]
    j = kj_ref[p]
    B = q_ref.shape[0]
    SR = min(ATT_STRIP, B)
    n_strip = B // SR
    n_sub = B // V7X_LANES
    c_far = table_ref[(REL_BUCKETS - 1) * DIFF_HEADS + h] * LOG2E

    @pl.when(p == 0)
    def _():
        p0 = _t5_bias_tile(table_ref, h, 0)
        p0 = jnp.where(lax.broadcasted_iota(jnp.int32, p0.shape, 1) <= lax.broadcasted_iota(jnp.int32, p0.shape, 0),
                       p0, NEG)
        p1 = _t5_bias_tile(table_ref, h, V7X_LANES)
        far = jnp.full((V7X_LANES, V7X_LANES), c_far, F32)
        neg = jnp.full((V7X_LANES, V7X_LANES), NEG, F32)
        for a in range(n_sub):
            for b in range(n_sub):
                tile = p0 if a == b else p1 if a == b + 1 else far if a > b else neg
                bdiag_ref[a * V7X_LANES:(a + 1) * V7X_LANES, b * V7X_LANES:(b + 1) * V7X_LANES] = tile
        bsub_ref[...] = jnp.full(bsub_ref.shape, c_far, F32)
        bsub_ref[0:V7X_LANES, B - V7X_LANES:B] = p1

    @pl.when(j == 0)
    def _():
        q = q_ref[...].astype(F32) * ((DIFF_DH ** -0.5) * LOG2E)
        lane = lax.broadcasted_iota(jnp.int32, q.shape, 1)
        q1_ref[...] = jnp.where(lane < DIFF_DH, q, 0.0).astype(BF16)
        q2_ref[...] = jnp.where(lane >= DIFF_DH, q, 0.0).astype(BF16)
        m_ref[...] = jnp.full(m_ref.shape, NEG, F32)
        acc_ref[...] = jnp.zeros_like(acc_ref)

    vx_ref[:, 0:DIFF_DV] = v_ref[...]
    vx_ref[:, DIFF_DV:] = jnp.ones((B, DIFF_DV), BF16)

    def strip(r, kl, bias_tile=None):
        rows = slice(r * SR, (r + 1) * SR)
        for c0 in range(0, kl, ATT_KCHUNK):
            cols = slice(c0, min(c0 + ATT_KCHUNK, kl))
            kb = kt_ref[:, cols]
            vb = vx_ref[cols, :]
            for a, qz_ref in enumerate((q1_ref, q2_ref)):
                s = _dot(qz_ref[rows, :], kb)
                m_old = m_ref[a, rows, :]
                if bias_tile is None:
                    m_new = jnp.maximum(m_old, jnp.max(s, -1, keepdims=True) + c_far)
                    pm = jnp.exp2(s - (m_new - c_far))
                else:
                    s = s + bias_tile[:, cols]
                    m_new = jnp.maximum(m_old, jnp.max(s, -1, keepdims=True))
                    pm = jnp.exp2(s - m_new)
                alpha = jnp.exp2(m_old - m_new)
                acc_ref[a, rows, :] = alpha * acc_ref[a, rows, :] + _dot(pm.astype(BF16), vb)
                m_ref[a, rows, :] = m_new

    @pl.when(j == i)
    def _():
        for r in range(n_strip):
            kl = (r + 1) * SR
            strip(r, kl, bdiag_ref[r * SR:(r + 1) * SR, 0:kl])
        lq1, lk1, lq2, lk2 = (lam_ref[t:t + 1, :] for t in range(4))
        lam = (jnp.exp(jnp.sum(lq1 * lk1, -1, keepdims=True)) - jnp.exp(jnp.sum(lq2 * lk2, -1, keepdims=True))
               + LAM_INIT)
        o = (acc_ref[0, :, 0:DIFF_DV] / acc_ref[0, :, DIFF_DV:DIFF_DV + 1]
             - lam * (acc_ref[1, :, 0:DIFF_DV] / acc_ref[1, :, DIFF_DV:DIFF_DV + 1]))
        o = o * lax.rsqrt(jnp.mean(o * o, -1, keepdims=True) + LN_EPS)
        o_ref[...] = (o * (1.0 - LAM_INIT)).astype(o_ref.dtype)

    @pl.when(j == i - 1)
    def _():
        strip(0, B, bsub_ref[...])
        for r in range(1, n_strip):
            strip(r, B)

    @pl.when(j < i - 1)
    def _():
        for r in range(n_strip):
            strip(r, B)


def _diff_attn(proj, k_t, lam_vecs, table_flat):
    s = proj.shape[0]
    B = min(ATT_B, s)
    nq = s // B
    pairs = [(i, j) for i in range(nq) for j in range(i + 1)]
    qi = jnp.asarray([a for a, _ in pairs], jnp.int32)
    kj = jnp.asarray([b for _, b in pairs], jnp.int32)
    q_blk = (2 * M_QK + 2 * M_V) // V7X_LANES
    v_blk = q_blk + D_QK // V7X_LANES
    SR = min(ATT_STRIP, B)
    grid_spec = pltpu.PrefetchScalarGridSpec(
        num_scalar_prefetch=2,
        grid=(DIFF_HEADS, len(pairs)),
        in_specs=[
            pl.BlockSpec((B, V7X_LANES), lambda h, p, qi, kj: (qi[p], q_blk + h)),
            pl.BlockSpec((2 * DIFF_DH, B), lambda h, p, qi, kj: (h, kj[p])),
            pl.BlockSpec((B, V7X_LANES), lambda h, p, qi, kj: (kj[p], v_blk + h)),
            pl.BlockSpec((4, DIFF_DH), lambda h, p, qi, kj: (0, 0)),
            pl.BlockSpec(memory_space=pltpu.SMEM),
        ],
        out_specs=pl.BlockSpec((B, V7X_LANES), lambda h, p, qi, kj: (qi[p], h)),
        scratch_shapes=[
            pltpu.VMEM((B, V7X_LANES), BF16),
            pltpu.VMEM((B, V7X_LANES), BF16),
            pltpu.VMEM((B, 2 * DIFF_DV), BF16),
            pltpu.VMEM((2, B, 1), F32),
            pltpu.VMEM((2, B, 2 * DIFF_DV), F32),
            pltpu.VMEM((B, B), F32),
            pltpu.VMEM((SR, B), F32),
        ],
    )
    return pl.pallas_call(
        _diff_attn_kernel,
        grid_spec=grid_spec,
        out_shape=jax.ShapeDtypeStruct((s, D_VW), BF16),
        compiler_params=_params(("arbitrary", "arbitrary")),
        name="diff_attn",
    )(qi, kj, proj, k_t, proj, lam_vecs, table_flat)


def _out_ln_kernel(hm_ref, hd_ref, x_ref, g0_ref, b0_ref, wo1_ref, wo2_ref, g1_ref, b1_ref, wr_ref,
                   h1_ref, logit_ref):
    mix = _dot(hm_ref[...], wo1_ref[...]) + _dot(hd_ref[...], wo2_ref[...])
    h0 = _layer_norm(x_ref[...], g0_ref[...], b0_ref[...])
    h1 = _layer_norm(ALPHA * h0 + mix, g1_ref[...], b1_ref[...])
    h1_ref[...] = h1
    logit_ref[...] = _dot_nt(wr_ref[...], h1.astype(BF16))


def _out_ln(hm, hd, x2, g0, b0, wo1, wo2, g1, b1, wr):
    s, d = x2.shape
    tm = min(OUT_TM, s)
    row = lambda i: (i, 0)
    fixed = lambda i: (0, 0)
    return pl.pallas_call(
        _out_ln_kernel,
        grid=(s // tm,),
        in_specs=[
            pl.BlockSpec((tm, M_V), row), pl.BlockSpec((tm, D_VW), row), pl.BlockSpec((tm, d), row),
            pl.BlockSpec((1, d), fixed), pl.BlockSpec((1, d), fixed),
            pl.BlockSpec((M_V, d), fixed), pl.BlockSpec((D_VW, d), fixed),
            pl.BlockSpec((1, d), fixed), pl.BlockSpec((1, d), fixed),
            pl.BlockSpec((N_EXPERTS, d), fixed),
        ],
        out_specs=[pl.BlockSpec((tm, d), row), pl.BlockSpec((N_EXPERTS, tm), lambda i: (0, i))],
        out_shape=[jax.ShapeDtypeStruct((s, d), F32), jax.ShapeDtypeStruct((N_EXPERTS, s), F32)],
        compiler_params=_params(("arbitrary",)),
        name="out_ln",
    )(hm, hd, x2, g0, b0, wo1, wo2, g1, b1, wr)


def _route_kernel(logit_ref, rb_ref, topi_ref, pos_ref, topw_ref, cnt_ref, carry_ref):
    n_e, tm = logit_ref.shape
    gsz = n_e // N_GROUPS
    ninf = -jnp.inf

    @pl.when(pl.program_id(0) == 0)
    def _():
        carry_ref[...] = jnp.zeros_like(carry_ref)

    sc = jax.nn.sigmoid(logit_ref[...])
    sel = sc + rb_ref[...]

    sel3 = sel.reshape(N_GROUPS, gsz, tm)
    in_grp = lax.broadcasted_iota(jnp.int32, sel3.shape, 1)
    m1 = jnp.max(sel3, 1, keepdims=True)
    i1 = jnp.min(jnp.where(sel3 == m1, in_grp, gsz), 1, keepdims=True)
    m2 = jnp.max(jnp.where(in_grp == i1, ninf, sel3), 1, keepdims=True)
    gscore = (m1 + m2).reshape(N_GROUPS, tm)
    gid = lax.broadcasted_iota(jnp.int32, gscore.shape, 0)
    beaten = jnp.zeros(gscore.shape, F32)
    for o in range(1, N_GROUPS):
        other = pltpu.roll(gscore, o, 0)
        wins = (other > gscore) | ((other == gscore) & (gid >= o))
        beaten = beaten + jnp.where(wins, 1.0, 0.0)
    keep = jnp.where(beaten < TOPK_GROUPS, 1.0, 0.0)
    keep3 = jnp.broadcast_to(keep.reshape(N_GROUPS, 1, tm), sel3.shape)
    masked = jnp.where(keep3 > 0.5, sel3, ninf).reshape(n_e, tm)

    eid = lax.broadcasted_iota(jnp.int32, (n_e, tm), 0)
    onehots, idxs, ws = [], [], []
    for _ in range(TOP_K):
        m = jnp.max(masked, 0, keepdims=True)
        idx = jnp.min(jnp.where(masked == m, eid, n_e), 0, keepdims=True)
        oh = eid == idx
        onehots.append(oh)
        idxs.append(idx)
        ws.append(jnp.sum(jnp.where(oh, sc, 0.0), 0, keepdims=True))
        masked = jnp.where(oh, ninf, masked)
    wsum = ws[0]
    for w in ws[1:]:
        wsum = wsum + w

    chosen = jnp.zeros((n_e, tm), F32)
    for oh in onehots:
        chosen = jnp.where(oh, 1.0, chosen)
    r_i = lax.broadcasted_iota(jnp.int32, (tm, tm), 0)
    c_i = lax.broadcasted_iota(jnp.int32, (tm, tm), 1)
    before = jnp.where(r_i < c_i, 1.0, 0.0).astype(BF16)
    rank = _dot(chosen.astype(BF16), before) + carry_ref[...]
    carry_new = carry_ref[...] + jnp.sum(chosen, 1, keepdims=True)
    carry_ref[...] = carry_new
    cnt_ref[...] = carry_new

    kid = lax.broadcasted_iota(jnp.int32, (TOP_K, tm), 0)
    topi = jnp.zeros((TOP_K, tm), jnp.int32)
    pos = jnp.zeros((TOP_K, tm), jnp.int32)
    topw = jnp.zeros((TOP_K, tm), F32)
    for k in range(TOP_K):
        pk = jnp.sum(jnp.where(onehots[k], rank, 0.0), 0, keepdims=True)
        topi = jnp.where(kid == k, idxs[k], topi)
        pos = jnp.where(kid == k, pk.astype(jnp.int32), pos)
        topw = jnp.where(kid == k, ws[k] / wsum * ROUTED_SCALE, topw)
    topi_ref[...] = topi
    pos_ref[...] = pos
    topw_ref[...] = topw


def _route(logits_t, router_bias_col):
    n_e, t = logits_t.shape
    tm = min(ROUTE_TM, t)
    col = lambda i: (0, i)
    fixed = lambda i: (0, 0)
    return pl.pallas_call(
        _route_kernel,
        grid=(t // tm,),
        in_specs=[pl.BlockSpec((n_e, tm), col), pl.BlockSpec((n_e, 1), fixed)],
        out_specs=[pl.BlockSpec((TOP_K, tm), col), pl.BlockSpec((TOP_K, tm), col),
                   pl.BlockSpec((TOP_K, tm), col), pl.BlockSpec((n_e, 1), fixed)],
        out_shape=[jax.ShapeDtypeStruct((TOP_K, t), jnp.int32), jax.ShapeDtypeStruct((TOP_K, t), jnp.int32),
                   jax.ShapeDtypeStruct((TOP_K, t), F32), jax.ShapeDtypeStruct((n_e, 1), F32)],
        scratch_shapes=[pltpu.VMEM((n_e, 1), F32)],
        compiler_params=_params(("arbitrary",)),
        name="route",
    )(logits_t, router_bias_col)


def _plan_kernel(cnt_ref, cntc_ref, topi_ref, pos_ref, off_ref, vexp_ref, vblk_ref, vcnt_ref, nvis_ref, dest_ref):
    nv = vexp_ref.shape[0]
    tr = float(MOE_TR)
    cnt = jnp.broadcast_to(cnt_ref[...], (V7X_SUBLANES, N_EXPERTS))
    r_i = lax.broadcasted_iota(jnp.int32, (N_EXPERTS, N_EXPERTS), 0)
    c_i = lax.broadcasted_iota(jnp.int32, (N_EXPERTS, N_EXPERTS), 1)
    upper = jnp.where(r_i <= c_i, 1.0, 0.0).astype(BF16)

    def cumsum_lanes(a):
        a1, a2, a3 = _split3(a)
        return _dot(a1, upper) + _dot(a2, upper) + _dot(a3, upper)

    end = cumsum_lanes(cnt)
    start = end - cnt
    first_blk = jnp.floor(start / tr)
    last_blk = jnp.floor((end - 1.0) / tr)
    nvis_e = jnp.where(cnt > 0.0, last_blk - first_blk + 1.0, 0.0)
    vend = cumsum_lanes(nvis_e)
    vstart = vend - nvis_e
    off_ref[...] = start[0:1, :].astype(jnp.int32)
    nvis_ref[...] = vend[0:1, N_EXPERTS - 1:N_EXPERTS].astype(jnp.int32)

    v = lax.broadcasted_iota(jnp.int32, (nv, N_EXPERTS), 0).astype(F32)
    ve = jnp.broadcast_to(vend[0:1, :], (nv, N_EXPERTS))
    expert = jnp.sum(jnp.where(ve <= v, 1.0, 0.0), -1, keepdims=True)
    expert = jnp.minimum(expert, N_EXPERTS - 1.0)
    lane = lax.broadcasted_iota(jnp.int32, (nv, N_EXPERTS), 1).astype(F32)
    mine = lane == expert
    fb = jnp.sum(jnp.where(mine, jnp.broadcast_to(first_blk[0:1, :], (nv, N_EXPERTS)), 0.0), -1, keepdims=True)
    vs = jnp.sum(jnp.where(mine, jnp.broadcast_to(vstart[0:1, :], (nv, N_EXPERTS)), 0.0), -1, keepdims=True)
    vc = jnp.sum(jnp.where(mine, jnp.broadcast_to(nvis_e[0:1, :], (nv, N_EXPERTS)), 0.0), -1, keepdims=True)
    vexp_ref[...] = expert.astype(jnp.int32)
    vblk_ref[...] = (fb + (v[:, 0:1] - vs)).astype(jnp.int32)
    vcnt_ref[...] = vc.astype(jnp.int32)

    lower = jnp.where(c_i < r_i, 1.0, 0.0).astype(BF16)
    c1, c2, c3 = _split3(jnp.broadcast_to(cntc_ref[...], (N_EXPERTS, V7X_LANES)))
    start_col = (_dot(lower, c1) + _dot(lower, c2) + _dot(lower, c3))[:, 0:1]
    n_tok = topi_ref.shape[1]
    tb = min(PLAN_TB, n_tok)
    eid = lax.broadcasted_iota(jnp.int32, (N_EXPERTS, tb), 0)
    kid = lax.broadcasted_iota(jnp.int32, (TOP_K, tb), 0)

    def dest_block(i, carry):
        cols = pl.ds(pl.multiple_of(i * tb, tb), tb)
        ti = topi_ref[:, cols]
        first_row = jnp.zeros((TOP_K, tb), F32)
        for k in range(TOP_K):
            fr = jnp.sum(jnp.where(eid == ti[k:k + 1, :], start_col, 0.0), 0, keepdims=True)
            first_row = jnp.where(kid == k, fr, first_row)
        dest_ref[:, cols] = first_row.astype(jnp.int32) + pos_ref[:, cols]
        return carry

    lax.fori_loop(0, n_tok // tb, dest_block, 0)


def _plan(counts_row, counts_col, topi_t, pos_t):
    n_rows = topi_t.shape[0] * topi_t.shape[1]
    nv = n_rows // MOE_TR + N_EXPERTS
    return pl.pallas_call(
        _plan_kernel,
        out_shape=[jax.ShapeDtypeStruct((1, N_EXPERTS), jnp.int32), jax.ShapeDtypeStruct((nv, 1), jnp.int32),
                   jax.ShapeDtypeStruct((nv, 1), jnp.int32), jax.ShapeDtypeStruct((nv, 1), jnp.int32),
                   jax.ShapeDtypeStruct((1, 1), jnp.int32), jax.ShapeDtypeStruct(topi_t.shape, jnp.int32)],
        compiler_params=pltpu.CompilerParams(vmem_limit_bytes=V7X_VMEM_LIMIT_BYTES),
        name="plan",
    )(counts_row, counts_col, topi_t, pos_t)


def _dispatch_kernel(dest_ref, x_ref, xs_ref, sem):
    tm = x_ref.shape[0]

    def issue(t, carry):
        for k in range(TOP_K):
            d = dest_ref[t * TOP_K + k]
            pltpu.make_async_copy(x_ref.at[pl.ds(t, 1)], xs_ref.at[pl.ds(d, 1)], sem).start(priority=k % 2)
        return carry

    lax.fori_loop(0, tm, issue, 0)
    for _ in range(TOP_K):
        pltpu.make_async_copy(x_ref, xs_ref.at[pl.ds(0, tm)], sem).wait()


def _dispatch(dest_flat, h1):
    t, d = h1.shape
    tm = min(MOVE_TM, t)
    return pl.pallas_call(
        _dispatch_kernel,
        grid=(t // tm,),
        in_specs=[
            pl.BlockSpec((tm * TOP_K,), lambda i: (i,), memory_space=pltpu.SMEM),
            pl.BlockSpec((tm, d), lambda i: (i, 0)),
        ],
        out_specs=pl.BlockSpec(memory_space=pl.ANY),
        out_shape=jax.ShapeDtypeStruct((t * TOP_K, d), F32),
        scratch_shapes=[pltpu.SemaphoreType.DMA(())],
        compiler_params=_params(("arbitrary",)),
        name="dispatch",
    )(dest_flat, h1)


def _experts_kernel(vexp_ref, vblk_ref, vcnt_ref, off_ref, nvis_ref, xs_ref, wg_hbm, wu_hbm, wd_hbm, ys_ref,
                    wg_buf, wu_buf, wd_buf, slot_ref, sem):
    v = pl.program_id(0)
    tr = xs_ref.shape[0]
    nvis = nvis_ref[0]

    def weight_copies(e, slot):
        return (pltpu.make_async_copy(wg_hbm.at[e], wg_buf.at[slot], sem.at[slot, 0]),
                pltpu.make_async_copy(wu_hbm.at[e], wu_buf.at[slot], sem.at[slot, 1]),
                pltpu.make_async_copy(wd_hbm.at[e], wd_buf.at[slot], sem.at[slot, 2]))

    @pl.when(v < nvis)
    def _():
        e = vexp_ref[v]
        blk = vblk_ref[v]
        first_of_expert = jnp.logical_or(v == 0, vexp_ref[jnp.maximum(v - 1, 0)] != e)

        @pl.when(v == 0)
        def _():
            slot_ref[0] = 0
            for c in weight_copies(e, 0):
                c.start()

        @pl.when(jnp.logical_and(first_of_expert, v > 0))
        def _():
            slot_ref[0] = 1 - slot_ref[0]

        slot = slot_ref[0]

        @pl.when(first_of_expert)
        def _():
            for c in weight_copies(e, slot):
                c.wait()
            nxt = v + vcnt_ref[v]

            @pl.when(nxt < nvis)
            def _():
                for c in weight_copies(vexp_ref[jnp.minimum(nxt, vexp_ref.shape[0] - 1)], 1 - slot):
                    c.start()

        lo = off_ref[e]
        hi = off_ref[e + 1]
        row = blk * tr + lax.broadcasted_iota(jnp.int32, (tr, 1), 0)
        mine = (row >= lo) & (row < hi)
        xb = xs_ref[...].astype(BF16)
        gate = _dot(xb, wg_buf[slot].astype(BF16))
        up = _dot(xb, wu_buf[slot].astype(BF16))
        act = (gate * jax.nn.sigmoid(gate) * up).astype(BF16)
        y = jnp.where(mine, _dot(act, wd_buf[slot].astype(BF16)), 0.0)
        first_of_block = jnp.logical_or(v == 0, vblk_ref[jnp.maximum(v - 1, 0)] != blk)

        @pl.when(first_of_block)
        def _():
            ys_ref[...] = y

        @pl.when(jnp.logical_not(first_of_block))
        def _():
            ys_ref[...] += y


def _experts(vexp, vblk, vcnt, off_ext, nvis, xs, w_gate, w_up, w_down):
    n_rows, d = xs.shape
    nv = vexp.shape[0]

    def block_of(v, ve, vb, vc, off, nvis):
        return (vb[jnp.minimum(v, nvis[0] - 1)], 0)

    grid_spec = pltpu.PrefetchScalarGridSpec(
        num_scalar_prefetch=5,
        grid=(nv,),
        in_specs=[
            pl.BlockSpec((MOE_TR, d), block_of),
            pl.BlockSpec(memory_space=pl.ANY),
            pl.BlockSpec(memory_space=pl.ANY),
            pl.BlockSpec(memory_space=pl.ANY),
        ],
        out_specs=pl.BlockSpec((MOE_TR, d), block_of),
        scratch_shapes=[
            pltpu.VMEM((2, d, D_EXPERT), F32),
            pltpu.VMEM((2, d, D_EXPERT), F32),
            pltpu.VMEM((2, D_EXPERT, d), F32),
            pltpu.SMEM((1,), jnp.int32),
            pltpu.SemaphoreType.DMA((2, 3)),
        ],
    )
    return pl.pallas_call(
        _experts_kernel,
        grid_spec=grid_spec,
        out_shape=jax.ShapeDtypeStruct((n_rows, d), F32),
        compiler_params=_params(("arbitrary",)),
        name="experts",
    )(vexp, vblk, vcnt, off_ext, nvis, xs, w_gate, w_up, w_down)


def _combine_kernel(dest_ref, topw_ref, h1_ref, ys_ref, sg_ref, su_ref, sd_ref, g2_ref, b2_ref, o_ref,
                    buf_ref, sem):
    tm = h1_ref.shape[0]

    def issue(t, carry):
        for k in range(TOP_K):
            d = dest_ref[t * TOP_K + k]
            pltpu.make_async_copy(ys_ref.at[pl.ds(d, 1)], buf_ref.at[k, pl.ds(t, 1)], sem).start(priority=k % 2)
        return carry

    lax.fori_loop(0, tm, issue, 0)

    h1 = h1_ref[...]
    hb = h1.astype(BF16)
    gate = _dot(hb, sg_ref[...])
    up = _dot(hb, su_ref[...])
    y = _dot((gate * jax.nn.sigmoid(gate) * up).astype(BF16), sd_ref[...])

    for k in range(TOP_K):
        pltpu.make_async_copy(ys_ref.at[pl.ds(0, tm)], buf_ref.at[k], sem).wait()
    w = topw_ref[...]
    for k in range(TOP_K):
        y = y + buf_ref[k] * w[:, k:k + 1]
    o_ref[...] = _layer_norm(ALPHA * h1 + y, g2_ref[...], b2_ref[...])


def _combine(dest_flat, topw, h1, ys, sg, su, sd, g2, b2):
    t, d = h1.shape
    tm = min(MOVE_TM, t)
    row = lambda i: (i, 0)
    fixed = lambda i: (0, 0)
    return pl.pallas_call(
        _combine_kernel,
        grid=(t // tm,),
        in_specs=[
            pl.BlockSpec((tm * TOP_K,), lambda i: (i,), memory_space=pltpu.SMEM),
            pl.BlockSpec((tm, TOP_K), row),
            pl.BlockSpec((tm, d), row),
            pl.BlockSpec(memory_space=pl.ANY),
            pl.BlockSpec((d, D_EXPERT), fixed), pl.BlockSpec((d, D_EXPERT), fixed),
            pl.BlockSpec((D_EXPERT, d), fixed),
            pl.BlockSpec((1, d), fixed), pl.BlockSpec((1, d), fixed),
        ],
        out_specs=pl.BlockSpec((tm, d), row),
        out_shape=jax.ShapeDtypeStruct((t, d), F32),
        scratch_shapes=[pltpu.VMEM((TOP_K, tm, d), F32), pltpu.SemaphoreType.DMA(())],
        compiler_params=_params(("arbitrary",)),
        name="combine",
    )(dest_flat, topw, h1, ys, sg, su, sd, g2, b2)


def kernel(x, ln_emb_g, ln_emb_b, w_in, conv_w, conv_b, b_igate, b_fgate, mlstm_norm_g, lambda_q1, lambda_k1,
           lambda_q2, lambda_k2, rel_bias, w_out, ln1_g, ln1_b, w_router, router_bias, w_gate, w_up, w_down,
           ws_gate, ws_up, ws_down, ln2_g, ln2_b):
    bsz, s, d = x.shape
    assert bsz == 1 and w_in.shape[0] == DEPTH == 1
    x2 = x.reshape(s, d)
    row = lambda a: a.reshape(1, -1).astype(F32)

    w = w_in[0]
    c0 = 2 * M_QK + 2 * M_V
    c1 = c0 + 2 * MLSTM_HEADS
    c2 = c1 + D_QK
    c3 = c2 + D_QK
    w_main = jnp.concatenate([w[:, :c0], w[:, c1:c2], w[:, c3:]], axis=1).astype(BF16)
    w_k_t = w[:, c2:c3].T.astype(BF16)
    wi, wf = w[:, c0:c0 + MLSTM_HEADS], w[:, c0 + MLSTM_HEADS:c1]
    w_gates = jnp.zeros((d, MLSTM_HEADS, V7X_LANES), F32).at[:, :, 0].set(wi).at[:, :, 1].set(wf)
    w_gates = w_gates.reshape(d, MLSTM_HEADS * V7X_LANES).astype(BF16)
    w_gates_t = jnp.zeros((MLSTM_HEADS, V7X_SUBLANES, d), F32).at[:, 0].set(wi.T).at[:, 1].set(wf.T)
    w_gates_t = w_gates_t.reshape(MLSTM_HEADS * V7X_SUBLANES, d).astype(BF16)

    proj, gates, gates_t, k_t = _ln_proj(x2, row(ln_emb_g), row(ln_emb_b), w_main, w_gates, w_gates_t, w_k_t)

    hm = _mlstm(proj, gates, gates_t, conv_w[0], row(conv_b[0]), b_igate[0].astype(F32),
                b_fgate[0].astype(F32), row(mlstm_norm_g[0]))

    lam_vecs = jnp.stack([lambda_q1[0], lambda_k1[0], lambda_q2[0], lambda_k2[0]]).astype(F32)
    hd = _diff_attn(proj, k_t, lam_vecs, rel_bias.astype(F32).reshape(-1))

    wo = w_out[0].astype(BF16)
    h1, logits_t = _out_ln(hm, hd, x2, row(ln_emb_g), row(ln_emb_b), wo[:M_V], wo[M_V:], row(ln1_g[0]),
                           row(ln1_b[0]), w_router[0].T.astype(BF16))

    topi_t, pos_t, topw_t, counts = _route(logits_t, router_bias[0].reshape(-1, 1).astype(F32))
    off, vexp, vblk, vcnt, nvis, dest_t = _plan(counts.reshape(1, -1), counts, topi_t, pos_t)
    off_ext = jnp.concatenate([off.reshape(-1), jnp.full((1,), s * TOP_K, jnp.int32)])
    dest_flat = dest_t.T.reshape(-1)

    xs = _dispatch(dest_flat, h1)
    ys = _experts(vexp.reshape(-1), vblk.reshape(-1), vcnt.reshape(-1), off_ext, nvis.reshape(-1), xs, w_gate[0],
                  w_up[0], w_down[0])
    out = _combine(dest_flat, topw_t.T, h1, ys, ws_gate[0].astype(BF16), ws_up[0].astype(BF16),
                   ws_down[0].astype(BF16), row(ln2_g[0]), row(ln2_b[0]))
    return out.reshape(bsz, s, d)
```

```python
import functools
import math

import jax
import jax.numpy as jnp
from jax import lax
from jax.experimental import pallas as pl
from jax.experimental.pallas import tpu as pltpu

F32 = jnp.float32
BF16 = jnp.bfloat16

DEPTH = 1
MLSTM_HEADS = 4
MLSTM_DQK = 128
MLSTM_DV = 256
CONV_WIDTH = 4
DIFF_HEADS = 8
DIFF_DH = 64
DIFF_DV = 2 * DIFF_DH
REL_BUCKETS = 32
REL_MAX_DIST = 128
N_EXPERTS = 256
TOP_K = 8
N_GROUPS = 8
TOPK_GROUPS = 4
D_EXPERT = 512
ROUTED_SCALE = 2.5
LN_EPS = 1e-5
ALPHA = (2 * DEPTH) ** 0.25
LAM_INIT = 0.8 - 0.6 * math.exp(-0.3 * 0)

M_QK = MLSTM_HEADS * MLSTM_DQK
M_V = MLSTM_HEADS * MLSTM_DV
D_QK = DIFF_HEADS * 2 * DIFF_DH
D_VW = DIFF_HEADS * DIFF_DV
PROJ_W = 2 * M_QK + 2 * M_V + D_QK + D_VW

V7X_LANES = 128
V7X_SUBLANES = 8
V7X_VMEM_LIMIT_BYTES = 56 * 1024 * 1024

PROJ_TM = 1024
PROJ_TN = 512
MLSTM_L = 256
ATT_BK = 1024
ATT_BQ = 2048
ATT_STRIP = 256
OUT_TM = 512
ROUTE_TM = 256
MOE_TR = 256
MOVE_TM = 256
PLAN_TB = 512

NEG = -1e30
LOG2E = 1.4426950408889634


def _params(semantics):
    return pltpu.CompilerParams(dimension_semantics=semantics, vmem_limit_bytes=V7X_VMEM_LIMIT_BYTES)


def _layer_norm(x, g, b):
    mu = jnp.mean(x, -1, keepdims=True)
    xc = x - mu
    var = jnp.mean(xc * xc, -1, keepdims=True)
    return xc * lax.rsqrt(var + LN_EPS) * g + b


def _dot(a, b):
    return jnp.dot(a, b, preferred_element_type=F32)


def _dot_nt(a, b):
    return lax.dot_general(a, b, (((1,), (1,)), ((), ())), preferred_element_type=F32)


def _split3(a):
    a1 = a.astype(BF16)
    r1 = a - a1.astype(F32)
    a2 = r1.astype(BF16)
    a3 = (r1 - a2.astype(F32)).astype(BF16)
    return a1, a2, a3


def _pack_bf16_pairs(x):
    n = x.shape[1] // 2
    bits = lax.bitcast_convert_type(x.astype(BF16).astype(F32), jnp.uint32)
    return lax.shift_right_logical(bits[:, :n], jnp.uint32(16)) | (bits[:, n:] & jnp.uint32(0xFFFF0000))


def _unpack_bf16_pairs(u):
    lo = lax.bitcast_convert_type(lax.shift_left(u, jnp.uint32(16)), F32)
    hi = lax.bitcast_convert_type(u & jnp.uint32(0xFFFF0000), F32)
    return jnp.concatenate([lo, hi], axis=1)


def _log_sigmoid(x):
    return jnp.minimum(x, 0.0) - jnp.log(1.0 + jnp.exp(-jnp.abs(x)))


def _ln_proj_kernel(x_ref, g_ref, b_ref, w_ref, wg_ref, wgt_ref, wkt_ref, o_ref, gates_ref, gatest_ref, kt_ref,
                    hb_ref):
    @pl.when(pl.program_id(1) == 0)
    def _():
        hb = _layer_norm(x_ref[...], g_ref[...], b_ref[...]).astype(BF16)
        hb_ref[...] = hb
        gates_ref[...] = _dot(hb, wg_ref[...])
        gatest_ref[...] = _dot_nt(wgt_ref[...], hb)
        kt_ref[...] = _dot_nt(wkt_ref[...], hb).astype(kt_ref.dtype)

    o_ref[...] = _dot(hb_ref[...], w_ref[...]).astype(o_ref.dtype)


def _ln_proj(x2, g, b, w_main, w_gates, w_gates_t, w_k_t):
    s, d = x2.shape
    tm, tn = min(PROJ_TM, s), PROJ_TN
    ng = w_gates.shape[1]
    fixed = lambda i, j: (0, 0)
    return pl.pallas_call(
        _ln_proj_kernel,
        grid=(s // tm, PROJ_W // tn),
        in_specs=[
            pl.BlockSpec((tm, d), lambda i, j: (i, 0)),
            pl.BlockSpec((1, d), fixed),
            pl.BlockSpec((1, d), fixed),
            pl.BlockSpec((d, tn), lambda i, j: (0, j)),
            pl.BlockSpec((d, ng), fixed),
            pl.BlockSpec((w_gates_t.shape[0], d), fixed),
            pl.BlockSpec((D_QK, d), fixed),
        ],
        out_specs=[
            pl.BlockSpec((tm, tn), lambda i, j: (i, j)),
            pl.BlockSpec((tm, ng), lambda i, j: (i, 0)),
            pl.BlockSpec((w_gates_t.shape[0], tm), lambda i, j: (0, i)),
            pl.BlockSpec((D_QK, tm), lambda i, j: (0, i)),
        ],
        out_shape=[
            jax.ShapeDtypeStruct((s, PROJ_W), BF16),
            jax.ShapeDtypeStruct((s, ng), F32),
            jax.ShapeDtypeStruct((w_gates_t.shape[0], s), F32),
            jax.ShapeDtypeStruct((D_QK, s), BF16),
        ],
        scratch_shapes=[pltpu.VMEM((tm, d), BF16)],
        compiler_params=_params(("arbitrary", "arbitrary")),
        name="ln_proj",
    )(x2, g, b, w_main, w_gates, w_gates_t, w_k_t)


def _mlstm_kernel(bi_ref, bf_ref, mq_ref, mk_ref, mv_ref, mo_ref, gates_ref, gatest_ref,
                  cw_ref, cb_ref, ng_ref, o_ref, c_ref, n_ref, m_ref, tail_ref):
    L = mq_ref.shape[0]
    dk, dv = MLSTM_DQK, MLSTM_DV

    @pl.when(pl.program_id(0) == 0)
    def _():
        c_ref[...] = jnp.zeros_like(c_ref)
        n_ref[...] = jnp.zeros_like(n_ref)
        m_ref[...] = jnp.zeros_like(m_ref)
        tail_ref[...] = jnp.zeros_like(tail_ref)

    u = jnp.concatenate([mq_ref[...], mk_ref[...]], axis=1).astype(F32)
    tail = tail_ref[...]
    row8 = lax.broadcasted_iota(jnp.int32, (V7X_SUBLANES, u.shape[1]), 0)
    conv = cb_ref[...] + cw_ref[CONV_WIDTH - 1:CONV_WIDTH, :] * u
    for back in range(1, CONV_WIDTH):
        ur = pltpu.roll(u, back, 0)
        head = jnp.where(row8 < back, pltpu.roll(tail, back, 0), ur[:V7X_SUBLANES])
        shifted = jnp.concatenate([head, ur[V7X_SUBLANES:]], axis=0)
        conv = conv + cw_ref[CONV_WIDTH - 1 - back:CONV_WIDTH - back, :] * shifted
    tail_ref[...] = u[L - V7X_SUBLANES:]
    qk = conv * jax.nn.sigmoid(conv)
    q_all = qk[:, :M_QK]
    k_all = qk[:, M_QK:] * (dk ** -0.5)

    r_i = lax.broadcasted_iota(jnp.int32, (L, L), 0)
    c_i = lax.broadcasted_iota(jnp.int32, (L, L), 1)
    causal = c_i <= r_i
    tril = jnp.where(causal, 1.0, 0.0).astype(BF16)
    triu = jnp.where(r_i <= c_i, 1.0, 0.0).astype(BF16)

    for h in range(MLSTM_HEADS):
        q = q_all[:, h * dk:(h + 1) * dk]
        k = k_all[:, h * dk:(h + 1) * dk]
        qb, kb = q.astype(BF16), k.astype(BF16)
        vb = mv_ref[:, h * dv:(h + 1) * dv]
        b_i, b_f = bi_ref[h], bf_ref[h]

        gblk = gates_ref[:, h * V7X_LANES:(h + 1) * V7X_LANES]
        i_col = gblk[:, 0:1] + b_i
        lf_blk = _log_sigmoid(gblk + b_f)
        f1, f2, f3 = _split3(lf_blk)
        bcum_blk = _dot(tril, f1) + _dot(tril, f2) + _dot(tril, f3)
        b_col = bcum_blk[:, 1:2]
        gt = gatest_ref[h * V7X_SUBLANES:(h + 1) * V7X_SUBLANES, :]
        i_row = gt[0:1, :] + b_i
        lf_rows = _log_sigmoid(gt + b_f)
        g1, g2, g3 = _split3(lf_rows)
        b_row = (_dot(g1, triu) + _dot(g2, triu) + _dot(g3, triu))[1:2, :]

        m_prev = m_ref[h, 0:1, 0:1]
        dmat = jnp.where(causal, b_col - b_row + i_row, NEG)
        inter = b_col + m_prev
        m_t = jnp.maximum(inter, jnp.max(dmat, -1, keepdims=True))
        wts = jnp.exp(dmat - m_t)
        g = jnp.exp(inter - m_t)
        sqk = _dot_nt(qb, kb) * wts
        c_prev = c_ref[h]
        n_prev = n_ref[h, 0:1, :]
        num = g * _dot(qb, c_prev.astype(BF16)) + _dot(sqk.astype(BF16), vb)
        den = g * jnp.sum(q * n_prev, -1, keepdims=True) + jnp.sum(sqk, -1, keepdims=True)
        hh = num / jnp.maximum(jnp.abs(den), jnp.exp(-m_t))

        b_last = b_col[L - 1:L, :]
        w_last_row = b_last - b_row + i_row
        m_new = jnp.maximum(b_last + m_prev, jnp.max(w_last_row, -1, keepdims=True))
        decay = jnp.exp(b_last + m_prev - m_new)
        ws_col = jnp.exp(b_last - b_col + i_col - m_new)
        kw = k * ws_col
        c_ref[h] = decay * c_prev + _dot(kw.T.astype(BF16), vb)
        n_ref[h, 0:1, :] = decay * n_prev + jnp.sum(kw, 0, keepdims=True)
        m_ref[h] = jnp.broadcast_to(m_new, m_ref.shape[1:])

        mu = jnp.mean(hh, -1, keepdims=True)
        hc = hh - mu
        var = jnp.mean(hc * hc, -1, keepdims=True)
        hn = hc * lax.rsqrt(var + LN_EPS) * ng_ref[:, h * dv:(h + 1) * dv]
        og = jax.nn.sigmoid(mo_ref[:, h * dv:(h + 1) * dv].astype(F32))
        o_ref[:, h * dv:(h + 1) * dv] = (hn * og).astype(o_ref.dtype)


def _mlstm(proj, gates, gates_t, conv_w, conv_b, b_i, b_f, norm_g):
    s = proj.shape[0]
    L = min(MLSTM_L, s)
    smem = pl.BlockSpec(memory_space=pltpu.SMEM)
    return pl.pallas_call(
        _mlstm_kernel,
        grid=(s // L,),
        in_specs=[
            smem, smem,
            pl.BlockSpec((L, M_QK), lambda c: (c, 0)),
            pl.BlockSpec((L, M_QK), lambda c: (c, 1)),
            pl.BlockSpec((L, M_V), lambda c: (c, 1)),
            pl.BlockSpec((L, M_V), lambda c: (c, 2)),
            pl.BlockSpec((L, gates.shape[1]), lambda c: (c, 0)),
            pl.BlockSpec((gates_t.shape[0], L), lambda c: (0, c)),
            pl.BlockSpec((CONV_WIDTH, 2 * M_QK), lambda c: (0, 0)),
            pl.BlockSpec((1, 2 * M_QK), lambda c: (0, 0)),
            pl.BlockSpec((1, M_V), lambda c: (0, 0)),
        ],
        out_specs=pl.BlockSpec((L, M_V), lambda c: (c, 0)),
        out_shape=jax.ShapeDtypeStruct((s, M_V), BF16),
        scratch_shapes=[
            pltpu.VMEM((MLSTM_HEADS, MLSTM_DQK, MLSTM_DV), F32),
            pltpu.VMEM((MLSTM_HEADS, V7X_SUBLANES, MLSTM_DQK), F32),
            pltpu.VMEM((MLSTM_HEADS, V7X_SUBLANES, V7X_LANES), F32),
            pltpu.VMEM((V7X_SUBLANES, 2 * M_QK), F32),
        ],
        compiler_params=_params(("arbitrary",)),
        name="mlstm",
    )(b_i, b_f, proj, proj, proj, proj, gates, gates_t, conv_w, conv_b, norm_g)


def _t5_bias_tile(table_ref, h, offset):
    n_t = V7X_LANES
    r_i = lax.broadcasted_iota(jnp.int32, (n_t, n_t), 0)
    c_i = lax.broadcasted_iota(jnp.int32, (n_t, n_t), 1)
    n = jnp.maximum(offset + r_i - c_i, 0)
    max_exact = REL_BUCKETS // 2
    large = max_exact + (jnp.log(jnp.maximum(n, 1).astype(F32) / max_exact)
                         / math.log(REL_MAX_DIST / max_exact) * (REL_BUCKETS - max_exact)).astype(jnp.int32)
    large = jnp.minimum(large, REL_BUCKETS - 1)
    bucket = jnp.where(n < max_exact, n, large)
    out = jnp.zeros((n_t, n_t), F32)
    for b in range(REL_BUCKETS):
        out = jnp.where(bucket == b, table_ref[b * DIFF_HEADS + h], out)
    return out * LOG2E


def _diff_attn_kernel(qi_ref, kj_ref, q_ref, kt_ref, v_ref, lam_ref, table_ref, o_ref,
                      q1_ref, q2_ref, vx_ref, m_ref, acc_ref, bdiag_ref, bsub_ref):
    h = pl.program_id(0)
    p = pl.program_id(1)
    i = qi_ref[p]
    j = kj_ref[p]
    BQ = q_ref.shape[0]
    BK = v_ref.shape[0]
    n_half = BQ // BK
    SR = min(ATT_STRIP, BK)
    strips_per_half = BK // SR
    n_sub = BK // V7X_LANES
    c_far = table_ref[(REL_BUCKETS - 1) * DIFF_HEADS + h] * LOG2E

    @pl.when(p == 0)
    def _():
        p0 = _t5_bias_tile(table_ref, h, 0)
        p0 = jnp.where(lax.broadcasted_iota(jnp.int32, p0.shape, 1) <= lax.broadcasted_iota(jnp.int32, p0.shape, 0),
                       p0, NEG)
        p1 = _t5_bias_tile(table_ref, h, V7X_LANES)
        far = jnp.full((V7X_LANES, V7X_LANES), c_far, F32)
        neg = jnp.full((V7X_LANES, V7X_LANES), NEG, F32)
        for a in range(n_sub):
            for b in range(n_sub):
                tile = p0 if a == b else p1 if a == b + 1 else far if a > b else neg
                bdiag_ref[a * V7X_LANES:(a + 1) * V7X_LANES, b * V7X_LANES:(b + 1) * V7X_LANES] = tile
        bsub_ref[...] = jnp.full(bsub_ref.shape, c_far, F32)
        bsub_ref[0:V7X_LANES, BK - V7X_LANES:BK] = p1

    @pl.when(j == 0)
    def _():
        q = q_ref[...].astype(F32) * ((DIFF_DH ** -0.5) * LOG2E)
        lane = lax.broadcasted_iota(jnp.int32, q.shape, 1)
        q1_ref[...] = jnp.where(lane < DIFF_DH, q, 0.0).astype(BF16)
        q2_ref[...] = jnp.where(lane >= DIFF_DH, q, 0.0).astype(BF16)
        m_ref[...] = jnp.full(m_ref.shape, NEG, F32)
        acc_ref[...] = jnp.zeros_like(acc_ref)

    vx_ref[:, 0:DIFF_DV] = v_ref[...]
    vx_ref[:, DIFF_DV:] = jnp.ones((BK, DIFF_DV), BF16)

    def strip(r, kl, bias_tile=None):
        rows = slice(r * SR, (r + 1) * SR)
        kb = kt_ref[:, 0:kl]
        vb = vx_ref[0:kl, :]
        for a, qz_ref in enumerate((q1_ref, q2_ref)):
            s = _dot(qz_ref[rows, :], kb)
            m_old = m_ref[a, rows, :]
            if bias_tile is None:
                m_new = jnp.maximum(m_old, jnp.max(s, -1, keepdims=True) + c_far)
                pm = jnp.exp2(s - (m_new - c_far))
            else:
                s = s + bias_tile
                m_new = jnp.maximum(m_old, jnp.max(s, -1, keepdims=True))
                pm = jnp.exp2(s - m_new)
            alpha = jnp.exp2(m_old - m_new)
            acc_ref[a, rows, :] = alpha * acc_ref[a, rows, :] + _dot(pm.astype(BF16), vb)
            m_ref[a, rows, :] = m_new

    def key_block(delta):
        for r in range(n_half * strips_per_half):
            ro = (r % strips_per_half) * SR
            rel = None if delta is None else delta - r // strips_per_half
            if rel is None or rel <= -2:
                strip(r, BK)
            elif rel == -1:
                strip(r, BK, bsub_ref[...] if ro == 0 else None)
            elif rel == 0:
                strip(r, ro + SR, bdiag_ref[ro:ro + SR, 0:ro + SR])

    d = j - i * n_half

    @pl.when(d <= -2)
    def _():
        key_block(None)

    for delta in range(-1, n_half):
        @pl.when(d == delta)
        def _(delta=delta):
            key_block(delta)
            if delta == n_half - 1:
                lq1, lk1, lq2, lk2 = (lam_ref[t:t + 1, :] for t in range(4))
                lam = (jnp.exp(jnp.sum(lq1 * lk1, -1, keepdims=True))
                       - jnp.exp(jnp.sum(lq2 * lk2, -1, keepdims=True)) + LAM_INIT)
                o = (acc_ref[0, :, 0:DIFF_DV] / acc_ref[0, :, DIFF_DV:DIFF_DV + 1]
                     - lam * (acc_ref[1, :, 0:DIFF_DV] / acc_ref[1, :, DIFF_DV:DIFF_DV + 1]))
                o = o * lax.rsqrt(jnp.mean(o * o, -1, keepdims=True) + LN_EPS)
                o_ref[...] = (o * (1.0 - LAM_INIT)).astype(o_ref.dtype)


def _diff_attn(proj, k_t, lam_vecs, table_flat):
    s = proj.shape[0]
    BK = min(ATT_BK, s)
    BQ = min(ATT_BQ, s)
    n_half = BQ // BK
    pairs = [(i, j) for i in range(s // BQ) for j in range((i + 1) * n_half)]
    qi = jnp.asarray([a for a, _ in pairs], jnp.int32)
    kj = jnp.asarray([b for _, b in pairs], jnp.int32)
    q_blk = (2 * M_QK + 2 * M_V) // V7X_LANES
    v_blk = q_blk + D_QK // V7X_LANES
    SR = min(ATT_STRIP, BK)
    grid_spec = pltpu.PrefetchScalarGridSpec(
        num_scalar_prefetch=2,
        grid=(DIFF_HEADS, len(pairs)),
        in_specs=[
            pl.BlockSpec((BQ, V7X_LANES), lambda h, p, qi, kj: (qi[p], q_blk + h)),
            pl.BlockSpec((2 * DIFF_DH, BK), lambda h, p, qi, kj: (h, kj[p])),
            pl.BlockSpec((BK, V7X_LANES), lambda h, p, qi, kj: (kj[p], v_blk + h)),
            pl.BlockSpec((4, DIFF_DH), lambda h, p, qi, kj: (0, 0)),
            pl.BlockSpec(memory_space=pltpu.SMEM),
        ],
        out_specs=pl.BlockSpec((BQ, V7X_LANES), lambda h, p, qi, kj: (qi[p], h)),
        scratch_shapes=[
            pltpu.VMEM((BQ, V7X_LANES), BF16),
            pltpu.VMEM((BQ, V7X_LANES), BF16),
            pltpu.VMEM((BK, 2 * DIFF_DV), BF16),
            pltpu.VMEM((2, BQ, 1), F32),
            pltpu.VMEM((2, BQ, 2 * DIFF_DV), F32),
            pltpu.VMEM((BK, BK), F32),
            pltpu.VMEM((SR, BK), F32),
        ],
    )
    return pl.pallas_call(
        _diff_attn_kernel,
        grid_spec=grid_spec,
        out_shape=jax.ShapeDtypeStruct((s, D_VW), BF16),
        compiler_params=_params(("arbitrary", "arbitrary")),
        name="diff_attn",
    )(qi, kj, proj, k_t, proj, lam_vecs, table_flat)


def _out_ln_kernel(hm_ref, hd_ref, x_ref, g0_ref, b0_ref, wo1_ref, wo2_ref, g1_ref, b1_ref, wr_ref,
                   h1_ref, h1p_ref, logit_ref):
    mix = _dot(hm_ref[...], wo1_ref[...]) + _dot(hd_ref[...], wo2_ref[...])
    h0 = _layer_norm(x_ref[...], g0_ref[...], b0_ref[...])
    h1 = _layer_norm(ALPHA * h0 + mix, g1_ref[...], b1_ref[...])
    h1_ref[...] = h1
    h1p_ref[...] = _pack_bf16_pairs(h1)
    logit_ref[...] = _dot_nt(wr_ref[...], h1.astype(BF16))


def _out_ln(hm, hd, x2, g0, b0, wo1, wo2, g1, b1, wr):
    s, d = x2.shape
    tm = min(OUT_TM, s)
    row = lambda i: (i, 0)
    fixed = lambda i: (0, 0)
    return pl.pallas_call(
        _out_ln_kernel,
        grid=(s // tm,),
        in_specs=[
            pl.BlockSpec((tm, M_V), row), pl.BlockSpec((tm, D_VW), row), pl.BlockSpec((tm, d), row),
            pl.BlockSpec((1, d), fixed), pl.BlockSpec((1, d), fixed),
            pl.BlockSpec((M_V, d), fixed), pl.BlockSpec((D_VW, d), fixed),
            pl.BlockSpec((1, d), fixed), pl.BlockSpec((1, d), fixed),
            pl.BlockSpec((N_EXPERTS, d), fixed),
        ],
        out_specs=[pl.BlockSpec((tm, d), row), pl.BlockSpec((tm, d // 2), row),
                   pl.BlockSpec((N_EXPERTS, tm), lambda i: (0, i))],
        out_shape=[jax.ShapeDtypeStruct((s, d), F32), jax.ShapeDtypeStruct((s, d // 2), jnp.uint32),
                   jax.ShapeDtypeStruct((N_EXPERTS, s), F32)],
        compiler_params=_params(("arbitrary",)),
        name="out_ln",
    )(hm, hd, x2, g0, b0, wo1, wo2, g1, b1, wr)


def _route_kernel(logit_ref, rb_ref, topi_ref, pos_ref, topw_ref, cnt_ref, carry_ref):
    n_e, tm = logit_ref.shape
    gsz = n_e // N_GROUPS
    ninf = -jnp.inf

    @pl.when(pl.program_id(0) == 0)
    def _():
        carry_ref[...] = jnp.zeros_like(carry_ref)

    sc = jax.nn.sigmoid(logit_ref[...])
    sel = sc + rb_ref[...]

    sel3 = sel.reshape(N_GROUPS, gsz, tm)
    in_grp = lax.broadcasted_iota(jnp.int32, sel3.shape, 1)
    m1 = jnp.max(sel3, 1, keepdims=True)
    i1 = jnp.min(jnp.where(sel3 == m1, in_grp, gsz), 1, keepdims=True)
    m2 = jnp.max(jnp.where(in_grp == i1, ninf, sel3), 1, keepdims=True)
    gscore = (m1 + m2).reshape(N_GROUPS, tm)
    gid = lax.broadcasted_iota(jnp.int32, gscore.shape, 0)
    beaten = jnp.zeros(gscore.shape, F32)
    for o in range(1, N_GROUPS):
        other = pltpu.roll(gscore, o, 0)
        wins = (other > gscore) | ((other == gscore) & (gid >= o))
        beaten = beaten + jnp.where(wins, 1.0, 0.0)
    keep = jnp.where(beaten < TOPK_GROUPS, 1.0, 0.0)
    keep3 = jnp.broadcast_to(keep.reshape(N_GROUPS, 1, tm), sel3.shape)
    masked = jnp.where(keep3 > 0.5, sel3, ninf).reshape(n_e, tm)

    eid = lax.broadcasted_iota(jnp.int32, (n_e, tm), 0)
    onehots, idxs, ws = [], [], []
    for _ in range(TOP_K):
        m = jnp.max(masked, 0, keepdims=True)
        idx = jnp.min(jnp.where(masked == m, eid, n_e), 0, keepdims=True)
        oh = eid == idx
        onehots.append(oh)
        idxs.append(idx)
        ws.append(jnp.sum(jnp.where(oh, sc, 0.0), 0, keepdims=True))
        masked = jnp.where(oh, ninf, masked)
    wsum = ws[0]
    for w in ws[1:]:
        wsum = wsum + w

    chosen = jnp.zeros((n_e, tm), F32)
    for oh in onehots:
        chosen = jnp.where(oh, 1.0, chosen)
    r_i = lax.broadcasted_iota(jnp.int32, (tm, tm), 0)
    c_i = lax.broadcasted_iota(jnp.int32, (tm, tm), 1)
    before = jnp.where(r_i < c_i, 1.0, 0.0).astype(BF16)
    rank = _dot(chosen.astype(BF16), before) + carry_ref[...]
    carry_new = carry_ref[...] + jnp.sum(chosen, 1, keepdims=True)
    carry_ref[...] = carry_new
    cnt_ref[...] = carry_new

    kid = lax.broadcasted_iota(jnp.int32, (TOP_K, tm), 0)
    topi = jnp.zeros((TOP_K, tm), jnp.int32)
    pos = jnp.zeros((TOP_K, tm), jnp.int32)
    topw = jnp.zeros((TOP_K, tm), F32)
    for k in range(TOP_K):
        pk = jnp.sum(jnp.where(onehots[k], rank, 0.0), 0, keepdims=True)
        topi = jnp.where(kid == k, idxs[k], topi)
        pos = jnp.where(kid == k, pk.astype(jnp.int32), pos)
        topw = jnp.where(kid == k, ws[k] / wsum * ROUTED_SCALE, topw)
    topi_ref[...] = topi
    pos_ref[...] = pos
    topw_ref[...] = topw


def _route(logits_t, router_bias_col):
    n_e, t = logits_t.shape
    tm = min(ROUTE_TM, t)
    col = lambda i: (0, i)
    fixed = lambda i: (0, 0)
    return pl.pallas_call(
        _route_kernel,
        grid=(t // tm,),
        in_specs=[pl.BlockSpec((n_e, tm), col), pl.BlockSpec((n_e, 1), fixed)],
        out_specs=[pl.BlockSpec((TOP_K, tm), col), pl.BlockSpec((TOP_K, tm), col),
                   pl.BlockSpec((TOP_K, tm), col), pl.BlockSpec((n_e, 1), fixed)],
        out_shape=[jax.ShapeDtypeStruct((TOP_K, t), jnp.int32), jax.ShapeDtypeStruct((TOP_K, t), jnp.int32),
                   jax.ShapeDtypeStruct((TOP_K, t), F32), jax.ShapeDtypeStruct((n_e, 1), F32)],
        scratch_shapes=[pltpu.VMEM((n_e, 1), F32)],
        compiler_params=_params(("arbitrary",)),
        name="route",
    )(logits_t, router_bias_col)


def _plan_kernel(cnt_ref, cntc_ref, topi_ref, pos_ref, off_ref, vexp_ref, vblk_ref, vcnt_ref, nvis_ref, dest_ref):
    nv = vexp_ref.shape[0]
    tr = float(MOE_TR)
    cnt = jnp.broadcast_to(cnt_ref[...], (V7X_SUBLANES, N_EXPERTS))
    r_i = lax.broadcasted_iota(jnp.int32, (N_EXPERTS, N_EXPERTS), 0)
    c_i = lax.broadcasted_iota(jnp.int32, (N_EXPERTS, N_EXPERTS), 1)
    upper = jnp.where(r_i <= c_i, 1.0, 0.0).astype(BF16)

    def cumsum_lanes(a):
        a1, a2, a3 = _split3(a)
        return _dot(a1, upper) + _dot(a2, upper) + _dot(a3, upper)

    end = cumsum_lanes(cnt)
    start = end - cnt
    first_blk = jnp.floor(start / tr)
    last_blk = jnp.floor((end - 1.0) / tr)
    nvis_e = jnp.where(cnt > 0.0, last_blk - first_blk + 1.0, 0.0)
    vend = cumsum_lanes(nvis_e)
    vstart = vend - nvis_e
    off_ref[...] = start[0:1, :].astype(jnp.int32)
    nvis_ref[...] = vend[0:1, N_EXPERTS - 1:N_EXPERTS].astype(jnp.int32)

    v = lax.broadcasted_iota(jnp.int32, (nv, N_EXPERTS), 0).astype(F32)
    ve = jnp.broadcast_to(vend[0:1, :], (nv, N_EXPERTS))
    expert = jnp.sum(jnp.where(ve <= v, 1.0, 0.0), -1, keepdims=True)
    expert = jnp.minimum(expert, N_EXPERTS - 1.0)
    lane = lax.broadcasted_iota(jnp.int32, (nv, N_EXPERTS), 1).astype(F32)
    mine = lane == expert
    fb = jnp.sum(jnp.where(mine, jnp.broadcast_to(first_blk[0:1, :], (nv, N_EXPERTS)), 0.0), -1, keepdims=True)
    vs = jnp.sum(jnp.where(mine, jnp.broadcast_to(vstart[0:1, :], (nv, N_EXPERTS)), 0.0), -1, keepdims=True)
    vc = jnp.sum(jnp.where(mine, jnp.broadcast_to(nvis_e[0:1, :], (nv, N_EXPERTS)), 0.0), -1, keepdims=True)
    vexp_ref[...] = expert.astype(jnp.int32)
    vblk_ref[...] = (fb + (v[:, 0:1] - vs)).astype(jnp.int32)
    vcnt_ref[...] = vc.astype(jnp.int32)

    lower = jnp.where(c_i < r_i, 1.0, 0.0).astype(BF16)
    c1, c2, c3 = _split3(jnp.broadcast_to(cntc_ref[...], (N_EXPERTS, V7X_LANES)))
    start_col = (_dot(lower, c1) + _dot(lower, c2) + _dot(lower, c3))[:, 0:1]
    n_tok = topi_ref.shape[1]
    tb = min(PLAN_TB, n_tok)
    eid = lax.broadcasted_iota(jnp.int32, (N_EXPERTS, tb), 0)
    kid = lax.broadcasted_iota(jnp.int32, (TOP_K, tb), 0)

    def dest_block(i, carry):
        cols = pl.ds(pl.multiple_of(i * tb, tb), tb)
        ti = topi_ref[:, cols]
        first_row = jnp.zeros((TOP_K, tb), F32)
        for k in range(TOP_K):
            fr = jnp.sum(jnp.where(eid == ti[k:k + 1, :], start_col, 0.0), 0, keepdims=True)
            first_row = jnp.where(kid == k, fr, first_row)
        dest_ref[:, cols] = first_row.astype(jnp.int32) + pos_ref[:, cols]
        return carry

    lax.fori_loop(0, n_tok // tb, dest_block, 0)


def _plan(counts_row, counts_col, topi_t, pos_t):
    n_rows = topi_t.shape[0] * topi_t.shape[1]
    nv = n_rows // MOE_TR + N_EXPERTS
    return pl.pallas_call(
        _plan_kernel,
        out_shape=[jax.ShapeDtypeStruct((1, N_EXPERTS), jnp.int32), jax.ShapeDtypeStruct((nv, 1), jnp.int32),
                   jax.ShapeDtypeStruct((nv, 1), jnp.int32), jax.ShapeDtypeStruct((nv, 1), jnp.int32),
                   jax.ShapeDtypeStruct((1, 1), jnp.int32), jax.ShapeDtypeStruct(topi_t.shape, jnp.int32)],
        compiler_params=pltpu.CompilerParams(vmem_limit_bytes=V7X_VMEM_LIMIT_BYTES),
        name="plan",
    )(counts_row, counts_col, topi_t, pos_t)


def _dispatch_kernel(dest_ref, x_ref, xs_ref, sem):
    tm = x_ref.shape[0]

    def issue(t, carry):
        for k in range(TOP_K):
            d = dest_ref[t * TOP_K + k]
            pltpu.make_async_copy(x_ref.at[pl.ds(t, 1)], xs_ref.at[pl.ds(d, 1)], sem).start(priority=k % 2)
        return carry

    lax.fori_loop(0, tm, issue, 0)
    for _ in range(TOP_K):
        pltpu.make_async_copy(x_ref, xs_ref.at[pl.ds(0, tm)], sem).wait()


def _dispatch(dest_flat, rows):
    t, d = rows.shape
    tm = min(MOVE_TM, t)
    return pl.pallas_call(
        _dispatch_kernel,
        grid=(t // tm,),
        in_specs=[
            pl.BlockSpec((tm * TOP_K,), lambda i: (i,), memory_space=pltpu.SMEM),
            pl.BlockSpec((tm, d), lambda i: (i, 0)),
        ],
        out_specs=pl.BlockSpec(memory_space=pl.ANY),
        out_shape=jax.ShapeDtypeStruct((t * TOP_K, d), rows.dtype),
        scratch_shapes=[pltpu.SemaphoreType.DMA(())],
        compiler_params=_params(("arbitrary",)),
        name="dispatch",
    )(dest_flat, rows)


def _experts_kernel(vexp_ref, vblk_ref, vcnt_ref, off_ref, nvis_ref, xs_ref, wg_hbm, wu_hbm, wd_hbm, ys_ref,
                    wg_buf, wu_buf, wd_buf, acc_ref, slot_ref, sem):
    v = pl.program_id(0)
    tr = xs_ref.shape[0]
    nvis = nvis_ref[0]

    def weight_copies(e, slot):
        return (pltpu.make_async_copy(wg_hbm.at[e], wg_buf.at[slot], sem.at[slot, 0]),
                pltpu.make_async_copy(wu_hbm.at[e], wu_buf.at[slot], sem.at[slot, 1]),
                pltpu.make_async_copy(wd_hbm.at[e], wd_buf.at[slot], sem.at[slot, 2]))

    @pl.when(v < nvis)
    def _():
        e = vexp_ref[v]
        blk = vblk_ref[v]
        first_of_expert = jnp.logical_or(v == 0, vexp_ref[jnp.maximum(v - 1, 0)] != e)

        @pl.when(v == 0)
        def _():
            slot_ref[0] = 0
            for c in weight_copies(e, 0):
                c.start()

        @pl.when(jnp.logical_and(first_of_expert, v > 0))
        def _():
            slot_ref[0] = 1 - slot_ref[0]

        slot = slot_ref[0]

        @pl.when(first_of_expert)
        def _():
            for c in weight_copies(e, slot):
                c.wait()
            nxt = v + vcnt_ref[v]

            @pl.when(nxt < nvis)
            def _():
                for c in weight_copies(vexp_ref[jnp.minimum(nxt, vexp_ref.shape[0] - 1)], 1 - slot):
                    c.start()

        lo = off_ref[e]
        hi = off_ref[e + 1]
        row = blk * tr + lax.broadcasted_iota(jnp.int32, (tr, 1), 0)
        mine = (row >= lo) & (row < hi)
        xb = _unpack_bf16_pairs(xs_ref[...]).astype(BF16)
        gate = _dot(xb, wg_buf[slot].astype(BF16))
        up = _dot(xb, wu_buf[slot].astype(BF16))
        act = (gate * jax.nn.sigmoid(gate) * up).astype(BF16)
        y = jnp.where(mine, _dot(act, wd_buf[slot].astype(BF16)), 0.0)
        first_of_block = jnp.logical_or(v == 0, vblk_ref[jnp.maximum(v - 1, 0)] != blk)

        @pl.when(first_of_block)
        def _():
            acc_ref[...] = y

        @pl.when(jnp.logical_not(first_of_block))
        def _():
            acc_ref[...] += y

        ys_ref[...] = _pack_bf16_pairs(acc_ref[...])


def _experts(vexp, vblk, vcnt, off_ext, nvis, xs, w_gate, w_up, w_down):
    n_rows, dp = xs.shape
    d = 2 * dp
    nv = vexp.shape[0]

    def block_of(v, ve, vb, vc, off, nvis):
        return (vb[jnp.minimum(v, nvis[0] - 1)], 0)

    grid_spec = pltpu.PrefetchScalarGridSpec(
        num_scalar_prefetch=5,
        grid=(nv,),
        in_specs=[
            pl.BlockSpec((MOE_TR, dp), block_of),
            pl.BlockSpec(memory_space=pl.ANY),
            pl.BlockSpec(memory_space=pl.ANY),
            pl.BlockSpec(memory_space=pl.ANY),
        ],
        out_specs=pl.BlockSpec((MOE_TR, dp), block_of),
        scratch_shapes=[
            pltpu.VMEM((2, d, D_EXPERT), F32),
            pltpu.VMEM((2, d, D_EXPERT), F32),
            pltpu.VMEM((2, D_EXPERT, d), F32),
            pltpu.VMEM((MOE_TR, d), F32),
            pltpu.SMEM((1,), jnp.int32),
            pltpu.SemaphoreType.DMA((2, 3)),
        ],
    )
    return pl.pallas_call(
        _experts_kernel,
        grid_spec=grid_spec,
        out_shape=jax.ShapeDtypeStruct((n_rows, dp), jnp.uint32),
        compiler_params=_params(("arbitrary",)),
        name="experts",
    )(vexp, vblk, vcnt, off_ext, nvis, xs, w_gate, w_up, w_down)


def _combine_kernel(dest_ref, topw_ref, h1_ref, ys_ref, sg_ref, su_ref, sd_ref, g2_ref, b2_ref, o_ref,
                    buf_ref, sem):
    tm = h1_ref.shape[0]

    def issue(t, carry):
        for k in range(TOP_K):
            d = dest_ref[t * TOP_K + k]
            pltpu.make_async_copy(ys_ref.at[pl.ds(d, 1)], buf_ref.at[k, pl.ds(t, 1)], sem).start(priority=k % 2)
        return carry

    lax.fori_loop(0, tm, issue, 0)

    h1 = h1_ref[...]
    hb = h1.astype(BF16)
    gate = _dot(hb, sg_ref[...])
    up = _dot(hb, su_ref[...])
    y = _dot((gate * jax.nn.sigmoid(gate) * up).astype(BF16), sd_ref[...])

    for k in range(TOP_K):
        pltpu.make_async_copy(ys_ref.at[pl.ds(0, tm)], buf_ref.at[k], sem).wait()
    w = topw_ref[...]
    for k in range(TOP_K):
        y = y + _unpack_bf16_pairs(buf_ref[k]) * w[:, k:k + 1]
    o_ref[...] = _layer_norm(ALPHA * h1 + y, g2_ref[...], b2_ref[...])


def _combine(dest_flat, topw, h1, ys, sg, su, sd, g2, b2):
    t, d = h1.shape
    tm = min(MOVE_TM, t)
    row = lambda i: (i, 0)
    fixed = lambda i: (0, 0)
    return pl.pallas_call(
        _combine_kernel,
        grid=(t // tm,),
        in_specs=[
            pl.BlockSpec((tm * TOP_K,), lambda i: (i,), memory_space=pltpu.SMEM),
            pl.BlockSpec((tm, TOP_K), row),
            pl.BlockSpec((tm, d), row),
            pl.BlockSpec(memory_space=pl.ANY),
            pl.BlockSpec((d, D_EXPERT), fixed), pl.BlockSpec((d, D_EXPERT), fixed),
            pl.BlockSpec((D_EXPERT, d), fixed),
            pl.BlockSpec((1, d), fixed), pl.BlockSpec((1, d), fixed),
        ],
        out_specs=pl.BlockSpec((tm, d), row),
        out_shape=jax.ShapeDtypeStruct((t, d), F32),
        scratch_shapes=[pltpu.VMEM((TOP_K, tm, ys.shape[1]), ys.dtype), pltpu.SemaphoreType.DMA(())],
        compiler_params=_params(("arbitrary",)),
        name="combine",
    )(dest_flat, topw, h1, ys, sg, su, sd, g2, b2)


def kernel(x, ln_emb_g, ln_emb_b, w_in, conv_w, conv_b, b_igate, b_fgate, mlstm_norm_g, lambda_q1, lambda_k1,
           lambda_q2, lambda_k2, rel_bias, w_out, ln1_g, ln1_b, w_router, router_bias, w_gate, w_up, w_down,
           ws_gate, ws_up, ws_down, ln2_g, ln2_b):
    bsz, s, d = x.shape
    assert bsz == 1 and w_in.shape[0] == DEPTH == 1
    x2 = x.reshape(s, d)
    row = lambda a: a.reshape(1, -1).astype(F32)

    w = w_in[0]
    c0 = 2 * M_QK + 2 * M_V
    c1 = c0 + 2 * MLSTM_HEADS
    c2 = c1 + D_QK
    c3 = c2 + D_QK
    w_main = jnp.concatenate([w[:, :c0], w[:, c1:c2], w[:, c3:]], axis=1).astype(BF16)
    w_k_t = w[:, c2:c3].T.astype(BF16)
    wi, wf = w[:, c0:c0 + MLSTM_HEADS], w[:, c0 + MLSTM_HEADS:c1]
    w_gates = jnp.zeros((d, MLSTM_HEADS, V7X_LANES), F32).at[:, :, 0].set(wi).at[:, :, 1].set(wf)
    w_gates = w_gates.reshape(d, MLSTM_HEADS * V7X_LANES).astype(BF16)
    w_gates_t = jnp.zeros((MLSTM_HEADS, V7X_SUBLANES, d), F32).at[:, 0].set(wi.T).at[:, 1].set(wf.T)
    w_gates_t = w_gates_t.reshape(MLSTM_HEADS * V7X_SUBLANES, d).astype(BF16)

    proj, gates, gates_t, k_t = _ln_proj(x2, row(ln_emb_g), row(ln_emb_b), w_main, w_gates, w_gates_t, w_k_t)

    hm = _mlstm(proj, gates, gates_t, conv_w[0], row(conv_b[0]), b_igate[0].astype(F32),
                b_fgate[0].astype(F32), row(mlstm_norm_g[0]))

    lam_vecs = jnp.stack([lambda_q1[0], lambda_k1[0], lambda_q2[0], lambda_k2[0]]).astype(F32)
    hd = _diff_attn(proj, k_t, lam_vecs, rel_bias.astype(F32).reshape(-1))

    wo = w_out[0].astype(BF16)
    h1, h1p, logits_t = _out_ln(hm, hd, x2, row(ln_emb_g), row(ln_emb_b), wo[:M_V], wo[M_V:], row(ln1_g[0]),
                           row(ln1_b[0]), w_router[0].T.astype(BF16))

    topi_t, pos_t, topw_t, counts = _route(logits_t, router_bias[0].reshape(-1, 1).astype(F32))
    off, vexp, vblk, vcnt, nvis, dest_t = _plan(counts.reshape(1, -1), counts, topi_t, pos_t)
    off_ext = jnp.concatenate([off.reshape(-1), jnp.full((1,), s * TOP_K, jnp.int32)])
    dest_flat = dest_t.T.reshape(-1)

    xs = _dispatch(dest_flat, h1p)
    ys = _experts(vexp.reshape(-1), vblk.reshape(-1), vcnt.reshape(-1), off_ext, nvis.reshape(-1), xs, w_gate[0],
                  w_up[0], w_down[0])
    out = _combine(dest_flat, topw_t.T, h1, ys, ws_gate[0].astype(BF16), ws_up[0].astype(BF16),
                   ws_down[0].astype(BF16), row(ln2_g[0]), row(ln2_b[0]))
    return out.reshape(bsz, s, d)
```

```python
import functools
import math

import jax
import jax.numpy as jnp
from jax import lax
from jax.experimental import pallas as pl
from jax.experimental.pallas import tpu as pltpu

F32 = jnp.float32
BF16 = jnp.bfloat16

DEPTH = 1
MLSTM_HEADS = 4
MLSTM_DQK = 128
MLSTM_DV = 256
CONV_WIDTH = 4
DIFF_HEADS = 8
DIFF_DH = 64
DIFF_DV = 2 * DIFF_DH
REL_BUCKETS = 32
REL_MAX_DIST = 128
N_EXPERTS = 256
TOP_K = 8
N_GROUPS = 8
TOPK_GROUPS = 4
D_EXPERT = 512
ROUTED_SCALE = 2.5
LN_EPS = 1e-5
ALPHA = (2 * DEPTH) ** 0.25
LAM_INIT = 0.8 - 0.6 * math.exp(-0.3 * 0)

M_QK = MLSTM_HEADS * MLSTM_DQK
M_V = MLSTM_HEADS * MLSTM_DV
D_QK = DIFF_HEADS * 2 * DIFF_DH
D_VW = DIFF_HEADS * DIFF_DV
PROJ_W = 2 * M_QK + 2 * M_V + D_QK + D_VW

V7X_LANES = 128
V7X_SUBLANES = 8
V7X_VMEM_LIMIT_BYTES = 56 * 1024 * 1024

PROJ_TM = 1024
PROJ_TN = 512
MLSTM_L = 256
ATT_BK = 1024
ATT_BQ = 2048
ATT_STRIP = 256
OUT_TM = 512
ROUTE_TM = 256
MOE_TR = 256
MOVE_TM = 256
SHARED_TM = 256
PLAN_TB = 512

NEG = -1e30
LOG2E = 1.4426950408889634


def _params(semantics):
    return pltpu.CompilerParams(dimension_semantics=semantics, vmem_limit_bytes=V7X_VMEM_LIMIT_BYTES)


def _layer_norm(x, g, b):
    mu = jnp.mean(x, -1, keepdims=True)
    xc = x - mu
    var = jnp.mean(xc * xc, -1, keepdims=True)
    return xc * lax.rsqrt(var + LN_EPS) * g + b


def _dot(a, b):
    return jnp.dot(a, b, preferred_element_type=F32)


def _dot_nt(a, b):
    return lax.dot_general(a, b, (((1,), (1,)), ((), ())), preferred_element_type=F32)


def _split3(a):
    a1 = a.astype(BF16)
    r1 = a - a1.astype(F32)
    a2 = r1.astype(BF16)
    a3 = (r1 - a2.astype(F32)).astype(BF16)
    return a1, a2, a3


def _pack_bf16_pairs(x):
    n = x.shape[1] // 2
    bits = lax.bitcast_convert_type(x.astype(BF16).astype(F32), jnp.uint32)
    return lax.shift_right_logical(bits[:, :n], jnp.uint32(16)) | (bits[:, n:] & jnp.uint32(0xFFFF0000))


def _unpack_bf16_pairs(u):
    lo = lax.bitcast_convert_type(lax.shift_left(u, jnp.uint32(16)), F32)
    hi = lax.bitcast_convert_type(u & jnp.uint32(0xFFFF0000), F32)
    return jnp.concatenate([lo, hi], axis=1)


def _log_sigmoid(x):
    return jnp.minimum(x, 0.0) - jnp.log(1.0 + jnp.exp(-jnp.abs(x)))


def _ln_proj_kernel(x_ref, g_ref, b_ref, w_ref, wg_ref, wgt_ref, wkt_ref, o_ref, gates_ref, gatest_ref, kt_ref,
                    hb_ref):
    @pl.when(pl.program_id(1) == 0)
    def _():
        hb = _layer_norm(x_ref[...], g_ref[...], b_ref[...]).astype(BF16)
        hb_ref[...] = hb
        gates_ref[...] = _dot(hb, wg_ref[...])
        gatest_ref[...] = _dot_nt(wgt_ref[...], hb)
        kt_ref[...] = _dot_nt(wkt_ref[...], hb).astype(kt_ref.dtype)

    o_ref[...] = _dot(hb_ref[...], w_ref[...]).astype(o_ref.dtype)


def _ln_proj(x2, g, b, w_main, w_gates, w_gates_t, w_k_t):
    s, d = x2.shape
    tm, tn = min(PROJ_TM, s), PROJ_TN
    ng = w_gates.shape[1]
    fixed = lambda i, j: (0, 0)
    return pl.pallas_call(
        _ln_proj_kernel,
        grid=(s // tm, PROJ_W // tn),
        in_specs=[
            pl.BlockSpec((tm, d), lambda i, j: (i, 0)),
            pl.BlockSpec((1, d), fixed),
            pl.BlockSpec((1, d), fixed),
            pl.BlockSpec((d, tn), lambda i, j: (0, j)),
            pl.BlockSpec((d, ng), fixed),
            pl.BlockSpec((w_gates_t.shape[0], d), fixed),
            pl.BlockSpec((D_QK, d), fixed),
        ],
        out_specs=[
            pl.BlockSpec((tm, tn), lambda i, j: (i, j)),
            pl.BlockSpec((tm, ng), lambda i, j: (i, 0)),
            pl.BlockSpec((w_gates_t.shape[0], tm), lambda i, j: (0, i)),
            pl.BlockSpec((D_QK, tm), lambda i, j: (0, i)),
        ],
        out_shape=[
            jax.ShapeDtypeStruct((s, PROJ_W), BF16),
            jax.ShapeDtypeStruct((s, ng), F32),
            jax.ShapeDtypeStruct((w_gates_t.shape[0], s), F32),
            jax.ShapeDtypeStruct((D_QK, s), BF16),
        ],
        scratch_shapes=[pltpu.VMEM((tm, d), BF16)],
        compiler_params=_params(("arbitrary", "arbitrary")),
        name="ln_proj",
    )(x2, g, b, w_main, w_gates, w_gates_t, w_k_t)


def _mlstm_kernel(bi_ref, bf_ref, mq_ref, mk_ref, mv_ref, mo_ref, gates_ref, gatest_ref,
                  cw_ref, cb_ref, ng_ref, o_ref, c_ref, n_ref, m_ref, tail_ref):
    L = mq_ref.shape[0]
    dk, dv = MLSTM_DQK, MLSTM_DV

    @pl.when(pl.program_id(0) == 0)
    def _():
        c_ref[...] = jnp.zeros_like(c_ref)
        n_ref[...] = jnp.zeros_like(n_ref)
        m_ref[...] = jnp.zeros_like(m_ref)
        tail_ref[...] = jnp.zeros_like(tail_ref)

    u = jnp.concatenate([mq_ref[...], mk_ref[...]], axis=1).astype(F32)
    tail = tail_ref[...]
    row8 = lax.broadcasted_iota(jnp.int32, (V7X_SUBLANES, u.shape[1]), 0)
    conv = cb_ref[...] + cw_ref[CONV_WIDTH - 1:CONV_WIDTH, :] * u
    for back in range(1, CONV_WIDTH):
        ur = pltpu.roll(u, back, 0)
        head = jnp.where(row8 < back, pltpu.roll(tail, back, 0), ur[:V7X_SUBLANES])
        shifted = jnp.concatenate([head, ur[V7X_SUBLANES:]], axis=0)
        conv = conv + cw_ref[CONV_WIDTH - 1 - back:CONV_WIDTH - back, :] * shifted
    tail_ref[...] = u[L - V7X_SUBLANES:]
    qk = conv * jax.nn.sigmoid(conv)
    q_all = qk[:, :M_QK]
    k_all = qk[:, M_QK:] * (dk ** -0.5)

    r_i = lax.broadcasted_iota(jnp.int32, (L, L), 0)
    c_i = lax.broadcasted_iota(jnp.int32, (L, L), 1)
    causal = c_i <= r_i
    tril = jnp.where(causal, 1.0, 0.0).astype(BF16)
    triu = jnp.where(r_i <= c_i, 1.0, 0.0).astype(BF16)

    for h in range(MLSTM_HEADS):
        q = q_all[:, h * dk:(h + 1) * dk]
        k = k_all[:, h * dk:(h + 1) * dk]
        qb, kb = q.astype(BF16), k.astype(BF16)
        vb = mv_ref[:, h * dv:(h + 1) * dv]
        b_i, b_f = bi_ref[h], bf_ref[h]

        gblk = gates_ref[:, h * V7X_LANES:(h + 1) * V7X_LANES]
        i_col = gblk[:, 0:1] + b_i
        lf_blk = _log_sigmoid(gblk + b_f)
        f1, f2, f3 = _split3(lf_blk)
        bcum_blk = _dot(tril, f1) + _dot(tril, f2) + _dot(tril, f3)
        b_col = bcum_blk[:, 1:2]
        gt = gatest_ref[h * V7X_SUBLANES:(h + 1) * V7X_SUBLANES, :]
        i_row = gt[0:1, :] + b_i
        lf_rows = _log_sigmoid(gt + b_f)
        g1, g2, g3 = _split3(lf_rows)
        b_row = (_dot(g1, triu) + _dot(g2, triu) + _dot(g3, triu))[1:2, :]

        m_prev = m_ref[h, 0:1, 0:1]
        dmat = jnp.where(causal, b_col - b_row + i_row, NEG)
        inter = b_col + m_prev
        m_t = jnp.maximum(inter, jnp.max(dmat, -1, keepdims=True))
        wts = jnp.exp(dmat - m_t)
        g = jnp.exp(inter - m_t)
        sqk = _dot_nt(qb, kb) * wts
        c_prev = c_ref[h]
        n_prev = n_ref[h, 0:1, :]
        num = g * _dot(qb, c_prev.astype(BF16)) + _dot(sqk.astype(BF16), vb)
        den = g * jnp.sum(q * n_prev, -1, keepdims=True) + jnp.sum(sqk, -1, keepdims=True)
        hh = num / jnp.maximum(jnp.abs(den), jnp.exp(-m_t))

        b_last = b_col[L - 1:L, :]
        w_last_row = b_last - b_row + i_row
        m_new = jnp.maximum(b_last + m_prev, jnp.max(w_last_row, -1, keepdims=True))
        decay = jnp.exp(b_last + m_prev - m_new)
        ws_col = jnp.exp(b_last - b_col + i_col - m_new)
        kw = k * ws_col
        c_ref[h] = decay * c_prev + _dot(kw.T.astype(BF16), vb)
        n_ref[h, 0:1, :] = decay * n_prev + jnp.sum(kw, 0, keepdims=True)
        m_ref[h] = jnp.broadcast_to(m_new, m_ref.shape[1:])

        mu = jnp.mean(hh, -1, keepdims=True)
        hc = hh - mu
        var = jnp.mean(hc * hc, -1, keepdims=True)
        hn = hc * lax.rsqrt(var + LN_EPS) * ng_ref[:, h * dv:(h + 1) * dv]
        og = jax.nn.sigmoid(mo_ref[:, h * dv:(h + 1) * dv].astype(F32))
        o_ref[:, h * dv:(h + 1) * dv] = (hn * og).astype(o_ref.dtype)


def _mlstm(proj, gates, gates_t, conv_w, conv_b, b_i, b_f, norm_g):
    s = proj.shape[0]
    L = min(MLSTM_L, s)
    smem = pl.BlockSpec(memory_space=pltpu.SMEM)
    return pl.pallas_call(
        _mlstm_kernel,
        grid=(s // L,),
        in_specs=[
            smem, smem,
            pl.BlockSpec((L, M_QK), lambda c: (c, 0)),
            pl.BlockSpec((L, M_QK), lambda c: (c, 1)),
            pl.BlockSpec((L, M_V), lambda c: (c, 1)),
            pl.BlockSpec((L, M_V), lambda c: (c, 2)),
            pl.BlockSpec((L, gates.shape[1]), lambda c: (c, 0)),
            pl.BlockSpec((gates_t.shape[0], L), lambda c: (0, c)),
            pl.BlockSpec((CONV_WIDTH, 2 * M_QK), lambda c: (0, 0)),
            pl.BlockSpec((1, 2 * M_QK), lambda c: (0, 0)),
            pl.BlockSpec((1, M_V), lambda c: (0, 0)),
        ],
        out_specs=pl.BlockSpec((L, M_V), lambda c: (c, 0)),
        out_shape=jax.ShapeDtypeStruct((s, M_V), BF16),
        scratch_shapes=[
            pltpu.VMEM((MLSTM_HEADS, MLSTM_DQK, MLSTM_DV), F32),
            pltpu.VMEM((MLSTM_HEADS, V7X_SUBLANES, MLSTM_DQK), F32),
            pltpu.VMEM((MLSTM_HEADS, V7X_SUBLANES, V7X_LANES), F32),
            pltpu.VMEM((V7X_SUBLANES, 2 * M_QK), F32),
        ],
        compiler_params=_params(("arbitrary",)),
        name="mlstm",
    )(b_i, b_f, proj, proj, proj, proj, gates, gates_t, conv_w, conv_b, norm_g)


def _t5_bias_tile(table_ref, h, offset):
    n_t = V7X_LANES
    r_i = lax.broadcasted_iota(jnp.int32, (n_t, n_t), 0)
    c_i = lax.broadcasted_iota(jnp.int32, (n_t, n_t), 1)
    n = jnp.maximum(offset + r_i - c_i, 0)
    max_exact = REL_BUCKETS // 2
    large = max_exact + (jnp.log(jnp.maximum(n, 1).astype(F32) / max_exact)
                         / math.log(REL_MAX_DIST / max_exact) * (REL_BUCKETS - max_exact)).astype(jnp.int32)
    large = jnp.minimum(large, REL_BUCKETS - 1)
    bucket = jnp.where(n < max_exact, n, large)
    out = jnp.zeros((n_t, n_t), F32)
    for b in range(REL_BUCKETS):
        out = jnp.where(bucket == b, table_ref[b * DIFF_HEADS + h], out)
    return out * LOG2E


def _diff_attn_kernel(qi_ref, kj_ref, q_ref, kt_ref, v_ref, lam_ref, table_ref, o_ref,
                      q1_ref, q2_ref, vx_ref, m_ref, acc_ref, bdiag_ref, bsub_ref):
    h = pl.program_id(0)
    p = pl.program_id(1)
    i = qi_ref[p]
    j = kj_ref[p]
    BQ = q_ref.shape[0]
    BK = v_ref.shape[0]
    n_half = BQ // BK
    SR = min(ATT_STRIP, BK)
    strips_per_half = BK // SR
    n_sub = BK // V7X_LANES
    c_far = table_ref[(REL_BUCKETS - 1) * DIFF_HEADS + h] * LOG2E

    @pl.when(p == 0)
    def _():
        p0 = _t5_bias_tile(table_ref, h, 0)
        p0 = jnp.where(lax.broadcasted_iota(jnp.int32, p0.shape, 1) <= lax.broadcasted_iota(jnp.int32, p0.shape, 0),
                       p0, NEG)
        p1 = _t5_bias_tile(table_ref, h, V7X_LANES)
        far = jnp.full((V7X_LANES, V7X_LANES), c_far, F32)
        neg = jnp.full((V7X_LANES, V7X_LANES), NEG, F32)
        for a in range(n_sub):
            for b in range(n_sub):
                tile = p0 if a == b else p1 if a == b + 1 else far if a > b else neg
                bdiag_ref[a * V7X_LANES:(a + 1) * V7X_LANES, b * V7X_LANES:(b + 1) * V7X_LANES] = tile
        bsub_ref[...] = jnp.full(bsub_ref.shape, c_far, F32)
        bsub_ref[0:V7X_LANES, BK - V7X_LANES:BK] = p1

    @pl.when(j == 0)
    def _():
        q = q_ref[...].astype(F32) * ((DIFF_DH ** -0.5) * LOG2E)
        lane = lax.broadcasted_iota(jnp.int32, q.shape, 1)
        q1_ref[...] = jnp.where(lane < DIFF_DH, q, 0.0).astype(BF16)
        q2_ref[...] = jnp.where(lane >= DIFF_DH, q, 0.0).astype(BF16)
        m_ref[...] = jnp.full(m_ref.shape, NEG, F32)
        acc_ref[...] = jnp.zeros_like(acc_ref)

    vx_ref[:, 0:DIFF_DV] = v_ref[...]
    vx_ref[:, DIFF_DV:] = jnp.ones((BK, DIFF_DV), BF16)

    def strip(r, kl, bias_tile=None):
        rows = slice(r * SR, (r + 1) * SR)
        kb = kt_ref[:, 0:kl]
        vb = vx_ref[0:kl, :]
        for a, qz_ref in enumerate((q1_ref, q2_ref)):
            s = _dot(qz_ref[rows, :], kb)
            m_old = m_ref[a, rows, :]
            if bias_tile is None:
                m_new = jnp.maximum(m_old, jnp.max(s, -1, keepdims=True) + c_far)
                pm = jnp.exp2(s - (m_new - c_far))
            else:
                s = s + bias_tile
                m_new = jnp.maximum(m_old, jnp.max(s, -1, keepdims=True))
                pm = jnp.exp2(s - m_new)
            alpha = jnp.exp2(m_old - m_new)
            acc_ref[a, rows, :] = alpha * acc_ref[a, rows, :] + _dot(pm.astype(BF16), vb)
            m_ref[a, rows, :] = m_new

    def key_block(delta):
        for r in range(n_half * strips_per_half):
            ro = (r % strips_per_half) * SR
            rel = None if delta is None else delta - r // strips_per_half
            if rel is None or rel <= -2:
                strip(r, BK)
            elif rel == -1:
                strip(r, BK, bsub_ref[...] if ro == 0 else None)
            elif rel == 0:
                strip(r, ro + SR, bdiag_ref[ro:ro + SR, 0:ro + SR])

    d = j - i * n_half

    @pl.when(d <= -2)
    def _():
        key_block(None)

    for delta in range(-1, n_half):
        @pl.when(d == delta)
        def _(delta=delta):
            key_block(delta)
            if delta == n_half - 1:
                lq1, lk1, lq2, lk2 = (lam_ref[t:t + 1, :] for t in range(4))
                lam = (jnp.exp(jnp.sum(lq1 * lk1, -1, keepdims=True))
                       - jnp.exp(jnp.sum(lq2 * lk2, -1, keepdims=True)) + LAM_INIT)
                o = (acc_ref[0, :, 0:DIFF_DV] / acc_ref[0, :, DIFF_DV:DIFF_DV + 1]
                     - lam * (acc_ref[1, :, 0:DIFF_DV] / acc_ref[1, :, DIFF_DV:DIFF_DV + 1]))
                o = o * lax.rsqrt(jnp.mean(o * o, -1, keepdims=True) + LN_EPS)
                o_ref[...] = (o * (1.0 - LAM_INIT)).astype(o_ref.dtype)


def _diff_attn(proj, k_t, lam_vecs, table_flat):
    s = proj.shape[0]
    BK = min(ATT_BK, s)
    BQ = min(ATT_BQ, s)
    n_half = BQ // BK
    pairs = [(i, j) for i in range(s // BQ) for j in range((i + 1) * n_half)]
    qi = jnp.asarray([a for a, _ in pairs], jnp.int32)
    kj = jnp.asarray([b for _, b in pairs], jnp.int32)
    q_blk = (2 * M_QK + 2 * M_V) // V7X_LANES
    v_blk = q_blk + D_QK // V7X_LANES
    SR = min(ATT_STRIP, BK)
    grid_spec = pltpu.PrefetchScalarGridSpec(
        num_scalar_prefetch=2,
        grid=(DIFF_HEADS, len(pairs)),
        in_specs=[
            pl.BlockSpec((BQ, V7X_LANES), lambda h, p, qi, kj: (qi[p], q_blk + h)),
            pl.BlockSpec((2 * DIFF_DH, BK), lambda h, p, qi, kj: (h, kj[p])),
            pl.BlockSpec((BK, V7X_LANES), lambda h, p, qi, kj: (kj[p], v_blk + h)),
            pl.BlockSpec((4, DIFF_DH), lambda h, p, qi, kj: (0, 0)),
            pl.BlockSpec(memory_space=pltpu.SMEM),
        ],
        out_specs=pl.BlockSpec((BQ, V7X_LANES), lambda h, p, qi, kj: (qi[p], h)),
        scratch_shapes=[
            pltpu.VMEM((BQ, V7X_LANES), BF16),
            pltpu.VMEM((BQ, V7X_LANES), BF16),
            pltpu.VMEM((BK, 2 * DIFF_DV), BF16),
            pltpu.VMEM((2, BQ, 1), F32),
            pltpu.VMEM((2, BQ, 2 * DIFF_DV), F32),
            pltpu.VMEM((BK, BK), F32),
            pltpu.VMEM((SR, BK), F32),
        ],
    )
    return pl.pallas_call(
        _diff_attn_kernel,
        grid_spec=grid_spec,
        out_shape=jax.ShapeDtypeStruct((s, D_VW), BF16),
        compiler_params=_params(("arbitrary", "arbitrary")),
        name="diff_attn",
    )(qi, kj, proj, k_t, proj, lam_vecs, table_flat)


def _out_ln_kernel(hm_ref, hd_ref, x_ref, g0_ref, b0_ref, wo1_ref, wo2_ref, g1_ref, b1_ref, wr_ref,
                   h1_ref, h1p_ref, logit_ref):
    mix = _dot(hm_ref[...], wo1_ref[...]) + _dot(hd_ref[...], wo2_ref[...])
    h0 = _layer_norm(x_ref[...], g0_ref[...], b0_ref[...])
    h1 = _layer_norm(ALPHA * h0 + mix, g1_ref[...], b1_ref[...])
    h1_ref[...] = h1
    h1p_ref[...] = _pack_bf16_pairs(h1)
    logit_ref[...] = _dot_nt(wr_ref[...], h1.astype(BF16))


def _out_ln(hm, hd, x2, g0, b0, wo1, wo2, g1, b1, wr):
    s, d = x2.shape
    tm = min(OUT_TM, s)
    row = lambda i: (i, 0)
    fixed = lambda i: (0, 0)
    return pl.pallas_call(
        _out_ln_kernel,
        grid=(s // tm,),
        in_specs=[
            pl.BlockSpec((tm, M_V), row), pl.BlockSpec((tm, D_VW), row), pl.BlockSpec((tm, d), row),
            pl.BlockSpec((1, d), fixed), pl.BlockSpec((1, d), fixed),
            pl.BlockSpec((M_V, d), fixed), pl.BlockSpec((D_VW, d), fixed),
            pl.BlockSpec((1, d), fixed), pl.BlockSpec((1, d), fixed),
            pl.BlockSpec((N_EXPERTS, d), fixed),
        ],
        out_specs=[pl.BlockSpec((tm, d), row), pl.BlockSpec((tm, d // 2), row),
                   pl.BlockSpec((N_EXPERTS, tm), lambda i: (0, i))],
        out_shape=[jax.ShapeDtypeStruct((s, d), F32), jax.ShapeDtypeStruct((s, d // 2), jnp.uint32),
                   jax.ShapeDtypeStruct((N_EXPERTS, s), F32)],
        compiler_params=_params(("arbitrary",)),
        name="out_ln",
    )(hm, hd, x2, g0, b0, wo1, wo2, g1, b1, wr)


def _route_kernel(logit_ref, rb_ref, topi_ref, pos_ref, topw_ref, cnt_ref, carry_ref):
    n_e, tm = logit_ref.shape
    gsz = n_e // N_GROUPS
    ninf = -jnp.inf

    @pl.when(pl.program_id(0) == 0)
    def _():
        carry_ref[...] = jnp.zeros_like(carry_ref)

    sc = jax.nn.sigmoid(logit_ref[...])
    sel = sc + rb_ref[...]

    sel3 = sel.reshape(N_GROUPS, gsz, tm)
    in_grp = lax.broadcasted_iota(jnp.int32, sel3.shape, 1)
    m1 = jnp.max(sel3, 1, keepdims=True)
    i1 = jnp.min(jnp.where(sel3 == m1, in_grp, gsz), 1, keepdims=True)
    m2 = jnp.max(jnp.where(in_grp == i1, ninf, sel3), 1, keepdims=True)
    gscore = (m1 + m2).reshape(N_GROUPS, tm)
    gid = lax.broadcasted_iota(jnp.int32, gscore.shape, 0)
    beaten = jnp.zeros(gscore.shape, F32)
    for o in range(1, N_GROUPS):
        other = pltpu.roll(gscore, o, 0)
        wins = (other > gscore) | ((other == gscore) & (gid >= o))
        beaten = beaten + jnp.where(wins, 1.0, 0.0)
    keep = jnp.where(beaten < TOPK_GROUPS, 1.0, 0.0)
    keep3 = jnp.broadcast_to(keep.reshape(N_GROUPS, 1, tm), sel3.shape)
    masked = jnp.where(keep3 > 0.5, sel3, ninf).reshape(n_e, tm)

    eid = lax.broadcasted_iota(jnp.int32, (n_e, tm), 0)
    onehots, idxs, ws = [], [], []
    for _ in range(TOP_K):
        m = jnp.max(masked, 0, keepdims=True)
        idx = jnp.min(jnp.where(masked == m, eid, n_e), 0, keepdims=True)
        oh = eid == idx
        onehots.append(oh)
        idxs.append(idx)
        ws.append(jnp.sum(jnp.where(oh, sc, 0.0), 0, keepdims=True))
        masked = jnp.where(oh, ninf, masked)
    wsum = ws[0]
    for w in ws[1:]:
        wsum = wsum + w

    chosen = jnp.zeros((n_e, tm), F32)
    for oh in onehots:
        chosen = jnp.where(oh, 1.0, chosen)
    r_i = lax.broadcasted_iota(jnp.int32, (tm, tm), 0)
    c_i = lax.broadcasted_iota(jnp.int32, (tm, tm), 1)
    before = jnp.where(r_i < c_i, 1.0, 0.0).astype(BF16)
    rank = _dot(chosen.astype(BF16), before) + carry_ref[...]
    carry_new = carry_ref[...] + jnp.sum(chosen, 1, keepdims=True)
    carry_ref[...] = carry_new
    cnt_ref[...] = carry_new

    kid = lax.broadcasted_iota(jnp.int32, (TOP_K, tm), 0)
    topi = jnp.zeros((TOP_K, tm), jnp.int32)
    pos = jnp.zeros((TOP_K, tm), jnp.int32)
    topw = jnp.zeros((TOP_K, tm), F32)
    for k in range(TOP_K):
        pk = jnp.sum(jnp.where(onehots[k], rank, 0.0), 0, keepdims=True)
        topi = jnp.where(kid == k, idxs[k], topi)
        pos = jnp.where(kid == k, pk.astype(jnp.int32), pos)
        topw = jnp.where(kid == k, ws[k] / wsum * ROUTED_SCALE, topw)
    topi_ref[...] = topi
    pos_ref[...] = pos
    topw_ref[...] = topw


def _route(logits_t, router_bias_col):
    n_e, t = logits_t.shape
    tm = min(ROUTE_TM, t)
    col = lambda i: (0, i)
    fixed = lambda i: (0, 0)
    return pl.pallas_call(
        _route_kernel,
        grid=(t // tm,),
        in_specs=[pl.BlockSpec((n_e, tm), col), pl.BlockSpec((n_e, 1), fixed)],
        out_specs=[pl.BlockSpec((TOP_K, tm), col), pl.BlockSpec((TOP_K, tm), col),
                   pl.BlockSpec((TOP_K, tm), col), pl.BlockSpec((n_e, 1), fixed)],
        out_shape=[jax.ShapeDtypeStruct((TOP_K, t), jnp.int32), jax.ShapeDtypeStruct((TOP_K, t), jnp.int32),
                   jax.ShapeDtypeStruct((TOP_K, t), F32), jax.ShapeDtypeStruct((n_e, 1), F32)],
        scratch_shapes=[pltpu.VMEM((n_e, 1), F32)],
        compiler_params=_params(("arbitrary",)),
        name="route",
    )(logits_t, router_bias_col)


def _plan_kernel(cnt_ref, cntc_ref, topi_ref, pos_ref, off_ref, vexp_ref, vblk_ref, vcnt_ref, nvis_ref, dest_ref):
    nv = vexp_ref.shape[0]
    tr = float(MOE_TR)
    cnt = jnp.broadcast_to(cnt_ref[...], (V7X_SUBLANES, N_EXPERTS))
    r_i = lax.broadcasted_iota(jnp.int32, (N_EXPERTS, N_EXPERTS), 0)
    c_i = lax.broadcasted_iota(jnp.int32, (N_EXPERTS, N_EXPERTS), 1)
    upper = jnp.where(r_i <= c_i, 1.0, 0.0).astype(BF16)

    def cumsum_lanes(a):
        a1, a2, a3 = _split3(a)
        return _dot(a1, upper) + _dot(a2, upper) + _dot(a3, upper)

    end = cumsum_lanes(cnt)
    start = end - cnt
    first_blk = jnp.floor(start / tr)
    last_blk = jnp.floor((end - 1.0) / tr)
    nvis_e = jnp.where(cnt > 0.0, last_blk - first_blk + 1.0, 0.0)
    vend = cumsum_lanes(nvis_e)
    vstart = vend - nvis_e
    off_ref[...] = start[0:1, :].astype(jnp.int32)
    nvis_ref[...] = vend[0:1, N_EXPERTS - 1:N_EXPERTS].astype(jnp.int32)

    v = lax.broadcasted_iota(jnp.int32, (nv, N_EXPERTS), 0).astype(F32)
    ve = jnp.broadcast_to(vend[0:1, :], (nv, N_EXPERTS))
    expert = jnp.sum(jnp.where(ve <= v, 1.0, 0.0), -1, keepdims=True)
    expert = jnp.minimum(expert, N_EXPERTS - 1.0)
    lane = lax.broadcasted_iota(jnp.int32, (nv, N_EXPERTS), 1).astype(F32)
    mine = lane == expert
    fb = jnp.sum(jnp.where(mine, jnp.broadcast_to(first_blk[0:1, :], (nv, N_EXPERTS)), 0.0), -1, keepdims=True)
    vs = jnp.sum(jnp.where(mine, jnp.broadcast_to(vstart[0:1, :], (nv, N_EXPERTS)), 0.0), -1, keepdims=True)
    vc = jnp.sum(jnp.where(mine, jnp.broadcast_to(nvis_e[0:1, :], (nv, N_EXPERTS)), 0.0), -1, keepdims=True)
    vexp_ref[...] = expert.astype(jnp.int32)
    vblk_ref[...] = (fb + (v[:, 0:1] - vs)).astype(jnp.int32)
    vcnt_ref[...] = vc.astype(jnp.int32)

    lower = jnp.where(c_i < r_i, 1.0, 0.0).astype(BF16)
    c1, c2, c3 = _split3(jnp.broadcast_to(cntc_ref[...], (N_EXPERTS, V7X_LANES)))
    start_col = (_dot(lower, c1) + _dot(lower, c2) + _dot(lower, c3))[:, 0:1]
    n_tok = topi_ref.shape[1]
    tb = min(PLAN_TB, n_tok)
    eid = lax.broadcasted_iota(jnp.int32, (N_EXPERTS, tb), 0)
    kid = lax.broadcasted_iota(jnp.int32, (TOP_K, tb), 0)

    def dest_block(i, carry):
        cols = pl.ds(pl.multiple_of(i * tb, tb), tb)
        ti = topi_ref[:, cols]
        first_row = jnp.zeros((TOP_K, tb), F32)
        for k in range(TOP_K):
            fr = jnp.sum(jnp.where(eid == ti[k:k + 1, :], start_col, 0.0), 0, keepdims=True)
            first_row = jnp.where(kid == k, fr, first_row)
        dest_ref[:, cols] = first_row.astype(jnp.int32) + pos_ref[:, cols]
        return carry

    lax.fori_loop(0, n_tok // tb, dest_block, 0)


def _plan(counts_row, counts_col, topi_t, pos_t):
    n_rows = topi_t.shape[0] * topi_t.shape[1]
    nv = n_rows // MOE_TR + N_EXPERTS
    return pl.pallas_call(
        _plan_kernel,
        out_shape=[jax.ShapeDtypeStruct((1, N_EXPERTS), jnp.int32), jax.ShapeDtypeStruct((nv, 1), jnp.int32),
                   jax.ShapeDtypeStruct((nv, 1), jnp.int32), jax.ShapeDtypeStruct((nv, 1), jnp.int32),
                   jax.ShapeDtypeStruct((1, 1), jnp.int32), jax.ShapeDtypeStruct(topi_t.shape, jnp.int32)],
        compiler_params=pltpu.CompilerParams(vmem_limit_bytes=V7X_VMEM_LIMIT_BYTES),
        name="plan",
    )(counts_row, counts_col, topi_t, pos_t)


def _dispatch_kernel(dest_ref, x_ref, xs_ref, sem):
    tm = x_ref.shape[0]

    def issue(t, carry):
        for k in range(TOP_K):
            d = dest_ref[t * TOP_K + k]
            pltpu.make_async_copy(x_ref.at[pl.ds(t, 1)], xs_ref.at[pl.ds(d, 1)], sem).start(priority=k % 2)
        return carry

    lax.fori_loop(0, tm, issue, 0)
    for _ in range(TOP_K):
        pltpu.make_async_copy(x_ref, xs_ref.at[pl.ds(0, tm)], sem).wait()


def _dispatch(dest_flat, rows):
    t, d = rows.shape
    tm = min(MOVE_TM, t)
    return pl.pallas_call(
        _dispatch_kernel,
        grid=(t // tm,),
        in_specs=[
            pl.BlockSpec((tm * TOP_K,), lambda i: (i,), memory_space=pltpu.SMEM),
            pl.BlockSpec((tm, d), lambda i: (i, 0)),
        ],
        out_specs=pl.BlockSpec(memory_space=pl.ANY),
        out_shape=jax.ShapeDtypeStruct((t * TOP_K, d), rows.dtype),
        scratch_shapes=[pltpu.SemaphoreType.DMA(())],
        compiler_params=_params(("arbitrary",)),
        name="dispatch",
    )(dest_flat, rows)


def _experts_kernel(n_shared_steps, vexp_ref, vblk_ref, vcnt_ref, off_ref, nvis_ref, xs_ref, wg_hbm, wu_hbm,
                    wd_hbm, hp_ref, sg_ref, su_ref, sd_ref, ys_ref, ysh_ref,
                    wg_buf, wu_buf, wd_buf, acc_ref, slot_ref, sem):
    v = pl.program_id(0)
    tr = xs_ref.shape[0]
    nvis = nvis_ref[0]

    @pl.when(v < n_shared_steps)
    def _():
        hb = _unpack_bf16_pairs(hp_ref[...]).astype(BF16)
        gate = _dot(hb, sg_ref[...])
        up = _dot(hb, su_ref[...])
        ysh_ref[...] = _dot((gate * jax.nn.sigmoid(gate) * up).astype(BF16), sd_ref[...])

    half = wd_buf.shape[1] // 2

    def weight_copies(e, slot):
        return ((pltpu.make_async_copy(wg_hbm.at[e], wg_buf.at[slot], sem.at[slot, 0]), 0),
                (pltpu.make_async_copy(wu_hbm.at[e], wu_buf.at[slot], sem.at[slot, 1]), 1),
                (pltpu.make_async_copy(wd_hbm.at[e, pl.ds(0, half)], wd_buf.at[slot, pl.ds(0, half)],
                                       sem.at[slot, 2]), 0),
                (pltpu.make_async_copy(wd_hbm.at[e, pl.ds(half, half)], wd_buf.at[slot, pl.ds(half, half)],
                                       sem.at[slot, 3]), 1))

    @pl.when(v < nvis)
    def _():
        e = vexp_ref[v]
        blk = vblk_ref[v]
        first_of_expert = jnp.logical_or(v == 0, vexp_ref[jnp.maximum(v - 1, 0)] != e)

        @pl.when(v == 0)
        def _():
            slot_ref[0] = 0
            for c, queue in weight_copies(e, 0):
                c.start(priority=queue)

        @pl.when(jnp.logical_and(first_of_expert, v > 0))
        def _():
            slot_ref[0] = 1 - slot_ref[0]

        slot = slot_ref[0]

        @pl.when(first_of_expert)
        def _():
            for c, _ in weight_copies(e, slot):
                c.wait()
            nxt = v + vcnt_ref[v]

            @pl.when(nxt < nvis)
            def _():
                for c, queue in weight_copies(vexp_ref[jnp.minimum(nxt, vexp_ref.shape[0] - 1)], 1 - slot):
                    c.start(priority=queue)

        lo = off_ref[e]
        hi = off_ref[e + 1]
        row = blk * tr + lax.broadcasted_iota(jnp.int32, (tr, 1), 0)
        mine = (row >= lo) & (row < hi)
        xb = _unpack_bf16_pairs(xs_ref[...]).astype(BF16)
        gate = _dot(xb, wg_buf[slot].astype(BF16))
        up = _dot(xb, wu_buf[slot].astype(BF16))
        act = (gate * jax.nn.sigmoid(gate) * up).astype(BF16)
        y = jnp.where(mine, _dot(act, wd_buf[slot].astype(BF16)), 0.0)
        first_of_block = jnp.logical_or(v == 0, vblk_ref[jnp.maximum(v - 1, 0)] != blk)

        @pl.when(first_of_block)
        def _():
            acc_ref[...] = y

        @pl.when(jnp.logical_not(first_of_block))
        def _():
            acc_ref[...] += y

        ys_ref[...] = _pack_bf16_pairs(acc_ref[...])


def _experts(vexp, vblk, vcnt, off_ext, nvis, xs, w_gate, w_up, w_down, h1p, sg, su, sd):
    n_rows, dp = xs.shape
    d = 2 * dp
    nv = vexp.shape[0]
    t = h1p.shape[0]
    tm = min(SHARED_TM, t)
    n_shared_steps = t // tm
    assert n_shared_steps <= n_rows // MOE_TR

    def block_of(v, ve, vb, vc, off, nvis):
        return (vb[jnp.minimum(v, nvis[0] - 1)], 0)

    def shared_block(v, ve, vb, vc, off, nvis):
        return (jnp.minimum(v, n_shared_steps - 1), 0)

    fixed = lambda v, ve, vb, vc, off, nvis: (0, 0)
    resident = dict(pipeline_mode=pl.Buffered(1))
    grid_spec = pltpu.PrefetchScalarGridSpec(
        num_scalar_prefetch=5,
        grid=(nv,),
        in_specs=[
            pl.BlockSpec((MOE_TR, dp), block_of),
            pl.BlockSpec(memory_space=pl.ANY),
            pl.BlockSpec(memory_space=pl.ANY),
            pl.BlockSpec(memory_space=pl.ANY),
            pl.BlockSpec((tm, dp), shared_block),
            pl.BlockSpec((d, D_EXPERT), fixed, **resident),
            pl.BlockSpec((d, D_EXPERT), fixed, **resident),
            pl.BlockSpec((D_EXPERT, d), fixed, **resident),
        ],
        out_specs=[pl.BlockSpec((MOE_TR, dp), block_of), pl.BlockSpec((tm, d), shared_block)],
        scratch_shapes=[
            pltpu.VMEM((2, d, D_EXPERT), F32),
            pltpu.VMEM((2, d, D_EXPERT), F32),
            pltpu.VMEM((2, D_EXPERT, d), F32),
            pltpu.VMEM((MOE_TR, d), F32),
            pltpu.SMEM((1,), jnp.int32),
            pltpu.SemaphoreType.DMA((2, 4)),
        ],
    )
    return pl.pallas_call(
        functools.partial(_experts_kernel, n_shared_steps),
        grid_spec=grid_spec,
        out_shape=[jax.ShapeDtypeStruct((n_rows, dp), jnp.uint32), jax.ShapeDtypeStruct((t, d), F32)],
        compiler_params=_params(("arbitrary",)),
        name="experts",
    )(vexp, vblk, vcnt, off_ext, nvis, xs, w_gate, w_up, w_down, h1p, sg, su, sd)


def _combine_kernel(dest_ref, topw_ref, h1_ref, ysh_ref, ys_ref, g2_ref, b2_ref, o_ref, buf_ref, sem):
    tm = h1_ref.shape[0]

    def issue(t, carry):
        for k in range(TOP_K):
            d = dest_ref[t * TOP_K + k]
            pltpu.make_async_copy(ys_ref.at[pl.ds(d, 1)], buf_ref.at[k, pl.ds(t, 1)], sem).start(priority=k % 2)
        return carry

    lax.fori_loop(0, tm, issue, 0)

    y = ALPHA * h1_ref[...] + ysh_ref[...]
    for k in range(TOP_K):
        pltpu.make_async_copy(ys_ref.at[pl.ds(0, tm)], buf_ref.at[k], sem).wait()
    w = topw_ref[...]
    for k in range(TOP_K):
        y = y + _unpack_bf16_pairs(buf_ref[k]) * w[:, k:k + 1]
    o_ref[...] = _layer_norm(y, g2_ref[...], b2_ref[...])


def _combine(dest_flat, topw, h1, ysh, ys, g2, b2):
    t, d = h1.shape
    tm = min(MOVE_TM, t)
    row = lambda i: (i, 0)
    fixed = lambda i: (0, 0)
    return pl.pallas_call(
        _combine_kernel,
        grid=(t // tm,),
        in_specs=[
            pl.BlockSpec((tm * TOP_K,), lambda i: (i,), memory_space=pltpu.SMEM),
            pl.BlockSpec((tm, TOP_K), row),
            pl.BlockSpec((tm, d), row),
            pl.BlockSpec((tm, d), row),
            pl.BlockSpec(memory_space=pl.ANY),
            pl.BlockSpec((1, d), fixed), pl.BlockSpec((1, d), fixed),
        ],
        out_specs=pl.BlockSpec((tm, d), row),
        out_shape=jax.ShapeDtypeStruct((t, d), F32),
        scratch_shapes=[pltpu.VMEM((TOP_K, tm, ys.shape[1]), ys.dtype), pltpu.SemaphoreType.DMA(())],
        compiler_params=_params(("arbitrary",)),
        name="combine",
    )(dest_flat, topw, h1, ysh, ys, g2, b2)


def kernel(x, ln_emb_g, ln_emb_b, w_in, conv_w, conv_b, b_igate, b_fgate, mlstm_norm_g, lambda_q1, lambda_k1,
           lambda_q2, lambda_k2, rel_bias, w_out, ln1_g, ln1_b, w_router, router_bias, w_gate, w_up, w_down,
           ws_gate, ws_up, ws_down, ln2_g, ln2_b):
    bsz, s, d = x.shape
    assert bsz == 1 and w_in.shape[0] == DEPTH == 1
    x2 = x.reshape(s, d)
    row = lambda a: a.reshape(1, -1).astype(F32)

    w = w_in[0]
    c0 = 2 * M_QK + 2 * M_V
    c1 = c0 + 2 * MLSTM_HEADS
    c2 = c1 + D_QK
    c3 = c2 + D_QK
    w_main = jnp.concatenate([w[:, :c0], w[:, c1:c2], w[:, c3:]], axis=1).astype(BF16)
    w_k_t = w[:, c2:c3].T.astype(BF16)
    wi, wf = w[:, c0:c0 + MLSTM_HEADS], w[:, c0 + MLSTM_HEADS:c1]
    w_gates = jnp.zeros((d, MLSTM_HEADS, V7X_LANES), F32).at[:, :, 0].set(wi).at[:, :, 1].set(wf)
    w_gates = w_gates.reshape(d, MLSTM_HEADS * V7X_LANES).astype(BF16)
    w_gates_t = jnp.zeros((MLSTM_HEADS, V7X_SUBLANES, d), F32).at[:, 0].set(wi.T).at[:, 1].set(wf.T)
    w_gates_t = w_gates_t.reshape(MLSTM_HEADS * V7X_SUBLANES, d).astype(BF16)

    proj, gates, gates_t, k_t = _ln_proj(x2, row(ln_emb_g), row(ln_emb_b), w_main, w_gates, w_gates_t, w_k_t)

    hm = _mlstm(proj, gates, gates_t, conv_w[0], row(conv_b[0]), b_igate[0].astype(F32),
                b_fgate[0].astype(F32), row(mlstm_norm_g[0]))

    lam_vecs = jnp.stack([lambda_q1[0], lambda_k1[0], lambda_q2[0], lambda_k2[0]]).astype(F32)
    hd = _diff_attn(proj, k_t, lam_vecs, rel_bias.astype(F32).reshape(-1))

    wo = w_out[0].astype(BF16)
    h1, h1p, logits_t = _out_ln(hm, hd, x2, row(ln_emb_g), row(ln_emb_b), wo[:M_V], wo[M_V:], row(ln1_g[0]),
                           row(ln1_b[0]), w_router[0].T.astype(BF16))

    topi_t, pos_t, topw_t, counts = _route(logits_t, router_bias[0].reshape(-1, 1).astype(F32))
    off, vexp, vblk, vcnt, nvis, dest_t = _plan(counts.reshape(1, -1), counts, topi_t, pos_t)
    off_ext = jnp.concatenate([off.reshape(-1), jnp.full((1,), s * TOP_K, jnp.int32)])
    dest_flat = dest_t.T.reshape(-1)

    xs = _dispatch(dest_flat, h1p)
    ys, ysh = _experts(vexp.reshape(-1), vblk.reshape(-1), vcnt.reshape(-1), off_ext, nvis.reshape(-1), xs,
                       w_gate[0], w_up[0], w_down[0], h1p, ws_gate[0].astype(BF16), ws_up[0].astype(BF16),
                       ws_down[0].astype(BF16))
    out = _combine(dest_flat, topw_t.T, h1, ysh, ys, row(ln2_g[0]), row(ln2_b[0]))
    return out.reshape(bsz, s, d)
```

```python
import functools
import math

import jax
import jax.numpy as jnp
from jax import lax
from jax.experimental import pallas as pl
from jax.experimental.pallas import tpu as pltpu

F32 = jnp.float32
BF16 = jnp.bfloat16

DEPTH = 1
MLSTM_HEADS = 4
MLSTM_DQK = 128
MLSTM_DV = 256
CONV_WIDTH = 4
DIFF_HEADS = 8
DIFF_DH = 64
DIFF_DV = 2 * DIFF_DH
REL_BUCKETS = 32
REL_MAX_DIST = 128
N_EXPERTS = 256
TOP_K = 8
N_GROUPS = 8
TOPK_GROUPS = 4
D_EXPERT = 512
ROUTED_SCALE = 2.5
LN_EPS = 1e-5
ALPHA = (2 * DEPTH) ** 0.25
LAM_INIT = 0.8 - 0.6 * math.exp(-0.3 * 0)

M_QK = MLSTM_HEADS * MLSTM_DQK
M_V = MLSTM_HEADS * MLSTM_DV
D_QK = DIFF_HEADS * 2 * DIFF_DH
D_VW = DIFF_HEADS * DIFF_DV
PROJ_W = 2 * M_QK + 2 * M_V + D_QK + D_VW
D_MODEL = M_V + D_VW

V7X_LANES = 128
V7X_SUBLANES = 8
V7X_VMEM_LIMIT_BYTES = 56 * 1024 * 1024
ROW_TILE = D_MODEL // 2 // V7X_LANES

PROJ_TM = 1024
PROJ_TN = 512
MLSTM_L = 256
ATT_BK = 1024
ATT_BQ = 2048
ATT_STRIP = 256
OUT_TM = 512
ROUTE_TM = 256
MOE_TR = 256
MOVE_TM = 256
SHARED_TM = 256
PLAN_TB = 512

NEG = -1e30
LOG2E = 1.4426950408889634


def _params(semantics):
    return pltpu.CompilerParams(dimension_semantics=semantics, vmem_limit_bytes=V7X_VMEM_LIMIT_BYTES)


def _layer_norm(x, g, b):
    mu = jnp.mean(x, -1, keepdims=True)
    xc = x - mu
    var = jnp.mean(xc * xc, -1, keepdims=True)
    return xc * lax.rsqrt(var + LN_EPS) * g + b


def _dot(a, b):
    return jnp.dot(a, b, preferred_element_type=F32)


def _dot_nt(a, b):
    return lax.dot_general(a, b, (((1,), (1,)), ((), ())), preferred_element_type=F32)


def _split3(a):
    a1 = a.astype(BF16)
    r1 = a - a1.astype(F32)
    a2 = r1.astype(BF16)
    a3 = (r1 - a2.astype(F32)).astype(BF16)
    return a1, a2, a3


def _pack_bf16_pairs(x):
    n = x.shape[1] // 2
    bits = lax.bitcast_convert_type(x.astype(BF16).astype(F32), jnp.uint32)
    return lax.shift_right_logical(bits[:, :n], jnp.uint32(16)) | (bits[:, n:] & jnp.uint32(0xFFFF0000))


def _unpack_bf16_pairs(u):
    lo = lax.bitcast_convert_type(lax.shift_left(u, jnp.uint32(16)), F32)
    hi = lax.bitcast_convert_type(u & jnp.uint32(0xFFFF0000), F32)
    return jnp.concatenate([lo, hi], axis=1)


def _store_row_tiles(ref, x, r0=0):
    m, n = x.shape[0], x.shape[1] // V7X_LANES
    for c in range(n):
        ref[pl.ds(r0 * n + c, m, stride=n), :] = x[:, c * V7X_LANES:(c + 1) * V7X_LANES]


def _load_row_tiles(ref, m, n, r0=0):
    return jnp.concatenate([ref[pl.ds(r0 * n + c, m, stride=n), :] for c in range(n)], axis=1)


def _log_sigmoid(x):
    return jnp.minimum(x, 0.0) - jnp.log(1.0 + jnp.exp(-jnp.abs(x)))


def _ln_proj_kernel(x_ref, g_ref, b_ref, w_ref, wg_ref, wgt_ref, wkt_ref, o_ref, gates_ref, gatest_ref, kt_ref,
                    hb_ref):
    @pl.when(pl.program_id(1) == 0)
    def _():
        hb = _layer_norm(x_ref[...], g_ref[...], b_ref[...]).astype(BF16)
        hb_ref[...] = hb
        gates_ref[...] = _dot(hb, wg_ref[...])
        gatest_ref[...] = _dot_nt(wgt_ref[...], hb)
        kt_ref[...] = _dot_nt(wkt_ref[...], hb).astype(kt_ref.dtype)

    o_ref[...] = _dot(hb_ref[...], w_ref[...]).astype(o_ref.dtype)


def _ln_proj(x2, g, b, w_main, w_gates, w_gates_t, w_k_t):
    s, d = x2.shape
    tm, tn = min(PROJ_TM, s), PROJ_TN
    ng = w_gates.shape[1]
    fixed = lambda i, j: (0, 0)
    return pl.pallas_call(
        _ln_proj_kernel,
        grid=(s // tm, PROJ_W // tn),
        in_specs=[
            pl.BlockSpec((tm, d), lambda i, j: (i, 0)),
            pl.BlockSpec((1, d), fixed),
            pl.BlockSpec((1, d), fixed),
            pl.BlockSpec((d, tn), lambda i, j: (0, j)),
            pl.BlockSpec((d, ng), fixed),
            pl.BlockSpec((w_gates_t.shape[0], d), fixed),
            pl.BlockSpec((D_QK, d), fixed),
        ],
        out_specs=[
            pl.BlockSpec((tm, tn), lambda i, j: (i, j)),
            pl.BlockSpec((tm, ng), lambda i, j: (i, 0)),
            pl.BlockSpec((w_gates_t.shape[0], tm), lambda i, j: (0, i)),
            pl.BlockSpec((D_QK, tm), lambda i, j: (0, i)),
        ],
        out_shape=[
            jax.ShapeDtypeStruct((s, PROJ_W), BF16),
            jax.ShapeDtypeStruct((s, ng), F32),
            jax.ShapeDtypeStruct((w_gates_t.shape[0], s), F32),
            jax.ShapeDtypeStruct((D_QK, s), BF16),
        ],
        scratch_shapes=[pltpu.VMEM((tm, d), BF16)],
        compiler_params=_params(("arbitrary", "arbitrary")),
        name="ln_proj",
    )(x2, g, b, w_main, w_gates, w_gates_t, w_k_t)


def _mlstm_kernel(bi_ref, bf_ref, mq_ref, mk_ref, mv_ref, mo_ref, gates_ref, gatest_ref,
                  cw_ref, cb_ref, ng_ref, o_ref, c_ref, n_ref, m_ref, tail_ref):
    L = mq_ref.shape[0]
    dk, dv = MLSTM_DQK, MLSTM_DV

    @pl.when(pl.program_id(0) == 0)
    def _():
        c_ref[...] = jnp.zeros_like(c_ref)
        n_ref[...] = jnp.zeros_like(n_ref)
        m_ref[...] = jnp.zeros_like(m_ref)
        tail_ref[...] = jnp.zeros_like(tail_ref)

    u = jnp.concatenate([mq_ref[...], mk_ref[...]], axis=1).astype(F32)
    tail = tail_ref[...]
    row8 = lax.broadcasted_iota(jnp.int32, (V7X_SUBLANES, u.shape[1]), 0)
    conv = cb_ref[...] + cw_ref[CONV_WIDTH - 1:CONV_WIDTH, :] * u
    for back in range(1, CONV_WIDTH):
        ur = pltpu.roll(u, back, 0)
        head = jnp.where(row8 < back, pltpu.roll(tail, back, 0), ur[:V7X_SUBLANES])
        shifted = jnp.concatenate([head, ur[V7X_SUBLANES:]], axis=0)
        conv = conv + cw_ref[CONV_WIDTH - 1 - back:CONV_WIDTH - back, :] * shifted
    tail_ref[...] = u[L - V7X_SUBLANES:]
    qk = conv * jax.nn.sigmoid(conv)
    q_all = qk[:, :M_QK]
    k_all = qk[:, M_QK:] * (dk ** -0.5)

    r_i = lax.broadcasted_iota(jnp.int32, (L, L), 0)
    c_i = lax.broadcasted_iota(jnp.int32, (L, L), 1)
    causal = c_i <= r_i
    tril = jnp.where(causal, 1.0, 0.0).astype(BF16)
    triu = jnp.where(r_i <= c_i, 1.0, 0.0).astype(BF16)

    for h in range(MLSTM_HEADS):
        q = q_all[:, h * dk:(h + 1) * dk]
        k = k_all[:, h * dk:(h + 1) * dk]
        qb, kb = q.astype(BF16), k.astype(BF16)
        vb = mv_ref[:, h * dv:(h + 1) * dv]
        b_i, b_f = bi_ref[h], bf_ref[h]

        gblk = gates_ref[:, h * V7X_LANES:(h + 1) * V7X_LANES]
        i_col = gblk[:, 0:1] + b_i
        lf_blk = _log_sigmoid(gblk + b_f)
        f1, f2, f3 = _split3(lf_blk)
        bcum_blk = _dot(tril, f1) + _dot(tril, f2) + _dot(tril, f3)
        b_col = bcum_blk[:, 1:2]
        gt = gatest_ref[h * V7X_SUBLANES:(h + 1) * V7X_SUBLANES, :]
        i_row = gt[0:1, :] + b_i
        lf_rows = _log_sigmoid(gt + b_f)
        g1, g2, g3 = _split3(lf_rows)
        b_row = (_dot(g1, triu) + _dot(g2, triu) + _dot(g3, triu))[1:2, :]

        m_prev = m_ref[h, 0:1, 0:1]
        dmat = jnp.where(causal, b_col - b_row + i_row, NEG)
        inter = b_col + m_prev
        m_t = jnp.maximum(inter, jnp.max(dmat, -1, keepdims=True))
        wts = jnp.exp(dmat - m_t)
        g = jnp.exp(inter - m_t)
        sqk = _dot_nt(qb, kb) * wts
        c_prev = c_ref[h]
        n_prev = n_ref[h, 0:1, :]
        num = g * _dot(qb, c_prev.astype(BF16)) + _dot(sqk.astype(BF16), vb)
        den = g * jnp.sum(q * n_prev, -1, keepdims=True) + jnp.sum(sqk, -1, keepdims=True)
        hh = num / jnp.maximum(jnp.abs(den), jnp.exp(-m_t))

        b_last = b_col[L - 1:L, :]
        w_last_row = b_last - b_row + i_row
        m_new = jnp.maximum(b_last + m_prev, jnp.max(w_last_row, -1, keepdims=True))
        decay = jnp.exp(b_last + m_prev - m_new)
        ws_col = jnp.exp(b_last - b_col + i_col - m_new)
        kw = k * ws_col
        c_ref[h] = decay * c_prev + _dot(kw.T.astype(BF16), vb)
        n_ref[h, 0:1, :] = decay * n_prev + jnp.sum(kw, 0, keepdims=True)
        m_ref[h] = jnp.broadcast_to(m_new, m_ref.shape[1:])

        mu = jnp.mean(hh, -1, keepdims=True)
        hc = hh - mu
        var = jnp.mean(hc * hc, -1, keepdims=True)
        hn = hc * lax.rsqrt(var + LN_EPS) * ng_ref[:, h * dv:(h + 1) * dv]
        og = jax.nn.sigmoid(mo_ref[:, h * dv:(h + 1) * dv].astype(F32))
        o_ref[:, h * dv:(h + 1) * dv] = (hn * og).astype(o_ref.dtype)


def _mlstm(proj, gates, gates_t, conv_w, conv_b, b_i, b_f, norm_g):
    s = proj.shape[0]
    L = min(MLSTM_L, s)
    smem = pl.BlockSpec(memory_space=pltpu.SMEM)
    return pl.pallas_call(
        _mlstm_kernel,
        grid=(s // L,),
        in_specs=[
            smem, smem,
            pl.BlockSpec((L, M_QK), lambda c: (c, 0)),
            pl.BlockSpec((L, M_QK), lambda c: (c, 1)),
            pl.BlockSpec((L, M_V), lambda c: (c, 1)),
            pl.BlockSpec((L, M_V), lambda c: (c, 2)),
            pl.BlockSpec((L, gates.shape[1]), lambda c: (c, 0)),
            pl.BlockSpec((gates_t.shape[0], L), lambda c: (0, c)),
            pl.BlockSpec((CONV_WIDTH, 2 * M_QK), lambda c: (0, 0)),
            pl.BlockSpec((1, 2 * M_QK), lambda c: (0, 0)),
            pl.BlockSpec((1, M_V), lambda c: (0, 0)),
        ],
        out_specs=pl.BlockSpec((L, M_V), lambda c: (c, 0)),
        out_shape=jax.ShapeDtypeStruct((s, M_V), BF16),
        scratch_shapes=[
            pltpu.VMEM((MLSTM_HEADS, MLSTM_DQK, MLSTM_DV), F32),
            pltpu.VMEM((MLSTM_HEADS, V7X_SUBLANES, MLSTM_DQK), F32),
            pltpu.VMEM((MLSTM_HEADS, V7X_SUBLANES, V7X_LANES), F32),
            pltpu.VMEM((V7X_SUBLANES, 2 * M_QK), F32),
        ],
        compiler_params=_params(("arbitrary",)),
        name="mlstm",
    )(b_i, b_f, proj, proj, proj, proj, gates, gates_t, conv_w, conv_b, norm_g)


def _t5_bias_tile(table_ref, h, offset):
    n_t = V7X_LANES
    r_i = lax.broadcasted_iota(jnp.int32, (n_t, n_t), 0)
    c_i = lax.broadcasted_iota(jnp.int32, (n_t, n_t), 1)
    n = jnp.maximum(offset + r_i - c_i, 0)
    max_exact = REL_BUCKETS // 2
    large = max_exact + (jnp.log(jnp.maximum(n, 1).astype(F32) / max_exact)
                         / math.log(REL_MAX_DIST / max_exact) * (REL_BUCKETS - max_exact)).astype(jnp.int32)
    large = jnp.minimum(large, REL_BUCKETS - 1)
    bucket = jnp.where(n < max_exact, n, large)
    out = jnp.zeros((n_t, n_t), F32)
    for b in range(REL_BUCKETS):
        out = jnp.where(bucket == b, table_ref[b * DIFF_HEADS + h], out)
    return out * LOG2E


def _diff_attn_kernel(qi_ref, kj_ref, q_ref, kt_ref, v_ref, lam_ref, table_ref, o_ref,
                      q1_ref, q2_ref, vx_ref, m_ref, acc_ref, bdiag_ref, bsub_ref):
    h = pl.program_id(0)
    p = pl.program_id(1)
    i = qi_ref[p]
    j = kj_ref[p]
    BQ = q_ref.shape[0]
    BK = v_ref.shape[0]
    n_half = BQ // BK
    SR = min(ATT_STRIP, BK)
    strips_per_half = BK // SR
    n_sub = BK // V7X_LANES
    c_far = table_ref[(REL_BUCKETS - 1) * DIFF_HEADS + h] * LOG2E

    @pl.when(p == 0)
    def _():
        p0 = _t5_bias_tile(table_ref, h, 0)
        p0 = jnp.where(lax.broadcasted_iota(jnp.int32, p0.shape, 1) <= lax.broadcasted_iota(jnp.int32, p0.shape, 0),
                       p0, NEG)
        p1 = _t5_bias_tile(table_ref, h, V7X_LANES)
        far = jnp.full((V7X_LANES, V7X_LANES), c_far, F32)
        neg = jnp.full((V7X_LANES, V7X_LANES), NEG, F32)
        for a in range(n_sub):
            for b in range(n_sub):
                tile = p0 if a == b else p1 if a == b + 1 else far if a > b else neg
                bdiag_ref[a * V7X_LANES:(a + 1) * V7X_LANES, b * V7X_LANES:(b + 1) * V7X_LANES] = tile
        bsub_ref[...] = jnp.full(bsub_ref.shape, c_far, F32)
        bsub_ref[0:V7X_LANES, BK - V7X_LANES:BK] = p1

    @pl.when(j == 0)
    def _():
        q = q_ref[...].astype(F32) * ((DIFF_DH ** -0.5) * LOG2E)
        lane = lax.broadcasted_iota(jnp.int32, q.shape, 1)
        q1_ref[...] = jnp.where(lane < DIFF_DH, q, 0.0).astype(BF16)
        q2_ref[...] = jnp.where(lane >= DIFF_DH, q, 0.0).astype(BF16)
        m_ref[...] = jnp.full(m_ref.shape, NEG, F32)
        acc_ref[...] = jnp.zeros_like(acc_ref)

    vx_ref[:, 0:DIFF_DV] = v_ref[...]
    vx_ref[:, DIFF_DV:] = jnp.ones((BK, DIFF_DV), BF16)

    def strip(r, kl, bias_tile=None):
        rows = slice(r * SR, (r + 1) * SR)
        kb = kt_ref[:, 0:kl]
        vb = vx_ref[0:kl, :]
        for a, qz_ref in enumerate((q1_ref, q2_ref)):
            s = _dot(qz_ref[rows, :], kb)
            m_old = m_ref[a, rows, :]
            if bias_tile is None:
                m_new = jnp.maximum(m_old, jnp.max(s, -1, keepdims=True) + c_far)
                pm = jnp.exp2(s - (m_new - c_far))
            else:
                s = s + bias_tile
                m_new = jnp.maximum(m_old, jnp.max(s, -1, keepdims=True))
                pm = jnp.exp2(s - m_new)
            alpha = jnp.exp2(m_old - m_new)
            acc_ref[a, rows, :] = alpha * acc_ref[a, rows, :] + _dot(pm.astype(BF16), vb)
            m_ref[a, rows, :] = m_new

    def key_block(delta):
        for r in range(n_half * strips_per_half):
            ro = (r % strips_per_half) * SR
            rel = None if delta is None else delta - r // strips_per_half
            if rel is None or rel <= -2:
                strip(r, BK)
            elif rel == -1:
                strip(r, BK, bsub_ref[...] if ro == 0 else None)
            elif rel == 0:
                strip(r, ro + SR, bdiag_ref[ro:ro + SR, 0:ro + SR])

    d = j - i * n_half

    @pl.when(d <= -2)
    def _():
        key_block(None)

    for delta in range(-1, n_half):
        @pl.when(d == delta)
        def _(delta=delta):
            key_block(delta)
            if delta == n_half - 1:
                lq1, lk1, lq2, lk2 = (lam_ref[t:t + 1, :] for t in range(4))
                lam = (jnp.exp(jnp.sum(lq1 * lk1, -1, keepdims=True))
                       - jnp.exp(jnp.sum(lq2 * lk2, -1, keepdims=True)) + LAM_INIT)
                o = (acc_ref[0, :, 0:DIFF_DV] / acc_ref[0, :, DIFF_DV:DIFF_DV + 1]
                     - lam * (acc_ref[1, :, 0:DIFF_DV] / acc_ref[1, :, DIFF_DV:DIFF_DV + 1]))
                o = o * lax.rsqrt(jnp.mean(o * o, -1, keepdims=True) + LN_EPS)
                o_ref[...] = (o * (1.0 - LAM_INIT)).astype(o_ref.dtype)


def _diff_attn(proj, k_t, lam_vecs, table_flat):
    s = proj.shape[0]
    BK = min(ATT_BK, s)
    BQ = min(ATT_BQ, s)
    n_half = BQ // BK
    pairs = [(i, j) for i in range(s // BQ) for j in range((i + 1) * n_half)]
    qi = jnp.asarray([a for a, _ in pairs], jnp.int32)
    kj = jnp.asarray([b for _, b in pairs], jnp.int32)
    q_blk = (2 * M_QK + 2 * M_V) // V7X_LANES
    v_blk = q_blk + D_QK // V7X_LANES
    SR = min(ATT_STRIP, BK)
    grid_spec = pltpu.PrefetchScalarGridSpec(
        num_scalar_prefetch=2,
        grid=(DIFF_HEADS, len(pairs)),
        in_specs=[
            pl.BlockSpec((BQ, V7X_LANES), lambda h, p, qi, kj: (qi[p], q_blk + h)),
            pl.BlockSpec((2 * DIFF_DH, BK), lambda h, p, qi, kj: (h, kj[p])),
            pl.BlockSpec((BK, V7X_LANES), lambda h, p, qi, kj: (kj[p], v_blk + h)),
            pl.BlockSpec((4, DIFF_DH), lambda h, p, qi, kj: (0, 0)),
            pl.BlockSpec(memory_space=pltpu.SMEM),
        ],
        out_specs=pl.BlockSpec((BQ, V7X_LANES), lambda h, p, qi, kj: (qi[p], h)),
        scratch_shapes=[
            pltpu.VMEM((BQ, V7X_LANES), BF16),
            pltpu.VMEM((BQ, V7X_LANES), BF16),
            pltpu.VMEM((BK, 2 * DIFF_DV), BF16),
            pltpu.VMEM((2, BQ, 1), F32),
            pltpu.VMEM((2, BQ, 2 * DIFF_DV), F32),
            pltpu.VMEM((BK, BK), F32),
            pltpu.VMEM((SR, BK), F32),
        ],
    )
    return pl.pallas_call(
        _diff_attn_kernel,
        grid_spec=grid_spec,
        out_shape=jax.ShapeDtypeStruct((s, D_VW), BF16),
        compiler_params=_params(("arbitrary", "arbitrary")),
        name="diff_attn",
    )(qi, kj, proj, k_t, proj, lam_vecs, table_flat)


def _out_ln_kernel(hm_ref, hd_ref, x_ref, g0_ref, b0_ref, wo1_ref, wo2_ref, g1_ref, b1_ref, wr_ref,
                   h1_ref, h1p_ref, logit_ref):
    mix = _dot(hm_ref[...], wo1_ref[...]) + _dot(hd_ref[...], wo2_ref[...])
    h0 = _layer_norm(x_ref[...], g0_ref[...], b0_ref[...])
    h1 = _layer_norm(ALPHA * h0 + mix, g1_ref[...], b1_ref[...])
    h1_ref[...] = h1
    _store_row_tiles(h1p_ref, _pack_bf16_pairs(h1))
    logit_ref[...] = _dot_nt(wr_ref[...], h1.astype(BF16))


def _out_ln(hm, hd, x2, g0, b0, wo1, wo2, g1, b1, wr):
    s, d = x2.shape
    tm = min(OUT_TM, s)
    row = lambda i: (i, 0)
    fixed = lambda i: (0, 0)
    return pl.pallas_call(
        _out_ln_kernel,
        grid=(s // tm,),
        in_specs=[
            pl.BlockSpec((tm, M_V), row), pl.BlockSpec((tm, D_VW), row), pl.BlockSpec((tm, d), row),
            pl.BlockSpec((1, d), fixed), pl.BlockSpec((1, d), fixed),
            pl.BlockSpec((M_V, d), fixed), pl.BlockSpec((D_VW, d), fixed),
            pl.BlockSpec((1, d), fixed), pl.BlockSpec((1, d), fixed),
            pl.BlockSpec((N_EXPERTS, d), fixed),
        ],
        out_specs=[pl.BlockSpec((tm, d), row), pl.BlockSpec((tm * ROW_TILE, V7X_LANES), row),
                   pl.BlockSpec((N_EXPERTS, tm), lambda i: (0, i))],
        out_shape=[jax.ShapeDtypeStruct((s, d), F32), jax.ShapeDtypeStruct((s * ROW_TILE, V7X_LANES), jnp.uint32),
                   jax.ShapeDtypeStruct((N_EXPERTS, s), F32)],
        compiler_params=_params(("arbitrary",)),
        name="out_ln",
    )(hm, hd, x2, g0, b0, wo1, wo2, g1, b1, wr)


def _route_kernel(logit_ref, rb_ref, topi_ref, pos_ref, topw_ref, cnt_ref, carry_ref):
    n_e, tm = logit_ref.shape
    gsz = n_e // N_GROUPS
    ninf = -jnp.inf

    @pl.when(pl.program_id(0) == 0)
    def _():
        carry_ref[...] = jnp.zeros_like(carry_ref)

    sc = jax.nn.sigmoid(logit_ref[...])
    sel = sc + rb_ref[...]

    sel3 = sel.reshape(N_GROUPS, gsz, tm)
    in_grp = lax.broadcasted_iota(jnp.int32, sel3.shape, 1)
    m1 = jnp.max(sel3, 1, keepdims=True)
    i1 = jnp.min(jnp.where(sel3 == m1, in_grp, gsz), 1, keepdims=True)
    m2 = jnp.max(jnp.where(in_grp == i1, ninf, sel3), 1, keepdims=True)
    gscore = (m1 + m2).reshape(N_GROUPS, tm)
    gid = lax.broadcasted_iota(jnp.int32, gscore.shape, 0)
    beaten = jnp.zeros(gscore.shape, F32)
    for o in range(1, N_GROUPS):
        other = pltpu.roll(gscore, o, 0)
        wins = (other > gscore) | ((other == gscore) & (gid >= o))
        beaten = beaten + jnp.where(wins, 1.0, 0.0)
    keep = jnp.where(beaten < TOPK_GROUPS, 1.0, 0.0)
    keep3 = jnp.broadcast_to(keep.reshape(N_GROUPS, 1, tm), sel3.shape)
    masked = jnp.where(keep3 > 0.5, sel3, ninf).reshape(n_e, tm)

    eid = lax.broadcasted_iota(jnp.int32, (n_e, tm), 0)
    onehots, idxs, ws = [], [], []
    for _ in range(TOP_K):
        m = jnp.max(masked, 0, keepdims=True)
        idx = jnp.min(jnp.where(masked == m, eid, n_e), 0, keepdims=True)
        oh = eid == idx
        onehots.append(oh)
        idxs.append(idx)
        ws.append(jnp.sum(jnp.where(oh, sc, 0.0), 0, keepdims=True))
        masked = jnp.where(oh, ninf, masked)
    wsum = ws[0]
    for w in ws[1:]:
        wsum = wsum + w

    chosen = jnp.zeros((n_e, tm), F32)
    for oh in onehots:
        chosen = jnp.where(oh, 1.0, chosen)
    r_i = lax.broadcasted_iota(jnp.int32, (tm, tm), 0)
    c_i = lax.broadcasted_iota(jnp.int32, (tm, tm), 1)
    before = jnp.where(r_i < c_i, 1.0, 0.0).astype(BF16)
    rank = _dot(chosen.astype(BF16), before) + carry_ref[...]
    carry_new = carry_ref[...] + jnp.sum(chosen, 1, keepdims=True)
    carry_ref[...] = carry_new
    cnt_ref[...] = carry_new

    kid = lax.broadcasted_iota(jnp.int32, (TOP_K, tm), 0)
    topi = jnp.zeros((TOP_K, tm), jnp.int32)
    pos = jnp.zeros((TOP_K, tm), jnp.int32)
    topw = jnp.zeros((TOP_K, tm), F32)
    for k in range(TOP_K):
        pk = jnp.sum(jnp.where(onehots[k], rank, 0.0), 0, keepdims=True)
        topi = jnp.where(kid == k, idxs[k], topi)
        pos = jnp.where(kid == k, pk.astype(jnp.int32), pos)
        topw = jnp.where(kid == k, ws[k] / wsum * ROUTED_SCALE, topw)
    topi_ref[...] = topi
    pos_ref[...] = pos
    topw_ref[...] = topw


def _route(logits_t, router_bias_col):
    n_e, t = logits_t.shape
    tm = min(ROUTE_TM, t)
    col = lambda i: (0, i)
    fixed = lambda i: (0, 0)
    return pl.pallas_call(
        _route_kernel,
        grid=(t // tm,),
        in_specs=[pl.BlockSpec((n_e, tm), col), pl.BlockSpec((n_e, 1), fixed)],
        out_specs=[pl.BlockSpec((TOP_K, tm), col), pl.BlockSpec((TOP_K, tm), col),
                   pl.BlockSpec((TOP_K, tm), col), pl.BlockSpec((n_e, 1), fixed)],
        out_shape=[jax.ShapeDtypeStruct((TOP_K, t), jnp.int32), jax.ShapeDtypeStruct((TOP_K, t), jnp.int32),
                   jax.ShapeDtypeStruct((TOP_K, t), F32), jax.ShapeDtypeStruct((n_e, 1), F32)],
        scratch_shapes=[pltpu.VMEM((n_e, 1), F32)],
        compiler_params=_params(("arbitrary",)),
        name="route",
    )(logits_t, router_bias_col)


def _plan_kernel(cnt_ref, cntc_ref, topi_ref, pos_ref, off_ref, vexp_ref, vblk_ref, vcnt_ref, nvis_ref, dest_ref):
    nv = vexp_ref.shape[0]
    tr = float(MOE_TR)
    cnt = jnp.broadcast_to(cnt_ref[...], (V7X_SUBLANES, N_EXPERTS))
    r_i = lax.broadcasted_iota(jnp.int32, (N_EXPERTS, N_EXPERTS), 0)
    c_i = lax.broadcasted_iota(jnp.int32, (N_EXPERTS, N_EXPERTS), 1)
    upper = jnp.where(r_i <= c_i, 1.0, 0.0).astype(BF16)

    def cumsum_lanes(a):
        a1, a2, a3 = _split3(a)
        return _dot(a1, upper) + _dot(a2, upper) + _dot(a3, upper)

    end = cumsum_lanes(cnt)
    start = end - cnt
    first_blk = jnp.floor(start / tr)
    last_blk = jnp.floor((end - 1.0) / tr)
    nvis_e = jnp.where(cnt > 0.0, last_blk - first_blk + 1.0, 0.0)
    vend = cumsum_lanes(nvis_e)
    vstart = vend - nvis_e
    off_ref[...] = start[0:1, :].astype(jnp.int32)
    nvis_ref[...] = vend[0:1, N_EXPERTS - 1:N_EXPERTS].astype(jnp.int32)

    v = lax.broadcasted_iota(jnp.int32, (nv, N_EXPERTS), 0).astype(F32)
    ve = jnp.broadcast_to(vend[0:1, :], (nv, N_EXPERTS))
    expert = jnp.sum(jnp.where(ve <= v, 1.0, 0.0), -1, keepdims=True)
    expert = jnp.minimum(expert, N_EXPERTS - 1.0)
    lane = lax.broadcasted_iota(jnp.int32, (nv, N_EXPERTS), 1).astype(F32)
    mine = lane == expert
    fb = jnp.sum(jnp.where(mine, jnp.broadcast_to(first_blk[0:1, :], (nv, N_EXPERTS)), 0.0), -1, keepdims=True)
    vs = jnp.sum(jnp.where(mine, jnp.broadcast_to(vstart[0:1, :], (nv, N_EXPERTS)), 0.0), -1, keepdims=True)
    vc = jnp.sum(jnp.where(mine, jnp.broadcast_to(nvis_e[0:1, :], (nv, N_EXPERTS)), 0.0), -1, keepdims=True)
    vexp_ref[...] = expert.astype(jnp.int32)
    vblk_ref[...] = (fb + (v[:, 0:1] - vs)).astype(jnp.int32)
    vcnt_ref[...] = vc.astype(jnp.int32)

    lower = jnp.where(c_i < r_i, 1.0, 0.0).astype(BF16)
    c1, c2, c3 = _split3(jnp.broadcast_to(cntc_ref[...], (N_EXPERTS, V7X_LANES)))
    start_col = (_dot(lower, c1) + _dot(lower, c2) + _dot(lower, c3))[:, 0:1]
    n_tok = topi_ref.shape[1]
    tb = min(PLAN_TB, n_tok)
    eid = lax.broadcasted_iota(jnp.int32, (N_EXPERTS, tb), 0)
    kid = lax.broadcasted_iota(jnp.int32, (TOP_K, tb), 0)

    def dest_block(i, carry):
        cols = pl.ds(pl.multiple_of(i * tb, tb), tb)
        ti = topi_ref[:, cols]
        first_row = jnp.zeros((TOP_K, tb), F32)
        for k in range(TOP_K):
            fr = jnp.sum(jnp.where(eid == ti[k:k + 1, :], start_col, 0.0), 0, keepdims=True)
            first_row = jnp.where(kid == k, fr, first_row)
        dest_ref[:, cols] = first_row.astype(jnp.int32) + pos_ref[:, cols]
        return carry

    lax.fori_loop(0, n_tok // tb, dest_block, 0)


def _plan(counts_row, counts_col, topi_t, pos_t):
    n_rows = topi_t.shape[0] * topi_t.shape[1]
    nv = n_rows // MOE_TR + N_EXPERTS
    return pl.pallas_call(
        _plan_kernel,
        out_shape=[jax.ShapeDtypeStruct((1, N_EXPERTS), jnp.int32), jax.ShapeDtypeStruct((nv, 1), jnp.int32),
                   jax.ShapeDtypeStruct((nv, 1), jnp.int32), jax.ShapeDtypeStruct((nv, 1), jnp.int32),
                   jax.ShapeDtypeStruct((1, 1), jnp.int32), jax.ShapeDtypeStruct(topi_t.shape, jnp.int32)],
        compiler_params=pltpu.CompilerParams(vmem_limit_bytes=V7X_VMEM_LIMIT_BYTES),
        name="plan",
    )(counts_row, counts_col, topi_t, pos_t)


def _row_tile(ref, r):
    return ref.at[pl.ds(pl.multiple_of(r * ROW_TILE, ROW_TILE), ROW_TILE)]


def _dispatch_kernel(dest_ref, x_ref, xs_ref, sem):
    tm = x_ref.shape[0] // ROW_TILE

    def issue(t, carry):
        for k in range(TOP_K):
            d = dest_ref[t * TOP_K + k]
            pltpu.make_async_copy(_row_tile(x_ref, t), _row_tile(xs_ref, d), sem).start(priority=k % 2)
        return carry

    lax.fori_loop(0, tm, issue, 0)
    for _ in range(TOP_K):
        pltpu.make_async_copy(x_ref, xs_ref.at[pl.ds(0, tm * ROW_TILE)], sem).wait()


def _dispatch(dest_flat, rows):
    t = rows.shape[0] // ROW_TILE
    tm = min(MOVE_TM, t)
    return pl.pallas_call(
        _dispatch_kernel,
        grid=(t // tm,),
        in_specs=[
            pl.BlockSpec((tm * TOP_K,), lambda i: (i,), memory_space=pltpu.SMEM),
            pl.BlockSpec((tm * ROW_TILE, V7X_LANES), lambda i: (i, 0)),
        ],
        out_specs=pl.BlockSpec(memory_space=pl.ANY),
        out_shape=jax.ShapeDtypeStruct((t * TOP_K * ROW_TILE, V7X_LANES), rows.dtype),
        scratch_shapes=[pltpu.SemaphoreType.DMA(())],
        compiler_params=_params(("arbitrary",)),
        name="dispatch",
    )(dest_flat, rows)


def _experts_kernel(n_shared_steps, vexp_ref, vblk_ref, vcnt_ref, off_ref, nvis_ref, xs_ref, wg_hbm, wu_hbm,
                    wd_hbm, hp_ref, sg_ref, su_ref, sd_ref, ys_ref, ysh_ref,
                    wg_buf, wu_buf, wd_buf, acc_ref, slot_ref, sem):
    v = pl.program_id(0)
    tr = xs_ref.shape[0] // ROW_TILE
    nvis = nvis_ref[0]

    @pl.when(v < n_shared_steps)
    def _():
        hb = _unpack_bf16_pairs(_load_row_tiles(hp_ref, hp_ref.shape[0] // ROW_TILE, ROW_TILE)).astype(BF16)
        gate = _dot(hb, sg_ref[...])
        up = _dot(hb, su_ref[...])
        ysh_ref[...] = _dot((gate * jax.nn.sigmoid(gate) * up).astype(BF16), sd_ref[...])

    half = wd_buf.shape[1] // 2

    def weight_copies(e, slot):
        return ((pltpu.make_async_copy(wg_hbm.at[e], wg_buf.at[slot], sem.at[slot, 0]), 0),
                (pltpu.make_async_copy(wu_hbm.at[e], wu_buf.at[slot], sem.at[slot, 1]), 1),
                (pltpu.make_async_copy(wd_hbm.at[e, pl.ds(0, half)], wd_buf.at[slot, pl.ds(0, half)],
                                       sem.at[slot, 2]), 0),
                (pltpu.make_async_copy(wd_hbm.at[e, pl.ds(half, half)], wd_buf.at[slot, pl.ds(half, half)],
                                       sem.at[slot, 3]), 1))

    @pl.when(v < nvis)
    def _():
        e = vexp_ref[v]
        blk = vblk_ref[v]
        first_of_expert = jnp.logical_or(v == 0, vexp_ref[jnp.maximum(v - 1, 0)] != e)

        @pl.when(v == 0)
        def _():
            slot_ref[0] = 0
            for c, queue in weight_copies(e, 0):
                c.start(priority=queue)

        @pl.when(jnp.logical_and(first_of_expert, v > 0))
        def _():
            slot_ref[0] = 1 - slot_ref[0]

        slot = slot_ref[0]

        @pl.when(first_of_expert)
        def _():
            for c, _ in weight_copies(e, slot):
                c.wait()
            nxt = v + vcnt_ref[v]

            @pl.when(nxt < nvis)
            def _():
                for c, queue in weight_copies(vexp_ref[jnp.minimum(nxt, vexp_ref.shape[0] - 1)], 1 - slot):
                    c.start(priority=queue)

        lo = off_ref[e] - blk * tr
        hi = off_ref[e + 1] - blk * tr
        first_of_block = jnp.logical_or(v == 0, vblk_ref[jnp.maximum(v - 1, 0)] != blk)

        @pl.when(first_of_block)
        def _():
            acc_ref[...] = jnp.zeros_like(acc_ref)

        def swiglu_rows(r0, n):
            row = r0 + lax.broadcasted_iota(jnp.int32, (n, 1), 0)
            mine = (row >= lo) & (row < hi)
            xb = _unpack_bf16_pairs(_load_row_tiles(xs_ref, n, ROW_TILE, r0)).astype(BF16)
            gate = _dot(xb, wg_buf[slot].astype(BF16))
            up = _dot(xb, wu_buf[slot].astype(BF16))
            act = (gate * jax.nn.sigmoid(gate) * up).astype(BF16)
            acc_ref[r0:r0 + n, :] += jnp.where(mine, _dot(act, wd_buf[slot].astype(BF16)), 0.0)

        hr = tr // 2
        top_only = hi <= hr
        bottom_only = lo >= hr

        @pl.when(top_only)
        def _():
            swiglu_rows(0, hr)

        @pl.when(bottom_only)
        def _():
            swiglu_rows(hr, hr)

        @pl.when(jnp.logical_not(jnp.logical_or(top_only, bottom_only)))
        def _():
            swiglu_rows(0, tr)

        _store_row_tiles(ys_ref, _pack_bf16_pairs(acc_ref[...]))


def _experts(vexp, vblk, vcnt, off_ext, nvis, xs, w_gate, w_up, w_down, h1p, sg, su, sd):
    n_rows = xs.shape[0] // ROW_TILE
    d = D_MODEL
    nv = vexp.shape[0]
    t = h1p.shape[0] // ROW_TILE
    tm = min(SHARED_TM, t)
    n_shared_steps = t // tm
    assert n_shared_steps <= n_rows // MOE_TR

    def block_of(v, ve, vb, vc, off, nvis):
        return (vb[jnp.minimum(v, nvis[0] - 1)], 0)

    def shared_block(v, ve, vb, vc, off, nvis):
        return (jnp.minimum(v, n_shared_steps - 1), 0)

    fixed = lambda v, ve, vb, vc, off, nvis: (0, 0)
    resident = dict(pipeline_mode=pl.Buffered(1))
    grid_spec = pltpu.PrefetchScalarGridSpec(
        num_scalar_prefetch=5,
        grid=(nv,),
        in_specs=[
            pl.BlockSpec((MOE_TR * ROW_TILE, V7X_LANES), block_of),
            pl.BlockSpec(memory_space=pl.ANY),
            pl.BlockSpec(memory_space=pl.ANY),
            pl.BlockSpec(memory_space=pl.ANY),
            pl.BlockSpec((tm * ROW_TILE, V7X_LANES), shared_block),
            pl.BlockSpec((d, D_EXPERT), fixed, **resident),
            pl.BlockSpec((d, D_EXPERT), fixed, **resident),
            pl.BlockSpec((D_EXPERT, d), fixed, **resident),
        ],
        out_specs=[pl.BlockSpec((MOE_TR * ROW_TILE, V7X_LANES), block_of), pl.BlockSpec((tm, d), shared_block)],
        scratch_shapes=[
            pltpu.VMEM((2, d, D_EXPERT), F32),
            pltpu.VMEM((2, d, D_EXPERT), F32),
            pltpu.VMEM((2, D_EXPERT, d), F32),
            pltpu.VMEM((MOE_TR, d), F32),
            pltpu.SMEM((1,), jnp.int32),
            pltpu.SemaphoreType.DMA((2, 4)),
        ],
    )
    return pl.pallas_call(
        functools.partial(_experts_kernel, n_shared_steps),
        grid_spec=grid_spec,
        out_shape=[jax.ShapeDtypeStruct(xs.shape, jnp.uint32), jax.ShapeDtypeStruct((t, d), F32)],
        compiler_params=_params(("arbitrary",)),
        name="experts",
    )(vexp, vblk, vcnt, off_ext, nvis, xs, w_gate, w_up, w_down, h1p, sg, su, sd)


def _combine_kernel(dest_ref, topw_ref, h1_ref, ysh_ref, ys_ref, g2_ref, b2_ref, o_ref, buf_ref, sem):
    tm = h1_ref.shape[0]

    def issue(t, carry):
        for k in range(TOP_K):
            d = dest_ref[t * TOP_K + k]
            pltpu.make_async_copy(_row_tile(ys_ref, d), _row_tile(buf_ref.at[k], t), sem).start(priority=k % 2)
        return carry

    lax.fori_loop(0, tm, issue, 0)

    y = ALPHA * h1_ref[...] + ysh_ref[...]
    for k in range(TOP_K):
        pltpu.make_async_copy(ys_ref.at[pl.ds(0, tm * ROW_TILE)], buf_ref.at[k], sem).wait()
    w = topw_ref[...]
    for k in range(TOP_K):
        y = y + _unpack_bf16_pairs(_load_row_tiles(buf_ref.at[k], tm, ROW_TILE)) * w[:, k:k + 1]
    o_ref[...] = _layer_norm(y, g2_ref[...], b2_ref[...])


def _combine(dest_flat, topw, h1, ysh, ys, g2, b2):
    t, d = h1.shape
    tm = min(MOVE_TM, t)
    row = lambda i: (i, 0)
    fixed = lambda i: (0, 0)
    return pl.pallas_call(
        _combine_kernel,
        grid=(t // tm,),
        in_specs=[
            pl.BlockSpec((tm * TOP_K,), lambda i: (i,), memory_space=pltpu.SMEM),
            pl.BlockSpec((tm, TOP_K), row),
            pl.BlockSpec((tm, d), row),
            pl.BlockSpec((tm, d), row),
            pl.BlockSpec(memory_space=pl.ANY),
            pl.BlockSpec((1, d), fixed), pl.BlockSpec((1, d), fixed),
        ],
        out_specs=pl.BlockSpec((tm, d), row),
        out_shape=jax.ShapeDtypeStruct((t, d), F32),
        scratch_shapes=[pltpu.VMEM((TOP_K, tm * ROW_TILE, V7X_LANES), ys.dtype), pltpu.SemaphoreType.DMA(())],
        compiler_params=_params(("arbitrary",)),
        name="combine",
    )(dest_flat, topw, h1, ysh, ys, g2, b2)


def kernel(x, ln_emb_g, ln_emb_b, w_in, conv_w, conv_b, b_igate, b_fgate, mlstm_norm_g, lambda_q1, lambda_k1,
           lambda_q2, lambda_k2, rel_bias, w_out, ln1_g, ln1_b, w_router, router_bias, w_gate, w_up, w_down,
           ws_gate, ws_up, ws_down, ln2_g, ln2_b):
    bsz, s, d = x.shape
    assert bsz == 1 and w_in.shape[0] == DEPTH == 1
    x2 = x.reshape(s, d)
    row = lambda a: a.reshape(1, -1).astype(F32)

    w = w_in[0]
    c0 = 2 * M_QK + 2 * M_V
    c1 = c0 + 2 * MLSTM_HEADS
    c2 = c1 + D_QK
    c3 = c2 + D_QK
    w_main = jnp.concatenate([w[:, :c0], w[:, c1:c2], w[:, c3:]], axis=1).astype(BF16)
    w_k_t = w[:, c2:c3].T.astype(BF16)
    wi, wf = w[:, c0:c0 + MLSTM_HEADS], w[:, c0 + MLSTM_HEADS:c1]
    w_gates = jnp.zeros((d, MLSTM_HEADS, V7X_LANES), F32).at[:, :, 0].set(wi).at[:, :, 1].set(wf)
    w_gates = w_gates.reshape(d, MLSTM_HEADS * V7X_LANES).astype(BF16)
    w_gates_t = jnp.zeros((MLSTM_HEADS, V7X_SUBLANES, d), F32).at[:, 0].set(wi.T).at[:, 1].set(wf.T)
    w_gates_t = w_gates_t.reshape(MLSTM_HEADS * V7X_SUBLANES, d).astype(BF16)

    proj, gates, gates_t, k_t = _ln_proj(x2, row(ln_emb_g), row(ln_emb_b), w_main, w_gates, w_gates_t, w_k_t)

    hm = _mlstm(proj, gates, gates_t, conv_w[0], row(conv_b[0]), b_igate[0].astype(F32),
                b_fgate[0].astype(F32), row(mlstm_norm_g[0]))

    lam_vecs = jnp.stack([lambda_q1[0], lambda_k1[0], lambda_q2[0], lambda_k2[0]]).astype(F32)
    hd = _diff_attn(proj, k_t, lam_vecs, rel_bias.astype(F32).reshape(-1))

    wo = w_out[0].astype(BF16)
    h1, h1p, logits_t = _out_ln(hm, hd, x2, row(ln_emb_g), row(ln_emb_b), wo[:M_V], wo[M_V:], row(ln1_g[0]),
                           row(ln1_b[0]), w_router[0].T.astype(BF16))

    topi_t, pos_t, topw_t, counts = _route(logits_t, router_bias[0].reshape(-1, 1).astype(F32))
    off, vexp, vblk, vcnt, nvis, dest_t = _plan(counts.reshape(1, -1), counts, topi_t, pos_t)
    off_ext = jnp.concatenate([off.reshape(-1), jnp.full((1,), s * TOP_K, jnp.int32)])
    dest_flat = dest_t.T.reshape(-1)

    xs = _dispatch(dest_flat, h1p)
    ys, ysh = _experts(vexp.reshape(-1), vblk.reshape(-1), vcnt.reshape(-1), off_ext, nvis.reshape(-1), xs,
                       w_gate[0], w_up[0], w_down[0], h1p, ws_gate[0].astype(BF16), ws_up[0].astype(BF16),
                       ws_down[0].astype(BF16))
    out = _combine(dest_flat, topw_t.T, h1, ysh, ys, row(ln2_g[0]), row(ln2_b[0]))
    return out.reshape(bsz, s, d)
```

```python
import functools
import math

import jax
import jax.numpy as jnp
from jax import lax
from jax.experimental import pallas as pl
from jax.experimental.pallas import tpu as pltpu

F32 = jnp.float32
BF16 = jnp.bfloat16

DEPTH = 1
MLSTM_HEADS = 4
MLSTM_DQK = 128
MLSTM_DV = 256
CONV_WIDTH = 4
DIFF_HEADS = 8
DIFF_DH = 64
DIFF_DV = 2 * DIFF_DH
REL_BUCKETS = 32
REL_MAX_DIST = 128
N_EXPERTS = 256
TOP_K = 8
N_GROUPS = 8
TOPK_GROUPS = 4
D_EXPERT = 512
ROUTED_SCALE = 2.5
LN_EPS = 1e-5
ALPHA = (2 * DEPTH) ** 0.25
LAM_INIT = 0.8 - 0.6 * math.exp(-0.3 * 0)

M_QK = MLSTM_HEADS * MLSTM_DQK
M_V = MLSTM_HEADS * MLSTM_DV
D_QK = DIFF_HEADS * 2 * DIFF_DH
D_VW = DIFF_HEADS * DIFF_DV
PROJ_W = 2 * M_QK + 2 * M_V + D_QK + D_VW
D_MODEL = M_V + D_VW

V7X_LANES = 128
V7X_SUBLANES = 8
V7X_VMEM_LIMIT_BYTES = 56 * 1024 * 1024
ROW_TILE = D_MODEL // 2 // V7X_LANES

PROJ_TM = 1024
PROJ_TN = 512
MLSTM_L = 256
ATT_BK = 1024
ATT_BQ = 2048
ATT_STRIP = 256
OUT_TM = 512
ROUTE_TM = 256
MOE_TR = 256
MOVE_TM = 256
SHARED_TM = 256
PLAN_TB = 512

NEG = -1e30
LOG2E = 1.4426950408889634


def _params(semantics):
    return pltpu.CompilerParams(dimension_semantics=semantics, vmem_limit_bytes=V7X_VMEM_LIMIT_BYTES)


def _layer_norm(x, g, b):
    mu = jnp.mean(x, -1, keepdims=True)
    xc = x - mu
    var = jnp.mean(xc * xc, -1, keepdims=True)
    return xc * lax.rsqrt(var + LN_EPS) * g + b


def _dot(a, b):
    return jnp.dot(a, b, preferred_element_type=F32)


def _dot_nt(a, b):
    return lax.dot_general(a, b, (((1,), (1,)), ((), ())), preferred_element_type=F32)


def _split3(a):
    a1 = a.astype(BF16)
    r1 = a - a1.astype(F32)
    a2 = r1.astype(BF16)
    a3 = (r1 - a2.astype(F32)).astype(BF16)
    return a1, a2, a3


def _pack_bf16_pairs(x):
    n = x.shape[1] // 2
    bits = lax.bitcast_convert_type(x.astype(BF16).astype(F32), jnp.uint32)
    return lax.shift_right_logical(bits[:, :n], jnp.uint32(16)) | (bits[:, n:] & jnp.uint32(0xFFFF0000))


def _unpack_bf16_pairs(u):
    lo = lax.bitcast_convert_type(lax.shift_left(u, jnp.uint32(16)), F32)
    hi = lax.bitcast_convert_type(u & jnp.uint32(0xFFFF0000), F32)
    return jnp.concatenate([lo, hi], axis=1)


def _store_row_tiles(ref, x, r0=0):
    m, n = x.shape[0], x.shape[1] // V7X_LANES
    for c in range(n):
        ref[pl.ds(r0 * n + c, m, stride=n), :] = x[:, c * V7X_LANES:(c + 1) * V7X_LANES]


def _load_row_tiles(ref, m, n, r0=0):
    return jnp.concatenate([ref[pl.ds(r0 * n + c, m, stride=n), :] for c in range(n)], axis=1)


def _log_sigmoid(x):
    return jnp.minimum(x, 0.0) - jnp.log(1.0 + jnp.exp(-jnp.abs(x)))


def _ln_proj_kernel(x_ref, g_ref, b_ref, w_ref, wg_ref, wgt_ref, wkt_ref, o_ref, gates_ref, gatest_ref, kt_ref,
                    hb_ref):
    @pl.when(pl.program_id(1) == 0)
    def _():
        hb = _layer_norm(x_ref[...], g_ref[...], b_ref[...]).astype(BF16)
        hb_ref[...] = hb
        gates_ref[...] = _dot(hb, wg_ref[...])
        gatest_ref[...] = _dot_nt(wgt_ref[...], hb)
        kt_ref[...] = _dot_nt(wkt_ref[...], hb).astype(kt_ref.dtype)

    o_ref[...] = _dot(hb_ref[...], w_ref[...]).astype(o_ref.dtype)


def _ln_proj(x2, g, b, w_main, w_gates, w_gates_t, w_k_t):
    s, d = x2.shape
    tm, tn = min(PROJ_TM, s), PROJ_TN
    ng = w_gates.shape[1]
    fixed = lambda i, j: (0, 0)
    return pl.pallas_call(
        _ln_proj_kernel,
        grid=(s // tm, PROJ_W // tn),
        in_specs=[
            pl.BlockSpec((tm, d), lambda i, j: (i, 0)),
            pl.BlockSpec((1, d), fixed),
            pl.BlockSpec((1, d), fixed),
            pl.BlockSpec((d, tn), lambda i, j: (0, j)),
            pl.BlockSpec((d, ng), fixed),
            pl.BlockSpec((w_gates_t.shape[0], d), fixed),
            pl.BlockSpec((D_QK, d), fixed),
        ],
        out_specs=[
            pl.BlockSpec((tm, tn), lambda i, j: (i, j)),
            pl.BlockSpec((tm, ng), lambda i, j: (i, 0)),
            pl.BlockSpec((w_gates_t.shape[0], tm), lambda i, j: (0, i)),
            pl.BlockSpec((D_QK, tm), lambda i, j: (0, i)),
        ],
        out_shape=[
            jax.ShapeDtypeStruct((s, PROJ_W), BF16),
            jax.ShapeDtypeStruct((s, ng), F32),
            jax.ShapeDtypeStruct((w_gates_t.shape[0], s), F32),
            jax.ShapeDtypeStruct((D_QK, s), BF16),
        ],
        scratch_shapes=[pltpu.VMEM((tm, d), BF16)],
        compiler_params=_params(("arbitrary", "arbitrary")),
        name="ln_proj",
    )(x2, g, b, w_main, w_gates, w_gates_t, w_k_t)


def _mlstm_kernel(bi_ref, bf_ref, mq_ref, mk_ref, mv_ref, mo_ref, gates_ref, gatest_ref,
                  cw_ref, cb_ref, ng_ref, o_ref, c_ref, n_ref, m_ref, tail_ref):
    L = mq_ref.shape[0]
    dk, dv = MLSTM_DQK, MLSTM_DV

    @pl.when(pl.program_id(0) == 0)
    def _():
        c_ref[...] = jnp.zeros_like(c_ref)
        n_ref[...] = jnp.zeros_like(n_ref)
        m_ref[...] = jnp.zeros_like(m_ref)
        tail_ref[...] = jnp.zeros_like(tail_ref)

    u = jnp.concatenate([mq_ref[...], mk_ref[...]], axis=1).astype(F32)
    tail = tail_ref[...]
    row8 = lax.broadcasted_iota(jnp.int32, (V7X_SUBLANES, u.shape[1]), 0)
    conv = cb_ref[...] + cw_ref[CONV_WIDTH - 1:CONV_WIDTH, :] * u
    for back in range(1, CONV_WIDTH):
        ur = pltpu.roll(u, back, 0)
        head = jnp.where(row8 < back, pltpu.roll(tail, back, 0), ur[:V7X_SUBLANES])
        shifted = jnp.concatenate([head, ur[V7X_SUBLANES:]], axis=0)
        conv = conv + cw_ref[CONV_WIDTH - 1 - back:CONV_WIDTH - back, :] * shifted
    tail_ref[...] = u[L - V7X_SUBLANES:]
    qk = conv * jax.nn.sigmoid(conv)
    q_all = qk[:, :M_QK]
    k_all = qk[:, M_QK:] * (dk ** -0.5)

    r_i = lax.broadcasted_iota(jnp.int32, (L, L), 0)
    c_i = lax.broadcasted_iota(jnp.int32, (L, L), 1)
    causal = c_i <= r_i
    tril = jnp.where(causal, 1.0, 0.0).astype(BF16)
    triu = jnp.where(r_i <= c_i, 1.0, 0.0).astype(BF16)

    for h in range(MLSTM_HEADS):
        q = q_all[:, h * dk:(h + 1) * dk]
        k = k_all[:, h * dk:(h + 1) * dk]
        qb, kb = q.astype(BF16), k.astype(BF16)
        vb = mv_ref[:, h * dv:(h + 1) * dv]
        b_i, b_f = bi_ref[h], bf_ref[h]

        gblk = gates_ref[:, h * V7X_LANES:(h + 1) * V7X_LANES]
        i_col = gblk[:, 0:1] + b_i
        lf_blk = _log_sigmoid(gblk + b_f)
        f1, f2, f3 = _split3(lf_blk)
        bcum_blk = _dot(tril, f1) + _dot(tril, f2) + _dot(tril, f3)
        b_col = bcum_blk[:, 1:2]
        gt = gatest_ref[h * V7X_SUBLANES:(h + 1) * V7X_SUBLANES, :]
        i_row = gt[0:1, :] + b_i
        lf_rows = _log_sigmoid(gt + b_f)
        g1, g2, g3 = _split3(lf_rows)
        b_row = (_dot(g1, triu) + _dot(g2, triu) + _dot(g3, triu))[1:2, :]

        m_prev = m_ref[h, 0:1, 0:1]
        dmat = jnp.where(causal, b_col - b_row + i_row, NEG)
        inter = b_col + m_prev
        m_t = jnp.maximum(inter, jnp.max(dmat, -1, keepdims=True))
        wts = jnp.exp(dmat - m_t)
        g = jnp.exp(inter - m_t)
        sqk = _dot_nt(qb, kb) * wts
        c_prev = c_ref[h]
        n_prev = n_ref[h, 0:1, :]
        num = g * _dot(qb, c_prev.astype(BF16)) + _dot(sqk.astype(BF16), vb)
        den = g * jnp.sum(q * n_prev, -1, keepdims=True) + jnp.sum(sqk, -1, keepdims=True)
        hh = num / jnp.maximum(jnp.abs(den), jnp.exp(-m_t))

        b_last = b_col[L - 1:L, :]
        w_last_row = b_last - b_row + i_row
        m_new = jnp.maximum(b_last + m_prev, jnp.max(w_last_row, -1, keepdims=True))
        decay = jnp.exp(b_last + m_prev - m_new)
        ws_col = jnp.exp(b_last - b_col + i_col - m_new)
        kw = k * ws_col
        c_ref[h] = decay * c_prev + _dot(kw.T.astype(BF16), vb)
        n_ref[h, 0:1, :] = decay * n_prev + jnp.sum(kw, 0, keepdims=True)
        m_ref[h] = jnp.broadcast_to(m_new, m_ref.shape[1:])

        mu = jnp.mean(hh, -1, keepdims=True)
        hc = hh - mu
        var = jnp.mean(hc * hc, -1, keepdims=True)
        hn = hc * lax.rsqrt(var + LN_EPS) * ng_ref[:, h * dv:(h + 1) * dv]
        og = jax.nn.sigmoid(mo_ref[:, h * dv:(h + 1) * dv].astype(F32))
        o_ref[:, h * dv:(h + 1) * dv] = (hn * og).astype(o_ref.dtype)


def _mlstm(proj, gates, gates_t, conv_w, conv_b, b_i, b_f, norm_g):
    s = proj.shape[0]
    L = min(MLSTM_L, s)
    smem = pl.BlockSpec(memory_space=pltpu.SMEM)
    return pl.pallas_call(
        _mlstm_kernel,
        grid=(s // L,),
        in_specs=[
            smem, smem,
            pl.BlockSpec((L, M_QK), lambda c: (c, 0)),
            pl.BlockSpec((L, M_QK), lambda c: (c, 1)),
            pl.BlockSpec((L, M_V), lambda c: (c, 1)),
            pl.BlockSpec((L, M_V), lambda c: (c, 2)),
            pl.BlockSpec((L, gates.shape[1]), lambda c: (c, 0)),
            pl.BlockSpec((gates_t.shape[0], L), lambda c: (0, c)),
            pl.BlockSpec((CONV_WIDTH, 2 * M_QK), lambda c: (0, 0)),
            pl.BlockSpec((1, 2 * M_QK), lambda c: (0, 0)),
            pl.BlockSpec((1, M_V), lambda c: (0, 0)),
        ],
        out_specs=pl.BlockSpec((L, M_V), lambda c: (c, 0)),
        out_shape=jax.ShapeDtypeStruct((s, M_V), BF16),
        scratch_shapes=[
            pltpu.VMEM((MLSTM_HEADS, MLSTM_DQK, MLSTM_DV), F32),
            pltpu.VMEM((MLSTM_HEADS, V7X_SUBLANES, MLSTM_DQK), F32),
            pltpu.VMEM((MLSTM_HEADS, V7X_SUBLANES, V7X_LANES), F32),
            pltpu.VMEM((V7X_SUBLANES, 2 * M_QK), F32),
        ],
        compiler_params=_params(("arbitrary",)),
        name="mlstm",
    )(b_i, b_f, proj, proj, proj, proj, gates, gates_t, conv_w, conv_b, norm_g)


def _t5_bias_tile(table_ref, h, offset):
    n_t = V7X_LANES
    r_i = lax.broadcasted_iota(jnp.int32, (n_t, n_t), 0)
    c_i = lax.broadcasted_iota(jnp.int32, (n_t, n_t), 1)
    n = jnp.maximum(offset + r_i - c_i, 0)
    max_exact = REL_BUCKETS // 2
    large = max_exact + (jnp.log(jnp.maximum(n, 1).astype(F32) / max_exact)
                         / math.log(REL_MAX_DIST / max_exact) * (REL_BUCKETS - max_exact)).astype(jnp.int32)
    large = jnp.minimum(large, REL_BUCKETS - 1)
    bucket = jnp.where(n < max_exact, n, large)
    out = jnp.zeros((n_t, n_t), F32)
    for b in range(REL_BUCKETS):
        out = jnp.where(bucket == b, table_ref[b * DIFF_HEADS + h], out)
    return out * LOG2E


def _diff_attn_kernel(qi_ref, kj_ref, q_ref, kt_ref, v_ref, lam_ref, table_ref, o_ref,
                      q1_ref, q2_ref, vx_ref, m_ref, acc_ref, bdiag_ref, bsub_ref):
    h = pl.program_id(0)
    p = pl.program_id(1)
    i = qi_ref[p]
    j = kj_ref[p]
    BQ = q_ref.shape[0]
    BK = v_ref.shape[0]
    n_half = BQ // BK
    SR = min(ATT_STRIP, BK)
    strips_per_half = BK // SR
    n_sub = BK // V7X_LANES
    c_far = table_ref[(REL_BUCKETS - 1) * DIFF_HEADS + h] * LOG2E

    @pl.when(p == 0)
    def _():
        p0 = _t5_bias_tile(table_ref, h, 0)
        p0 = jnp.where(lax.broadcasted_iota(jnp.int32, p0.shape, 1) <= lax.broadcasted_iota(jnp.int32, p0.shape, 0),
                       p0, NEG)
        p1 = _t5_bias_tile(table_ref, h, V7X_LANES)
        far = jnp.full((V7X_LANES, V7X_LANES), c_far, F32)
        neg = jnp.full((V7X_LANES, V7X_LANES), NEG, F32)
        for a in range(n_sub):
            for b in range(n_sub):
                tile = p0 if a == b else p1 if a == b + 1 else far if a > b else neg
                bdiag_ref[a * V7X_LANES:(a + 1) * V7X_LANES, b * V7X_LANES:(b + 1) * V7X_LANES] = tile
        bsub_ref[...] = jnp.full(bsub_ref.shape, c_far, F32)
        bsub_ref[0:V7X_LANES, BK - V7X_LANES:BK] = p1

    @pl.when(j == 0)
    def _():
        q = q_ref[...].astype(F32) * ((DIFF_DH ** -0.5) * LOG2E)
        lane = lax.broadcasted_iota(jnp.int32, q.shape, 1)
        q1_ref[...] = jnp.where(lane < DIFF_DH, q, 0.0).astype(BF16)
        q2_ref[...] = jnp.where(lane >= DIFF_DH, q, 0.0).astype(BF16)
        m_ref[...] = jnp.full(m_ref.shape, NEG, F32)
        acc_ref[...] = jnp.zeros_like(acc_ref)

    vx_ref[:, 0:DIFF_DV] = v_ref[...]
    vx_ref[:, DIFF_DV:] = jnp.ones((BK, DIFF_DV), BF16)

    def strip(r, kl, bias_tile=None):
        rows = slice(r * SR, (r + 1) * SR)
        kb = kt_ref[:, 0:kl]
        vb = vx_ref[0:kl, :]
        for a, qz_ref in enumerate((q1_ref, q2_ref)):
            s = _dot(qz_ref[rows, :], kb)
            m_old = m_ref[a, rows, :]
            if bias_tile is None:
                m_new = jnp.maximum(m_old, jnp.max(s, -1, keepdims=True) + c_far)
                pm = jnp.exp2(s - (m_new - c_far))
            else:
                s = s + bias_tile
                m_new = jnp.maximum(m_old, jnp.max(s, -1, keepdims=True))
                pm = jnp.exp2(s - m_new)
            alpha = jnp.exp2(m_old - m_new)
            acc_ref[a, rows, :] = alpha * acc_ref[a, rows, :] + _dot(pm.astype(BF16), vb)
            m_ref[a, rows, :] = m_new

    def key_block(delta):
        for r in range(n_half * strips_per_half):
            ro = (r % strips_per_half) * SR
            rel = None if delta is None else delta - r // strips_per_half
            if rel is None or rel <= -2:
                strip(r, BK)
            elif rel == -1:
                strip(r, BK, bsub_ref[...] if ro == 0 else None)
            elif rel == 0:
                strip(r, ro + SR, bdiag_ref[ro:ro + SR, 0:ro + SR])

    d = j - i * n_half

    @pl.when(d <= -2)
    def _():
        key_block(None)

    for delta in range(-1, n_half):
        @pl.when(d == delta)
        def _(delta=delta):
            key_block(delta)
            if delta == n_half - 1:
                lq1, lk1, lq2, lk2 = (lam_ref[t:t + 1, :] for t in range(4))
                lam = (jnp.exp(jnp.sum(lq1 * lk1, -1, keepdims=True))
                       - jnp.exp(jnp.sum(lq2 * lk2, -1, keepdims=True)) + LAM_INIT)
                o = (acc_ref[0, :, 0:DIFF_DV] / acc_ref[0, :, DIFF_DV:DIFF_DV + 1]
                     - lam * (acc_ref[1, :, 0:DIFF_DV] / acc_ref[1, :, DIFF_DV:DIFF_DV + 1]))
                o = o * lax.rsqrt(jnp.mean(o * o, -1, keepdims=True) + LN_EPS)
                o_ref[...] = (o * (1.0 - LAM_INIT)).astype(o_ref.dtype)


def _diff_attn(proj, k_t, lam_vecs, table_flat):
    s = proj.shape[0]
    BK = min(ATT_BK, s)
    BQ = min(ATT_BQ, s)
    n_half = BQ // BK
    pairs = [(i, j) for i in range(s // BQ) for j in range((i + 1) * n_half)]
    qi = jnp.asarray([a for a, _ in pairs], jnp.int32)
    kj = jnp.asarray([b for _, b in pairs], jnp.int32)
    q_blk = (2 * M_QK + 2 * M_V) // V7X_LANES
    v_blk = q_blk + D_QK // V7X_LANES
    SR = min(ATT_STRIP, BK)
    grid_spec = pltpu.PrefetchScalarGridSpec(
        num_scalar_prefetch=2,
        grid=(DIFF_HEADS, len(pairs)),
        in_specs=[
            pl.BlockSpec((BQ, V7X_LANES), lambda h, p, qi, kj: (qi[p], q_blk + h)),
            pl.BlockSpec((2 * DIFF_DH, BK), lambda h, p, qi, kj: (h, kj[p])),
            pl.BlockSpec((BK, V7X_LANES), lambda h, p, qi, kj: (kj[p], v_blk + h)),
            pl.BlockSpec((4, DIFF_DH), lambda h, p, qi, kj: (0, 0)),
            pl.BlockSpec(memory_space=pltpu.SMEM),
        ],
        out_specs=pl.BlockSpec((BQ, V7X_LANES), lambda h, p, qi, kj: (qi[p], h)),
        scratch_shapes=[
            pltpu.VMEM((BQ, V7X_LANES), BF16),
            pltpu.VMEM((BQ, V7X_LANES), BF16),
            pltpu.VMEM((BK, 2 * DIFF_DV), BF16),
            pltpu.VMEM((2, BQ, 1), F32),
            pltpu.VMEM((2, BQ, 2 * DIFF_DV), F32),
            pltpu.VMEM((BK, BK), F32),
            pltpu.VMEM((SR, BK), F32),
        ],
    )
    return pl.pallas_call(
        _diff_attn_kernel,
        grid_spec=grid_spec,
        out_shape=jax.ShapeDtypeStruct((s, D_VW), BF16),
        compiler_params=_params(("arbitrary", "arbitrary")),
        name="diff_attn",
    )(qi, kj, proj, k_t, proj, lam_vecs, table_flat)


def _out_ln_kernel(hm_ref, hd_ref, x_ref, g0_ref, b0_ref, wo1_ref, wo2_ref, g1_ref, b1_ref, wr_ref,
                   h1_ref, h1p_ref, logit_ref):
    mix = _dot(hm_ref[...], wo1_ref[...]) + _dot(hd_ref[...], wo2_ref[...])
    h0 = _layer_norm(x_ref[...], g0_ref[...], b0_ref[...])
    h1 = _layer_norm(ALPHA * h0 + mix, g1_ref[...], b1_ref[...])
    h1_ref[...] = h1
    _store_row_tiles(h1p_ref, _pack_bf16_pairs(h1))
    logit_ref[...] = _dot_nt(wr_ref[...], h1.astype(BF16))


def _out_ln(hm, hd, x2, g0, b0, wo1, wo2, g1, b1, wr):
    s, d = x2.shape
    tm = min(OUT_TM, s)
    row = lambda i: (i, 0)
    fixed = lambda i: (0, 0)
    return pl.pallas_call(
        _out_ln_kernel,
        grid=(s // tm,),
        in_specs=[
            pl.BlockSpec((tm, M_V), row), pl.BlockSpec((tm, D_VW), row), pl.BlockSpec((tm, d), row),
            pl.BlockSpec((1, d), fixed), pl.BlockSpec((1, d), fixed),
            pl.BlockSpec((M_V, d), fixed), pl.BlockSpec((D_VW, d), fixed),
            pl.BlockSpec((1, d), fixed), pl.BlockSpec((1, d), fixed),
            pl.BlockSpec((N_EXPERTS, d), fixed),
        ],
        out_specs=[pl.BlockSpec((tm, d), row), pl.BlockSpec((tm * ROW_TILE, V7X_LANES), row),
                   pl.BlockSpec((N_EXPERTS, tm), lambda i: (0, i))],
        out_shape=[jax.ShapeDtypeStruct((s, d), F32), jax.ShapeDtypeStruct((s * ROW_TILE, V7X_LANES), jnp.uint32),
                   jax.ShapeDtypeStruct((N_EXPERTS, s), F32)],
        compiler_params=_params(("arbitrary",)),
        name="out_ln",
    )(hm, hd, x2, g0, b0, wo1, wo2, g1, b1, wr)


def _route_kernel(logit_ref, rb_ref, topi_ref, pos_ref, topw_ref, cnt_ref, carry_ref):
    n_e, tm = logit_ref.shape
    gsz = n_e // N_GROUPS
    ninf = -jnp.inf

    @pl.when(pl.program_id(0) == 0)
    def _():
        carry_ref[...] = jnp.zeros_like(carry_ref)

    sc = jax.nn.sigmoid(logit_ref[...])
    sel = sc + rb_ref[...]

    sel3 = sel.reshape(N_GROUPS, gsz, tm)
    in_grp = lax.broadcasted_iota(jnp.int32, sel3.shape, 1)
    m1 = jnp.max(sel3, 1, keepdims=True)
    i1 = jnp.min(jnp.where(sel3 == m1, in_grp, gsz), 1, keepdims=True)
    m2 = jnp.max(jnp.where(in_grp == i1, ninf, sel3), 1, keepdims=True)
    gscore = (m1 + m2).reshape(N_GROUPS, tm)
    gid = lax.broadcasted_iota(jnp.int32, gscore.shape, 0)
    beaten = jnp.zeros(gscore.shape, F32)
    for o in range(1, N_GROUPS):
        other = pltpu.roll(gscore, o, 0)
        wins = (other > gscore) | ((other == gscore) & (gid >= o))
        beaten = beaten + jnp.where(wins, 1.0, 0.0)
    keep = jnp.where(beaten < TOPK_GROUPS, 1.0, 0.0)
    keep3 = jnp.broadcast_to(keep.reshape(N_GROUPS, 1, tm), sel3.shape)
    masked = jnp.where(keep3 > 0.5, sel3, ninf).reshape(n_e, tm)

    eid = lax.broadcasted_iota(jnp.int32, (n_e, tm), 0)
    onehots, idxs, ws = [], [], []
    for _ in range(TOP_K):
        m = jnp.max(masked, 0, keepdims=True)
        idx = jnp.min(jnp.where(masked == m, eid, n_e), 0, keepdims=True)
        oh = eid == idx
        onehots.append(oh)
        idxs.append(idx)
        ws.append(jnp.sum(jnp.where(oh, sc, 0.0), 0, keepdims=True))
        masked = jnp.where(oh, ninf, masked)
    wsum = ws[0]
    for w in ws[1:]:
        wsum = wsum + w

    chosen = jnp.zeros((n_e, tm), F32)
    for oh in onehots:
        chosen = jnp.where(oh, 1.0, chosen)
    r_i = lax.broadcasted_iota(jnp.int32, (tm, tm), 0)
    c_i = lax.broadcasted_iota(jnp.int32, (tm, tm), 1)
    before = jnp.where(r_i < c_i, 1.0, 0.0).astype(BF16)
    rank = _dot(chosen.astype(BF16), before) + carry_ref[...]
    carry_new = carry_ref[...] + jnp.sum(chosen, 1, keepdims=True)
    carry_ref[...] = carry_new
    cnt_ref[...] = carry_new

    kid = lax.broadcasted_iota(jnp.int32, (TOP_K, tm), 0)
    topi = jnp.zeros((TOP_K, tm), jnp.int32)
    pos = jnp.zeros((TOP_K, tm), jnp.int32)
    topw = jnp.zeros((TOP_K, tm), F32)
    for k in range(TOP_K):
        pk = jnp.sum(jnp.where(onehots[k], rank, 0.0), 0, keepdims=True)
        topi = jnp.where(kid == k, idxs[k], topi)
        pos = jnp.where(kid == k, pk.astype(jnp.int32), pos)
        topw = jnp.where(kid == k, ws[k] / wsum * ROUTED_SCALE, topw)
    topi_ref[...] = topi
    pos_ref[...] = pos
    topw_ref[...] = topw


def _route(logits_t, router_bias_col):
    n_e, t = logits_t.shape
    tm = min(ROUTE_TM, t)
    col = lambda i: (0, i)
    fixed = lambda i: (0, 0)
    return pl.pallas_call(
        _route_kernel,
        grid=(t // tm,),
        in_specs=[pl.BlockSpec((n_e, tm), col), pl.BlockSpec((n_e, 1), fixed)],
        out_specs=[pl.BlockSpec((TOP_K, tm), col), pl.BlockSpec((TOP_K, tm), col),
                   pl.BlockSpec((TOP_K, tm), col), pl.BlockSpec((n_e, 1), fixed)],
        out_shape=[jax.ShapeDtypeStruct((TOP_K, t), jnp.int32), jax.ShapeDtypeStruct((TOP_K, t), jnp.int32),
                   jax.ShapeDtypeStruct((TOP_K, t), F32), jax.ShapeDtypeStruct((n_e, 1), F32)],
        scratch_shapes=[pltpu.VMEM((n_e, 1), F32)],
        compiler_params=_params(("arbitrary",)),
        name="route",
    )(logits_t, router_bias_col)


def _plan_kernel(cnt_ref, cntc_ref, topi_ref, pos_ref, off_ref, vexp_ref, vblk_ref, vcnt_ref, nvis_ref, dest_ref):
    nv = vexp_ref.shape[0]
    tr = float(MOE_TR)
    cnt = jnp.broadcast_to(cnt_ref[...], (V7X_SUBLANES, N_EXPERTS))
    r_i = lax.broadcasted_iota(jnp.int32, (N_EXPERTS, N_EXPERTS), 0)
    c_i = lax.broadcasted_iota(jnp.int32, (N_EXPERTS, N_EXPERTS), 1)
    upper = jnp.where(r_i <= c_i, 1.0, 0.0).astype(BF16)

    def cumsum_lanes(a):
        a1, a2, a3 = _split3(a)
        return _dot(a1, upper) + _dot(a2, upper) + _dot(a3, upper)

    end = cumsum_lanes(cnt)
    start = end - cnt
    first_blk = jnp.floor(start / tr)
    last_blk = jnp.floor((end - 1.0) / tr)
    nvis_e = jnp.where(cnt > 0.0, last_blk - first_blk + 1.0, 0.0)
    vend = cumsum_lanes(nvis_e)
    vstart = vend - nvis_e
    off_ref[...] = start[0:1, :].astype(jnp.int32)
    nvis_ref[...] = vend[0:1, N_EXPERTS - 1:N_EXPERTS].astype(jnp.int32)

    v = lax.broadcasted_iota(jnp.int32, (nv, N_EXPERTS), 0).astype(F32)
    ve = jnp.broadcast_to(vend[0:1, :], (nv, N_EXPERTS))
    expert = jnp.sum(jnp.where(ve <= v, 1.0, 0.0), -1, keepdims=True)
    expert = jnp.minimum(expert, N_EXPERTS - 1.0)
    lane = lax.broadcasted_iota(jnp.int32, (nv, N_EXPERTS), 1).astype(F32)
    mine = lane == expert
    fb = jnp.sum(jnp.where(mine, jnp.broadcast_to(first_blk[0:1, :], (nv, N_EXPERTS)), 0.0), -1, keepdims=True)
    vs = jnp.sum(jnp.where(mine, jnp.broadcast_to(vstart[0:1, :], (nv, N_EXPERTS)), 0.0), -1, keepdims=True)
    vc = jnp.sum(jnp.where(mine, jnp.broadcast_to(nvis_e[0:1, :], (nv, N_EXPERTS)), 0.0), -1, keepdims=True)
    vexp_ref[...] = expert.astype(jnp.int32)
    vblk_ref[...] = (fb + (v[:, 0:1] - vs)).astype(jnp.int32)
    vcnt_ref[...] = vc.astype(jnp.int32)

    lower = jnp.where(c_i < r_i, 1.0, 0.0).astype(BF16)
    c1, c2, c3 = _split3(jnp.broadcast_to(cntc_ref[...], (N_EXPERTS, V7X_LANES)))
    start_col = (_dot(lower, c1) + _dot(lower, c2) + _dot(lower, c3))[:, 0:1]
    n_tok = topi_ref.shape[1]
    tb = min(PLAN_TB, n_tok)
    eid = lax.broadcasted_iota(jnp.int32, (N_EXPERTS, tb), 0)
    kid = lax.broadcasted_iota(jnp.int32, (TOP_K, tb), 0)

    def dest_block(i, carry):
        cols = pl.ds(pl.multiple_of(i * tb, tb), tb)
        ti = topi_ref[:, cols]
        first_row = jnp.zeros((TOP_K, tb), F32)
        for k in range(TOP_K):
            fr = jnp.sum(jnp.where(eid == ti[k:k + 1, :], start_col, 0.0), 0, keepdims=True)
            first_row = jnp.where(kid == k, fr, first_row)
        dest_ref[:, cols] = first_row.astype(jnp.int32) + pos_ref[:, cols]
        return carry

    lax.fori_loop(0, n_tok // tb, dest_block, 0)


def _plan(counts_row, counts_col, topi_t, pos_t):
    n_rows = topi_t.shape[0] * topi_t.shape[1]
    nv = n_rows // MOE_TR + N_EXPERTS
    return pl.pallas_call(
        _plan_kernel,
        out_shape=[jax.ShapeDtypeStruct((1, N_EXPERTS), jnp.int32), jax.ShapeDtypeStruct((nv, 1), jnp.int32),
                   jax.ShapeDtypeStruct((nv, 1), jnp.int32), jax.ShapeDtypeStruct((nv, 1), jnp.int32),
                   jax.ShapeDtypeStruct((1, 1), jnp.int32), jax.ShapeDtypeStruct(topi_t.shape, jnp.int32)],
        compiler_params=pltpu.CompilerParams(vmem_limit_bytes=V7X_VMEM_LIMIT_BYTES),
        name="plan",
    )(counts_row, counts_col, topi_t, pos_t)


def _row_tile(ref, r):
    return ref.at[pl.ds(pl.multiple_of(r * ROW_TILE, ROW_TILE), ROW_TILE)]


def _dispatch_kernel(dest_ref, rows_ref, xs_ref, sem):
    i = pl.program_id(0)
    tm = dest_ref.shape[0] // TOP_K

    def issue(t, carry):
        src = _row_tile(rows_ref, i * tm + t)
        for k in range(TOP_K):
            d = dest_ref[t * TOP_K + k]
            pltpu.make_async_copy(src, _row_tile(xs_ref, d), sem).start(priority=k % 2)
        return carry

    lax.fori_loop(0, tm, issue, 0)

    def retire_one_step():
        block = pl.ds(0, tm * ROW_TILE)
        for _ in range(TOP_K):
            pltpu.make_async_copy(rows_ref.at[block], xs_ref.at[block], sem).wait()

    @pl.when(i > 0)
    def _():
        retire_one_step()

    @pl.when(i == pl.num_programs(0) - 1)
    def _():
        retire_one_step()


def _dispatch(dest_flat, rows):
    t = rows.shape[0] // ROW_TILE
    tm = min(MOVE_TM, t)
    return pl.pallas_call(
        _dispatch_kernel,
        grid=(t // tm,),
        in_specs=[
            pl.BlockSpec((tm * TOP_K,), lambda i: (i,), memory_space=pltpu.SMEM),
            pl.BlockSpec(memory_space=pl.ANY),
        ],
        out_specs=pl.BlockSpec(memory_space=pl.ANY),
        out_shape=jax.ShapeDtypeStruct((t * TOP_K * ROW_TILE, V7X_LANES), rows.dtype),
        scratch_shapes=[pltpu.SemaphoreType.DMA(())],
        compiler_params=_params(("arbitrary",)),
        name="dispatch",
    )(dest_flat, rows)


def _experts_kernel(n_shared_steps, vexp_ref, vblk_ref, vcnt_ref, off_ref, nvis_ref, xs_ref, wg_hbm, wu_hbm,
                    wd_hbm, hp_ref, sg_ref, su_ref, sd_ref, ys_ref, ysh_ref,
                    wg_buf, wu_buf, wd_buf, acc_ref, slot_ref, sem):
    v = pl.program_id(0)
    tr = xs_ref.shape[0] // ROW_TILE
    nvis = nvis_ref[0]

    @pl.when(v < n_shared_steps)
    def _():
        hb = _unpack_bf16_pairs(_load_row_tiles(hp_ref, hp_ref.shape[0] // ROW_TILE, ROW_TILE)).astype(BF16)
        gate = _dot(hb, sg_ref[...])
        up = _dot(hb, su_ref[...])
        ysh_ref[...] = _dot((gate * jax.nn.sigmoid(gate) * up).astype(BF16), sd_ref[...])

    half = wd_buf.shape[1] // 2

    def weight_copies(e, slot):
        return ((pltpu.make_async_copy(wg_hbm.at[e], wg_buf.at[slot], sem.at[slot, 0]), 0),
                (pltpu.make_async_copy(wu_hbm.at[e], wu_buf.at[slot], sem.at[slot, 1]), 1),
                (pltpu.make_async_copy(wd_hbm.at[e, pl.ds(0, half)], wd_buf.at[slot, pl.ds(0, half)],
                                       sem.at[slot, 2]), 0),
                (pltpu.make_async_copy(wd_hbm.at[e, pl.ds(half, half)], wd_buf.at[slot, pl.ds(half, half)],
                                       sem.at[slot, 3]), 1))

    @pl.when(v < nvis)
    def _():
        e = vexp_ref[v]
        blk = vblk_ref[v]
        first_of_expert = jnp.logical_or(v == 0, vexp_ref[jnp.maximum(v - 1, 0)] != e)

        @pl.when(v == 0)
        def _():
            slot_ref[0] = 0
            for c, queue in weight_copies(e, 0):
                c.start(priority=queue)

        @pl.when(jnp.logical_and(first_of_expert, v > 0))
        def _():
            slot_ref[0] = 1 - slot_ref[0]

        slot = slot_ref[0]

        @pl.when(first_of_expert)
        def _():
            nxt = v + vcnt_ref[v]

            @pl.when(nxt < nvis)
            def _():
                for c, queue in weight_copies(vexp_ref[jnp.minimum(nxt, vexp_ref.shape[0] - 1)], 1 - slot):
                    c.start(priority=queue)

            for c, _ in weight_copies(e, slot):
                c.wait()

        lo = off_ref[e] - blk * tr
        hi = off_ref[e + 1] - blk * tr
        first_of_block = jnp.logical_or(v == 0, vblk_ref[jnp.maximum(v - 1, 0)] != blk)

        @pl.when(first_of_block)
        def _():
            acc_ref[...] = jnp.zeros_like(acc_ref)

        def swiglu_rows(r0, n):
            row = r0 + lax.broadcasted_iota(jnp.int32, (n, 1), 0)
            mine = (row >= lo) & (row < hi)
            xb = _unpack_bf16_pairs(_load_row_tiles(xs_ref, n, ROW_TILE, r0)).astype(BF16)
            gate = _dot(xb, wg_buf[slot].astype(BF16))
            up = _dot(xb, wu_buf[slot].astype(BF16))
            act = (gate * jax.nn.sigmoid(gate) * up).astype(BF16)
            acc_ref[r0:r0 + n, :] += jnp.where(mine, _dot(act, wd_buf[slot].astype(BF16)), 0.0)

        hr = tr // 2
        top_only = hi <= hr
        bottom_only = lo >= hr

        @pl.when(top_only)
        def _():
            swiglu_rows(0, hr)

        @pl.when(bottom_only)
        def _():
            swiglu_rows(hr, hr)

        @pl.when(jnp.logical_not(jnp.logical_or(top_only, bottom_only)))
        def _():
            swiglu_rows(0, tr)

        _store_row_tiles(ys_ref, _pack_bf16_pairs(acc_ref[...]))


def _experts(vexp, vblk, vcnt, off_ext, nvis, xs, w_gate, w_up, w_down, h1p, sg, su, sd):
    n_rows = xs.shape[0] // ROW_TILE
    d = D_MODEL
    nv = vexp.shape[0]
    t = h1p.shape[0] // ROW_TILE
    tm = min(SHARED_TM, t)
    n_shared_steps = t // tm
    assert n_shared_steps <= n_rows // MOE_TR

    def block_of(v, ve, vb, vc, off, nvis):
        return (vb[jnp.minimum(v, nvis[0] - 1)], 0)

    def shared_block(v, ve, vb, vc, off, nvis):
        return (jnp.minimum(v, n_shared_steps - 1), 0)

    fixed = lambda v, ve, vb, vc, off, nvis: (0, 0)
    resident = dict(pipeline_mode=pl.Buffered(1))
    grid_spec = pltpu.PrefetchScalarGridSpec(
        num_scalar_prefetch=5,
        grid=(nv,),
        in_specs=[
            pl.BlockSpec((MOE_TR * ROW_TILE, V7X_LANES), block_of),
            pl.BlockSpec(memory_space=pl.ANY),
            pl.BlockSpec(memory_space=pl.ANY),
            pl.BlockSpec(memory_space=pl.ANY),
            pl.BlockSpec((tm * ROW_TILE, V7X_LANES), shared_block),
            pl.BlockSpec((d, D_EXPERT), fixed, **resident),
            pl.BlockSpec((d, D_EXPERT), fixed, **resident),
            pl.BlockSpec((D_EXPERT, d), fixed, **resident),
        ],
        out_specs=[pl.BlockSpec((MOE_TR * ROW_TILE, V7X_LANES), block_of), pl.BlockSpec((tm, d), shared_block)],
        scratch_shapes=[
            pltpu.VMEM((2, d, D_EXPERT), F32),
            pltpu.VMEM((2, d, D_EXPERT), F32),
            pltpu.VMEM((2, D_EXPERT, d), F32),
            pltpu.VMEM((MOE_TR, d), F32),
            pltpu.SMEM((1,), jnp.int32),
            pltpu.SemaphoreType.DMA((2, 4)),
        ],
    )
    return pl.pallas_call(
        functools.partial(_experts_kernel, n_shared_steps),
        grid_spec=grid_spec,
        out_shape=[jax.ShapeDtypeStruct(xs.shape, jnp.uint32), jax.ShapeDtypeStruct((t, d), F32)],
        compiler_params=_params(("arbitrary",)),
        name="experts",
    )(vexp, vblk, vcnt, off_ext, nvis, xs, w_gate, w_up, w_down, h1p, sg, su, sd)


def _combine_kernel(dest_ref, dest_next_ref, topw_ref, h1_ref, ysh_ref, ys_ref, g2_ref, b2_ref, o_ref,
                    buf_ref, sem):
    i = pl.program_id(0)
    tm = h1_ref.shape[0]
    slot = lax.rem(i, 2)

    def issue_block(d_ref, s):
        def issue(t, carry):
            for k in range(TOP_K):
                d = d_ref[t * TOP_K + k]
                pltpu.make_async_copy(_row_tile(ys_ref, d), _row_tile(buf_ref.at[s, k], t),
                                      sem.at[s]).start(priority=k % 2)
            return carry

        lax.fori_loop(0, tm, issue, 0)

    @pl.when(i == 0)
    def _():
        issue_block(dest_ref, 0)

    @pl.when(i + 1 < pl.num_programs(0))
    def _():
        issue_block(dest_next_ref, 1 - slot)

    y = ALPHA * h1_ref[...] + ysh_ref[...]
    for k in range(TOP_K):
        pltpu.make_async_copy(ys_ref.at[pl.ds(0, tm * ROW_TILE)], buf_ref.at[slot, k], sem.at[slot]).wait()
    w = topw_ref[...]
    for k in range(TOP_K):
        y = y + _unpack_bf16_pairs(_load_row_tiles(buf_ref.at[slot, k], tm, ROW_TILE)) * w[:, k:k + 1]
    o_ref[...] = _layer_norm(y, g2_ref[...], b2_ref[...])


def _combine(dest_flat, topw, h1, ysh, ys, g2, b2):
    t, d = h1.shape
    tm = min(MOVE_TM, t)
    nb = t // tm
    row = lambda i: (i, 0)
    fixed = lambda i: (0, 0)
    return pl.pallas_call(
        _combine_kernel,
        grid=(nb,),
        in_specs=[
            pl.BlockSpec((tm * TOP_K,), lambda i: (i,), memory_space=pltpu.SMEM),
            pl.BlockSpec((tm * TOP_K,), lambda i: (jnp.minimum(i + 1, nb - 1),), memory_space=pltpu.SMEM),
            pl.BlockSpec((tm, TOP_K), row),
            pl.BlockSpec((tm, d), row),
            pl.BlockSpec((tm, d), row),
            pl.BlockSpec(memory_space=pl.ANY),
            pl.BlockSpec((1, d), fixed), pl.BlockSpec((1, d), fixed),
        ],
        out_specs=pl.BlockSpec((tm, d), row),
        out_shape=jax.ShapeDtypeStruct((t, d), F32),
        scratch_shapes=[pltpu.VMEM((2, TOP_K, tm * ROW_TILE, V7X_LANES), ys.dtype), pltpu.SemaphoreType.DMA((2,))],
        compiler_params=_params(("arbitrary",)),
        name="combine",
    )(dest_flat, dest_flat, topw, h1, ysh, ys, g2, b2)


def kernel(x, ln_emb_g, ln_emb_b, w_in, conv_w, conv_b, b_igate, b_fgate, mlstm_norm_g, lambda_q1, lambda_k1,
           lambda_q2, lambda_k2, rel_bias, w_out, ln1_g, ln1_b, w_router, router_bias, w_gate, w_up, w_down,
           ws_gate, ws_up, ws_down, ln2_g, ln2_b):
    bsz, s, d = x.shape
    assert bsz == 1 and w_in.shape[0] == DEPTH == 1
    x2 = x.reshape(s, d)
    row = lambda a: a.reshape(1, -1).astype(F32)

    w = w_in[0]
    c0 = 2 * M_QK + 2 * M_V
    c1 = c0 + 2 * MLSTM_HEADS
    c2 = c1 + D_QK
    c3 = c2 + D_QK
    w_main = jnp.concatenate([w[:, :c0], w[:, c1:c2], w[:, c3:]], axis=1).astype(BF16)
    w_k_t = w[:, c2:c3].T.astype(BF16)
    wi, wf = w[:, c0:c0 + MLSTM_HEADS], w[:, c0 + MLSTM_HEADS:c1]
    w_gates = jnp.zeros((d, MLSTM_HEADS, V7X_LANES), F32).at[:, :, 0].set(wi).at[:, :, 1].set(wf)
    w_gates = w_gates.reshape(d, MLSTM_HEADS * V7X_LANES).astype(BF16)
    w_gates_t = jnp.zeros((MLSTM_HEADS, V7X_SUBLANES, d), F32).at[:, 0].set(wi.T).at[:, 1].set(wf.T)
    w_gates_t = w_gates_t.reshape(MLSTM_HEADS * V7X_SUBLANES, d).astype(BF16)

    proj, gates, gates_t, k_t = _ln_proj(x2, row(ln_emb_g), row(ln_emb_b), w_main, w_gates, w_gates_t, w_k_t)

    hm = _mlstm(proj, gates, gates_t, conv_w[0], row(conv_b[0]), b_igate[0].astype(F32),
                b_fgate[0].astype(F32), row(mlstm_norm_g[0]))

    lam_vecs = jnp.stack([lambda_q1[0], lambda_k1[0], lambda_q2[0], lambda_k2[0]]).astype(F32)
    hd = _diff_attn(proj, k_t, lam_vecs, rel_bias.astype(F32).reshape(-1))

    wo = w_out[0].astype(BF16)
    h1, h1p, logits_t = _out_ln(hm, hd, x2, row(ln_emb_g), row(ln_emb_b), wo[:M_V], wo[M_V:], row(ln1_g[0]),
                           row(ln1_b[0]), w_router[0].T.astype(BF16))

    topi_t, pos_t, topw_t, counts = _route(logits_t, router_bias[0].reshape(-1, 1).astype(F32))
    off, vexp, vblk, vcnt, nvis, dest_t = _plan(counts.reshape(1, -1), counts, topi_t, pos_t)
    off_ext = jnp.concatenate([off.reshape(-1), jnp.full((1,), s * TOP_K, jnp.int32)])
    dest_flat = dest_t.T.reshape(-1)

    xs = _dispatch(dest_flat, h1p)
    ys, ysh = _experts(vexp.reshape(-1), vblk.reshape(-1), vcnt.reshape(-1), off_ext, nvis.reshape(-1), xs,
                       w_gate[0], w_up[0], w_down[0], h1p, ws_gate[0].astype(BF16), ws_up[0].astype(BF16),
                       ws_down[0].astype(BF16))
    out = _combine(dest_flat, topw_t.T, h1, ysh, ys, row(ln2_g[0]), row(ln2_b[0]))
    return out.reshape(bsz, s, d)
```

```python
import functools
import math

import jax
import jax.numpy as jnp
from jax import lax
from jax.experimental import pallas as pl
from jax.experimental.pallas import tpu as pltpu

F32 = jnp.float32
BF16 = jnp.bfloat16

DEPTH = 1
MLSTM_HEADS = 4
MLSTM_DQK = 128
MLSTM_DV = 256
CONV_WIDTH = 4
DIFF_HEADS = 8
DIFF_DH = 64
DIFF_DV = 2 * DIFF_DH
REL_BUCKETS = 32
REL_MAX_DIST = 128
N_EXPERTS = 256
TOP_K = 8
N_GROUPS = 8
TOPK_GROUPS = 4
D_EXPERT = 512
ROUTED_SCALE = 2.5
LN_EPS = 1e-5
ALPHA = (2 * DEPTH) ** 0.25
LAM_INIT = 0.8 - 0.6 * math.exp(-0.3 * 0)

M_QK = MLSTM_HEADS * MLSTM_DQK
M_V = MLSTM_HEADS * MLSTM_DV
D_QK = DIFF_HEADS * 2 * DIFF_DH
D_VW = DIFF_HEADS * DIFF_DV
PROJ_W = 2 * M_QK + 2 * M_V + D_QK + D_VW
D_MODEL = M_V + D_VW

V7X_LANES = 128
V7X_SUBLANES = 8
V7X_VMEM_LIMIT_BYTES = 56 * 1024 * 1024
ROW_TILE = D_MODEL // 2 // V7X_LANES

PROJ_TM = 1024
PROJ_TN = 512
MLSTM_L = 256
ATT_BK = 1024
ATT_BQ = 2048
ATT_STRIP = 256
OUT_TM = 512
ROUTE_TM = 256
MOE_TR = 256
MOVE_TM = 256
SHARED_TM = 256
PLAN_TB = 512

NEG = -1e30
LOG2E = 1.4426950408889634


def _params(semantics):
    return pltpu.CompilerParams(dimension_semantics=semantics, vmem_limit_bytes=V7X_VMEM_LIMIT_BYTES)


def _layer_norm(x, g, b):
    mu = jnp.mean(x, -1, keepdims=True)
    xc = x - mu
    var = jnp.mean(xc * xc, -1, keepdims=True)
    return xc * lax.rsqrt(var + LN_EPS) * g + b


def _dot(a, b):
    return jnp.dot(a, b, preferred_element_type=F32)


def _dot_nt(a, b):
    return lax.dot_general(a, b, (((1,), (1,)), ((), ())), preferred_element_type=F32)


def _split3(a):
    a1 = a.astype(BF16)
    r1 = a - a1.astype(F32)
    a2 = r1.astype(BF16)
    a3 = (r1 - a2.astype(F32)).astype(BF16)
    return a1, a2, a3


def _pack_bf16_pairs(x):
    n = x.shape[1] // 2
    bits = lax.bitcast_convert_type(x.astype(BF16).astype(F32), jnp.uint32)
    return lax.shift_right_logical(bits[:, :n], jnp.uint32(16)) | (bits[:, n:] & jnp.uint32(0xFFFF0000))


def _unpack_bf16_pairs(u):
    lo = lax.bitcast_convert_type(lax.shift_left(u, jnp.uint32(16)), F32)
    hi = lax.bitcast_convert_type(u & jnp.uint32(0xFFFF0000), F32)
    return jnp.concatenate([lo, hi], axis=1)


def _store_row_tiles(ref, x, r0=0):
    m, n = x.shape[0], x.shape[1] // V7X_LANES
    for c in range(n):
        ref[pl.ds(r0 * n + c, m, stride=n), :] = x[:, c * V7X_LANES:(c + 1) * V7X_LANES]


def _load_row_tiles(ref, m, n, r0=0):
    return jnp.concatenate([ref[pl.ds(r0 * n + c, m, stride=n), :] for c in range(n)], axis=1)


def _log_sigmoid(x):
    return jnp.minimum(x, 0.0) - jnp.log(1.0 + jnp.exp(-jnp.abs(x)))


def _ln_proj_kernel(x_ref, g_ref, b_ref, w_ref, wg_ref, wgt_ref, wkt_ref, o_ref, gates_ref, gatest_ref, kt_ref,
                    hb_ref):
    @pl.when(pl.program_id(1) == 0)
    def _():
        hb = _layer_norm(x_ref[...], g_ref[...], b_ref[...]).astype(BF16)
        hb_ref[...] = hb
        gates_ref[...] = _dot(hb, wg_ref[...])
        gatest_ref[...] = _dot_nt(wgt_ref[...], hb)
        kt_ref[...] = _dot_nt(wkt_ref[...], hb).astype(kt_ref.dtype)

    o_ref[...] = _dot(hb_ref[...], w_ref[...]).astype(o_ref.dtype)


def _ln_proj(x2, g, b, w_main, w_gates, w_gates_t, w_k_t):
    s, d = x2.shape
    tm, tn = min(PROJ_TM, s), PROJ_TN
    ng = w_gates.shape[1]
    fixed = lambda i, j: (0, 0)
    return pl.pallas_call(
        _ln_proj_kernel,
        grid=(s // tm, PROJ_W // tn),
        in_specs=[
            pl.BlockSpec((tm, d), lambda i, j: (i, 0)),
            pl.BlockSpec((1, d), fixed),
            pl.BlockSpec((1, d), fixed),
            pl.BlockSpec((d, tn), lambda i, j: (0, j)),
            pl.BlockSpec((d, ng), fixed),
            pl.BlockSpec((w_gates_t.shape[0], d), fixed),
            pl.BlockSpec((D_QK, d), fixed),
        ],
        out_specs=[
            pl.BlockSpec((tm, tn), lambda i, j: (i, j)),
            pl.BlockSpec((tm, ng), lambda i, j: (i, 0)),
            pl.BlockSpec((w_gates_t.shape[0], tm), lambda i, j: (0, i)),
            pl.BlockSpec((D_QK, tm), lambda i, j: (0, i)),
        ],
        out_shape=[
            jax.ShapeDtypeStruct((s, PROJ_W), BF16),
            jax.ShapeDtypeStruct((s, ng), F32),
            jax.ShapeDtypeStruct((w_gates_t.shape[0], s), F32),
            jax.ShapeDtypeStruct((D_QK, s), BF16),
        ],
        scratch_shapes=[pltpu.VMEM((tm, d), BF16)],
        compiler_params=_params(("arbitrary", "arbitrary")),
        name="ln_proj",
    )(x2, g, b, w_main, w_gates, w_gates_t, w_k_t)


def _mlstm_kernel(bi_ref, bf_ref, mq_ref, mk_ref, mv_ref, mo_ref, gates_ref, gatest_ref,
                  cw_ref, cb_ref, ng_ref, o_ref, c_ref, n_ref, m_ref, tail_ref):
    L = mq_ref.shape[0]
    dk, dv = MLSTM_DQK, MLSTM_DV

    @pl.when(pl.program_id(0) == 0)
    def _():
        c_ref[...] = jnp.zeros_like(c_ref)
        n_ref[...] = jnp.zeros_like(n_ref)
        m_ref[...] = jnp.zeros_like(m_ref)
        tail_ref[...] = jnp.zeros_like(tail_ref)

    u = jnp.concatenate([mq_ref[...], mk_ref[...]], axis=1).astype(F32)
    tail = tail_ref[...]
    row8 = lax.broadcasted_iota(jnp.int32, (V7X_SUBLANES, u.shape[1]), 0)
    conv = cb_ref[...] + cw_ref[CONV_WIDTH - 1:CONV_WIDTH, :] * u
    for back in range(1, CONV_WIDTH):
        ur = pltpu.roll(u, back, 0)
        head = jnp.where(row8 < back, pltpu.roll(tail, back, 0), ur[:V7X_SUBLANES])
        shifted = jnp.concatenate([head, ur[V7X_SUBLANES:]], axis=0)
        conv = conv + cw_ref[CONV_WIDTH - 1 - back:CONV_WIDTH - back, :] * shifted
    tail_ref[...] = u[L - V7X_SUBLANES:]
    qk = conv * jax.nn.sigmoid(conv)
    q_all = qk[:, :M_QK]
    k_all = qk[:, M_QK:] * (dk ** -0.5)

    r_i = lax.broadcasted_iota(jnp.int32, (L, L), 0)
    c_i = lax.broadcasted_iota(jnp.int32, (L, L), 1)
    causal = c_i <= r_i
    tril = jnp.where(causal, 1.0, 0.0).astype(BF16)
    triu = jnp.where(r_i <= c_i, 1.0, 0.0).astype(BF16)

    for h in range(MLSTM_HEADS):
        q = q_all[:, h * dk:(h + 1) * dk]
        k = k_all[:, h * dk:(h + 1) * dk]
        qb, kb = q.astype(BF16), k.astype(BF16)
        vb = mv_ref[:, h * dv:(h + 1) * dv]
        b_i, b_f = bi_ref[h], bf_ref[h]

        gblk = gates_ref[:, h * V7X_LANES:(h + 1) * V7X_LANES]
        i_col = gblk[:, 0:1] + b_i
        lf_blk = _log_sigmoid(gblk + b_f)
        f1, f2, f3 = _split3(lf_blk)
        bcum_blk = _dot(tril, f1) + _dot(tril, f2) + _dot(tril, f3)
        b_col = bcum_blk[:, 1:2]
        gt = gatest_ref[h * V7X_SUBLANES:(h + 1) * V7X_SUBLANES, :]
        i_row = gt[0:1, :] + b_i
        lf_rows = _log_sigmoid(gt + b_f)
        g1, g2, g3 = _split3(lf_rows)
        b_row = (_dot(g1, triu) + _dot(g2, triu) + _dot(g3, triu))[1:2, :]

        m_prev = m_ref[h, 0:1, 0:1]
        dmat = jnp.where(causal, b_col - b_row + i_row, NEG)
        inter = b_col + m_prev
        m_t = jnp.maximum(inter, jnp.max(dmat, -1, keepdims=True))
        wts = jnp.exp(dmat - m_t)
        g = jnp.exp(inter - m_t)
        sqk = _dot_nt(qb, kb) * wts
        c_prev = c_ref[h]
        n_prev = n_ref[h, 0:1, :]
        num = g * _dot(qb, c_prev.astype(BF16)) + _dot(sqk.astype(BF16), vb)
        den = g * jnp.sum(q * n_prev, -1, keepdims=True) + jnp.sum(sqk, -1, keepdims=True)
        hh = num / jnp.maximum(jnp.abs(den), jnp.exp(-m_t))

        b_last = b_col[L - 1:L, :]
        w_last_row = b_last - b_row + i_row
        m_new = jnp.maximum(b_last + m_prev, jnp.max(w_last_row, -1, keepdims=True))
        decay = jnp.exp(b_last + m_prev - m_new)
        ws_col = jnp.exp(b_last - b_col + i_col - m_new)
        kw = k * ws_col
        c_ref[h] = decay * c_prev + _dot(kw.T.astype(BF16), vb)
        n_ref[h, 0:1, :] = decay * n_prev + jnp.sum(kw, 0, keepdims=True)
        m_ref[h] = jnp.broadcast_to(m_new, m_ref.shape[1:])

        mu = jnp.mean(hh, -1, keepdims=True)
        hc = hh - mu
        var = jnp.mean(hc * hc, -1, keepdims=True)
        hn = hc * lax.rsqrt(var + LN_EPS) * ng_ref[:, h * dv:(h + 1) * dv]
        og = jax.nn.sigmoid(mo_ref[:, h * dv:(h + 1) * dv].astype(F32))
        o_ref[:, h * dv:(h + 1) * dv] = (hn * og).astype(o_ref.dtype)


def _mlstm(proj, gates, gates_t, conv_w, conv_b, b_i, b_f, norm_g):
    s = proj.shape[0]
    L = min(MLSTM_L, s)
    smem = pl.BlockSpec(memory_space=pltpu.SMEM)
    return pl.pallas_call(
        _mlstm_kernel,
        grid=(s // L,),
        in_specs=[
            smem, smem,
            pl.BlockSpec((L, M_QK), lambda c: (c, 0)),
            pl.BlockSpec((L, M_QK), lambda c: (c, 1)),
            pl.BlockSpec((L, M_V), lambda c: (c, 1)),
            pl.BlockSpec((L, M_V), lambda c: (c, 2)),
            pl.BlockSpec((L, gates.shape[1]), lambda c: (c, 0)),
            pl.BlockSpec((gates_t.shape[0], L), lambda c: (0, c)),
            pl.BlockSpec((CONV_WIDTH, 2 * M_QK), lambda c: (0, 0)),
            pl.BlockSpec((1, 2 * M_QK), lambda c: (0, 0)),
            pl.BlockSpec((1, M_V), lambda c: (0, 0)),
        ],
        out_specs=pl.BlockSpec((L, M_V), lambda c: (c, 0)),
        out_shape=jax.ShapeDtypeStruct((s, M_V), BF16),
        scratch_shapes=[
            pltpu.VMEM((MLSTM_HEADS, MLSTM_DQK, MLSTM_DV), F32),
            pltpu.VMEM((MLSTM_HEADS, V7X_SUBLANES, MLSTM_DQK), F32),
            pltpu.VMEM((MLSTM_HEADS, V7X_SUBLANES, V7X_LANES), F32),
            pltpu.VMEM((V7X_SUBLANES, 2 * M_QK), F32),
        ],
        compiler_params=_params(("arbitrary",)),
        name="mlstm",
    )(b_i, b_f, proj, proj, proj, proj, gates, gates_t, conv_w, conv_b, norm_g)


def _t5_bias_tile(table_ref, h, offset):
    n_t = V7X_LANES
    r_i = lax.broadcasted_iota(jnp.int32, (n_t, n_t), 0)
    c_i = lax.broadcasted_iota(jnp.int32, (n_t, n_t), 1)
    n = jnp.maximum(offset + r_i - c_i, 0)
    max_exact = REL_BUCKETS // 2
    large = max_exact + (jnp.log(jnp.maximum(n, 1).astype(F32) / max_exact)
                         / math.log(REL_MAX_DIST / max_exact) * (REL_BUCKETS - max_exact)).astype(jnp.int32)
    large = jnp.minimum(large, REL_BUCKETS - 1)
    bucket = jnp.where(n < max_exact, n, large)
    out = jnp.zeros((n_t, n_t), F32)
    for b in range(REL_BUCKETS):
        out = jnp.where(bucket == b, table_ref[b * DIFF_HEADS + h], out)
    return out * LOG2E


def _diff_attn_kernel(qi_ref, kj_ref, q_ref, kt_ref, v_ref, lam_ref, table_ref, o_ref,
                      q1_ref, q2_ref, vx_ref, m_ref, acc_ref, bdiag_ref, bsub_ref):
    h = pl.program_id(0)
    p = pl.program_id(1)
    i = qi_ref[p]
    j = kj_ref[p]
    BQ = q_ref.shape[0]
    BK = v_ref.shape[0]
    n_half = BQ // BK
    SR = min(ATT_STRIP, BK)
    strips_per_half = BK // SR
    n_sub = BK // V7X_LANES
    c_far = table_ref[(REL_BUCKETS - 1) * DIFF_HEADS + h] * LOG2E

    @pl.when(p == 0)
    def _():
        p0 = _t5_bias_tile(table_ref, h, 0)
        p0 = jnp.where(lax.broadcasted_iota(jnp.int32, p0.shape, 1) <= lax.broadcasted_iota(jnp.int32, p0.shape, 0),
                       p0, NEG)
        p1 = _t5_bias_tile(table_ref, h, V7X_LANES)
        far = jnp.full((V7X_LANES, V7X_LANES), c_far, F32)
        neg = jnp.full((V7X_LANES, V7X_LANES), NEG, F32)
        for a in range(n_sub):
            for b in range(n_sub):
                tile = p0 if a == b else p1 if a == b + 1 else far if a > b else neg
                bdiag_ref[a * V7X_LANES:(a + 1) * V7X_LANES, b * V7X_LANES:(b + 1) * V7X_LANES] = tile
        bsub_ref[...] = jnp.full(bsub_ref.shape, c_far, F32)
        bsub_ref[0:V7X_LANES, BK - V7X_LANES:BK] = p1

    @pl.when(j == 0)
    def _():
        q = q_ref[...].astype(F32) * ((DIFF_DH ** -0.5) * LOG2E)
        lane = lax.broadcasted_iota(jnp.int32, q.shape, 1)
        q1_ref[...] = jnp.where(lane < DIFF_DH, q, 0.0).astype(BF16)
        q2_ref[...] = jnp.where(lane >= DIFF_DH, q, 0.0).astype(BF16)
        m_ref[...] = jnp.full(m_ref.shape, NEG, F32)
        acc_ref[...] = jnp.zeros_like(acc_ref)

    vx_ref[:, 0:DIFF_DV] = v_ref[...]
    vx_ref[:, DIFF_DV:] = jnp.ones((BK, DIFF_DV), BF16)

    def strip(r, kl, bias_tile=None):
        rows = slice(r * SR, (r + 1) * SR)
        kb = kt_ref[:, 0:kl]
        vb = vx_ref[0:kl, :]
        for a, qz_ref in enumerate((q1_ref, q2_ref)):
            s = _dot(qz_ref[rows, :], kb)
            m_old = m_ref[a, rows, :]
            if bias_tile is None:
                m_new = jnp.maximum(m_old, jnp.max(s, -1, keepdims=True) + c_far)
                pm = jnp.exp2(s - (m_new - c_far))
            else:
                s = s + bias_tile
                m_new = jnp.maximum(m_old, jnp.max(s, -1, keepdims=True))
                pm = jnp.exp2(s - m_new)
            alpha = jnp.exp2(m_old - m_new)
            acc_ref[a, rows, :] = alpha * acc_ref[a, rows, :] + _dot(pm.astype(BF16), vb)
            m_ref[a, rows, :] = m_new

    def key_block(delta):
        for r in range(n_half * strips_per_half):
            ro = (r % strips_per_half) * SR
            rel = None if delta is None else delta - r // strips_per_half
            if rel is None or rel <= -2:
                strip(r, BK)
            elif rel == -1:
                strip(r, BK, bsub_ref[...] if ro == 0 else None)
            elif rel == 0:
                strip(r, ro + SR, bdiag_ref[ro:ro + SR, 0:ro + SR])

    d = j - i * n_half

    @pl.when(d <= -2)
    def _():
        key_block(None)

    for delta in range(-1, n_half):
        @pl.when(d == delta)
        def _(delta=delta):
            key_block(delta)
            if delta == n_half - 1:
                lq1, lk1, lq2, lk2 = (lam_ref[t:t + 1, :] for t in range(4))
                lam = (jnp.exp(jnp.sum(lq1 * lk1, -1, keepdims=True))
                       - jnp.exp(jnp.sum(lq2 * lk2, -1, keepdims=True)) + LAM_INIT)
                o = (acc_ref[0, :, 0:DIFF_DV] / acc_ref[0, :, DIFF_DV:DIFF_DV + 1]
                     - lam * (acc_ref[1, :, 0:DIFF_DV] / acc_ref[1, :, DIFF_DV:DIFF_DV + 1]))
                o = o * lax.rsqrt(jnp.mean(o * o, -1, keepdims=True) + LN_EPS)
                o_ref[...] = (o * (1.0 - LAM_INIT)).astype(o_ref.dtype)


def _diff_attn(proj, k_t, lam_vecs, table_flat):
    s = proj.shape[0]
    BK = min(ATT_BK, s)
    BQ = min(ATT_BQ, s)
    n_half = BQ // BK
    pairs = [(i, j) for i in range(s // BQ) for j in range((i + 1) * n_half)]
    qi = jnp.asarray([a for a, _ in pairs], jnp.int32)
    kj = jnp.asarray([b for _, b in pairs], jnp.int32)
    q_blk = (2 * M_QK + 2 * M_V) // V7X_LANES
    v_blk = q_blk + D_QK // V7X_LANES
    SR = min(ATT_STRIP, BK)
    grid_spec = pltpu.PrefetchScalarGridSpec(
        num_scalar_prefetch=2,
        grid=(DIFF_HEADS, len(pairs)),
        in_specs=[
            pl.BlockSpec((BQ, V7X_LANES), lambda h, p, qi, kj: (qi[p], q_blk + h)),
            pl.BlockSpec((2 * DIFF_DH, BK), lambda h, p, qi, kj: (h, kj[p])),
            pl.BlockSpec((BK, V7X_LANES), lambda h, p, qi, kj: (kj[p], v_blk + h)),
            pl.BlockSpec((4, DIFF_DH), lambda h, p, qi, kj: (0, 0)),
            pl.BlockSpec(memory_space=pltpu.SMEM),
        ],
        out_specs=pl.BlockSpec((BQ, V7X_LANES), lambda h, p, qi, kj: (qi[p], h)),
        scratch_shapes=[
            pltpu.VMEM((BQ, V7X_LANES), BF16),
            pltpu.VMEM((BQ, V7X_LANES), BF16),
            pltpu.VMEM((BK, 2 * DIFF_DV), BF16),
            pltpu.VMEM((2, BQ, 1), F32),
            pltpu.VMEM((2, BQ, 2 * DIFF_DV), F32),
            pltpu.VMEM((BK, BK), F32),
            pltpu.VMEM((SR, BK), F32),
        ],
    )
    return pl.pallas_call(
        _diff_attn_kernel,
        grid_spec=grid_spec,
        out_shape=jax.ShapeDtypeStruct((s, D_VW), BF16),
        compiler_params=_params(("arbitrary", "arbitrary")),
        name="diff_attn",
    )(qi, kj, proj, k_t, proj, lam_vecs, table_flat)


def _out_ln_kernel(hm_ref, hd_ref, x_ref, g0_ref, b0_ref, wo1_ref, wo2_ref, g1_ref, b1_ref, wr_ref,
                   h1_ref, h1p_ref, logit_ref):
    mix = _dot(hm_ref[...], wo1_ref[...]) + _dot(hd_ref[...], wo2_ref[...])
    h0 = _layer_norm(x_ref[...], g0_ref[...], b0_ref[...])
    h1 = _layer_norm(ALPHA * h0 + mix, g1_ref[...], b1_ref[...])
    h1_ref[...] = h1
    _store_row_tiles(h1p_ref, _pack_bf16_pairs(h1))
    logit_ref[...] = _dot_nt(wr_ref[...], h1.astype(BF16))


def _out_ln(hm, hd, x2, g0, b0, wo1, wo2, g1, b1, wr):
    s, d = x2.shape
    tm = min(OUT_TM, s)
    row = lambda i: (i, 0)
    fixed = lambda i: (0, 0)
    return pl.pallas_call(
        _out_ln_kernel,
        grid=(s // tm,),
        in_specs=[
            pl.BlockSpec((tm, M_V), row), pl.BlockSpec((tm, D_VW), row), pl.BlockSpec((tm, d), row),
            pl.BlockSpec((1, d), fixed), pl.BlockSpec((1, d), fixed),
            pl.BlockSpec((M_V, d), fixed), pl.BlockSpec((D_VW, d), fixed),
            pl.BlockSpec((1, d), fixed), pl.BlockSpec((1, d), fixed),
            pl.BlockSpec((N_EXPERTS, d), fixed),
        ],
        out_specs=[pl.BlockSpec((tm, d), row), pl.BlockSpec((tm * ROW_TILE, V7X_LANES), row),
                   pl.BlockSpec((N_EXPERTS, tm), lambda i: (0, i))],
        out_shape=[jax.ShapeDtypeStruct((s, d), F32), jax.ShapeDtypeStruct((s * ROW_TILE, V7X_LANES), jnp.uint32),
                   jax.ShapeDtypeStruct((N_EXPERTS, s), F32)],
        compiler_params=_params(("arbitrary",)),
        name="out_ln",
    )(hm, hd, x2, g0, b0, wo1, wo2, g1, b1, wr)


def _route_kernel(logit_ref, rb_ref, topi_ref, pos_ref, topw_ref, cnt_ref, carry_ref):
    n_e, tm = logit_ref.shape
    gsz = n_e // N_GROUPS
    ninf = -jnp.inf

    @pl.when(pl.program_id(0) == 0)
    def _():
        carry_ref[...] = jnp.zeros_like(carry_ref)

    sc = jax.nn.sigmoid(logit_ref[...])
    sel = sc + rb_ref[...]

    sel3 = sel.reshape(N_GROUPS, gsz, tm)
    in_grp = lax.broadcasted_iota(jnp.int32, sel3.shape, 1)
    m1 = jnp.max(sel3, 1, keepdims=True)
    i1 = jnp.min(jnp.where(sel3 == m1, in_grp, gsz), 1, keepdims=True)
    m2 = jnp.max(jnp.where(in_grp == i1, ninf, sel3), 1, keepdims=True)
    gscore = (m1 + m2).reshape(N_GROUPS, tm)
    gid = lax.broadcasted_iota(jnp.int32, gscore.shape, 0)
    beaten = jnp.zeros(gscore.shape, F32)
    for o in range(1, N_GROUPS):
        other = pltpu.roll(gscore, o, 0)
        wins = (other > gscore) | ((other == gscore) & (gid >= o))
        beaten = beaten + jnp.where(wins, 1.0, 0.0)
    keep = jnp.where(beaten < TOPK_GROUPS, 1.0, 0.0)
    keep3 = jnp.broadcast_to(keep.reshape(N_GROUPS, 1, tm), sel3.shape)
    masked = jnp.where(keep3 > 0.5, sel3, ninf).reshape(n_e, tm)

    eid = lax.broadcasted_iota(jnp.int32, (n_e, tm), 0)
    onehots, idxs, ws = [], [], []
    for _ in range(TOP_K):
        m = jnp.max(masked, 0, keepdims=True)
        idx = jnp.min(jnp.where(masked == m, eid, n_e), 0, keepdims=True)
        oh = eid == idx
        onehots.append(oh)
        idxs.append(idx)
        ws.append(jnp.sum(jnp.where(oh, sc, 0.0), 0, keepdims=True))
        masked = jnp.where(oh, ninf, masked)
    wsum = ws[0]
    for w in ws[1:]:
        wsum = wsum + w

    chosen = jnp.zeros((n_e, tm), F32)
    for oh in onehots:
        chosen = jnp.where(oh, 1.0, chosen)
    r_i = lax.broadcasted_iota(jnp.int32, (tm, tm), 0)
    c_i = lax.broadcasted_iota(jnp.int32, (tm, tm), 1)
    before = jnp.where(r_i < c_i, 1.0, 0.0).astype(BF16)
    rank = _dot(chosen.astype(BF16), before) + carry_ref[...]
    carry_new = carry_ref[...] + jnp.sum(chosen, 1, keepdims=True)
    carry_ref[...] = carry_new
    cnt_ref[...] = carry_new

    kid = lax.broadcasted_iota(jnp.int32, (TOP_K, tm), 0)
    topi = jnp.zeros((TOP_K, tm), jnp.int32)
    pos = jnp.zeros((TOP_K, tm), jnp.int32)
    topw = jnp.zeros((TOP_K, tm), F32)
    for k in range(TOP_K):
        pk = jnp.sum(jnp.where(onehots[k], rank, 0.0), 0, keepdims=True)
        topi = jnp.where(kid == k, idxs[k], topi)
        pos = jnp.where(kid == k, pk.astype(jnp.int32), pos)
        topw = jnp.where(kid == k, ws[k] / wsum * ROUTED_SCALE, topw)
    topi_ref[...] = topi
    pos_ref[...] = pos
    topw_ref[...] = topw


def _route(logits_t, router_bias_col):
    n_e, t = logits_t.shape
    tm = min(ROUTE_TM, t)
    col = lambda i: (0, i)
    fixed = lambda i: (0, 0)
    return pl.pallas_call(
        _route_kernel,
        grid=(t // tm,),
        in_specs=[pl.BlockSpec((n_e, tm), col), pl.BlockSpec((n_e, 1), fixed)],
        out_specs=[pl.BlockSpec((TOP_K, tm), col), pl.BlockSpec((TOP_K, tm), col),
                   pl.BlockSpec((TOP_K, tm), col), pl.BlockSpec((n_e, 1), fixed)],
        out_shape=[jax.ShapeDtypeStruct((TOP_K, t), jnp.int32), jax.ShapeDtypeStruct((TOP_K, t), jnp.int32),
                   jax.ShapeDtypeStruct((TOP_K, t), F32), jax.ShapeDtypeStruct((n_e, 1), F32)],
        scratch_shapes=[pltpu.VMEM((n_e, 1), F32)],
        compiler_params=_params(("arbitrary",)),
        name="route",
    )(logits_t, router_bias_col)


def _plan_kernel(cnt_ref, cntc_ref, topi_ref, pos_ref, off_ref, vexp_ref, vblk_ref, vcnt_ref, nvis_ref, dest_ref):
    nv = vexp_ref.shape[0]
    tr = float(MOE_TR)
    cnt = jnp.broadcast_to(cnt_ref[...], (V7X_SUBLANES, N_EXPERTS))
    r_i = lax.broadcasted_iota(jnp.int32, (N_EXPERTS, N_EXPERTS), 0)
    c_i = lax.broadcasted_iota(jnp.int32, (N_EXPERTS, N_EXPERTS), 1)
    upper = jnp.where(r_i <= c_i, 1.0, 0.0).astype(BF16)

    def cumsum_lanes(a):
        a1, a2, a3 = _split3(a)
        return _dot(a1, upper) + _dot(a2, upper) + _dot(a3, upper)

    end = cumsum_lanes(cnt)
    start = end - cnt
    first_blk = jnp.floor(start / tr)
    last_blk = jnp.floor((end - 1.0) / tr)
    nvis_e = jnp.where(cnt > 0.0, last_blk - first_blk + 1.0, 0.0)
    vend = cumsum_lanes(nvis_e)
    vstart = vend - nvis_e
    off_ref[...] = start[0:1, :].astype(jnp.int32)
    nvis_ref[...] = vend[0:1, N_EXPERTS - 1:N_EXPERTS].astype(jnp.int32)

    v = lax.broadcasted_iota(jnp.int32, (nv, N_EXPERTS), 0).astype(F32)
    ve = jnp.broadcast_to(vend[0:1, :], (nv, N_EXPERTS))
    expert = jnp.sum(jnp.where(ve <= v, 1.0, 0.0), -1, keepdims=True)
    expert = jnp.minimum(expert, N_EXPERTS - 1.0)
    lane = lax.broadcasted_iota(jnp.int32, (nv, N_EXPERTS), 1).astype(F32)
    mine = lane == expert
    fb = jnp.sum(jnp.where(mine, jnp.broadcast_to(first_blk[0:1, :], (nv, N_EXPERTS)), 0.0), -1, keepdims=True)
    vs = jnp.sum(jnp.where(mine, jnp.broadcast_to(vstart[0:1, :], (nv, N_EXPERTS)), 0.0), -1, keepdims=True)
    vc = jnp.sum(jnp.where(mine, jnp.broadcast_to(nvis_e[0:1, :], (nv, N_EXPERTS)), 0.0), -1, keepdims=True)
    vexp_ref[...] = expert.astype(jnp.int32)
    vblk_ref[...] = (fb + (v[:, 0:1] - vs)).astype(jnp.int32)
    vcnt_ref[...] = vc.astype(jnp.int32)

    lower = jnp.where(c_i < r_i, 1.0, 0.0).astype(BF16)
    c1, c2, c3 = _split3(jnp.broadcast_to(cntc_ref[...], (N_EXPERTS, V7X_LANES)))
    start_col = (_dot(lower, c1) + _dot(lower, c2) + _dot(lower, c3))[:, 0:1]
    n_tok = topi_ref.shape[1]
    tb = min(PLAN_TB, n_tok)
    eid = lax.broadcasted_iota(jnp.int32, (N_EXPERTS, tb), 0)
    kid = lax.broadcasted_iota(jnp.int32, (TOP_K, tb), 0)

    def dest_block(i, carry):
        cols = pl.ds(pl.multiple_of(i * tb, tb), tb)
        ti = topi_ref[:, cols]
        first_row = jnp.zeros((TOP_K, tb), F32)
        for k in range(TOP_K):
            fr = jnp.sum(jnp.where(eid == ti[k:k + 1, :], start_col, 0.0), 0, keepdims=True)
            first_row = jnp.where(kid == k, fr, first_row)
        dest_ref[:, cols] = first_row.astype(jnp.int32) + pos_ref[:, cols]
        return carry

    lax.fori_loop(0, n_tok // tb, dest_block, 0)


def _plan(counts_row, counts_col, topi_t, pos_t):
    n_rows = topi_t.shape[0] * topi_t.shape[1]
    nv = n_rows // MOE_TR + N_EXPERTS
    return pl.pallas_call(
        _plan_kernel,
        out_shape=[jax.ShapeDtypeStruct((1, N_EXPERTS), jnp.int32), jax.ShapeDtypeStruct((nv, 1), jnp.int32),
                   jax.ShapeDtypeStruct((nv, 1), jnp.int32), jax.ShapeDtypeStruct((nv, 1), jnp.int32),
                   jax.ShapeDtypeStruct((1, 1), jnp.int32), jax.ShapeDtypeStruct(topi_t.shape, jnp.int32)],
        compiler_params=pltpu.CompilerParams(vmem_limit_bytes=V7X_VMEM_LIMIT_BYTES),
        name="plan",
    )(counts_row, counts_col, topi_t, pos_t)


def _row_tile(ref, r):
    return ref.at[pl.ds(pl.multiple_of(r * ROW_TILE, ROW_TILE), ROW_TILE)]


def _dispatch_kernel(dest_ref, x_ref, xs_ref, sem):
    tm = x_ref.shape[0] // ROW_TILE

    def issue(t, carry):
        for k in range(TOP_K):
            d = dest_ref[t * TOP_K + k]
            pltpu.make_async_copy(_row_tile(x_ref, t), _row_tile(xs_ref, d), sem).start(priority=k % 2)
        return carry

    lax.fori_loop(0, tm, issue, 0)
    for _ in range(TOP_K):
        pltpu.make_async_copy(x_ref, xs_ref.at[pl.ds(0, tm * ROW_TILE)], sem).wait()


def _dispatch(dest_flat, rows):
    t = rows.shape[0] // ROW_TILE
    tm = min(MOVE_TM, t)
    return pl.pallas_call(
        _dispatch_kernel,
        grid=(t // tm,),
        in_specs=[
            pl.BlockSpec((tm * TOP_K,), lambda i: (i,), memory_space=pltpu.SMEM),
            pl.BlockSpec((tm * ROW_TILE, V7X_LANES), lambda i: (i, 0)),
        ],
        out_specs=pl.BlockSpec(memory_space=pl.ANY),
        out_shape=jax.ShapeDtypeStruct((t * TOP_K * ROW_TILE, V7X_LANES), rows.dtype),
        scratch_shapes=[pltpu.SemaphoreType.DMA(())],
        compiler_params=_params(("arbitrary",)),
        name="dispatch",
    )(dest_flat, rows)


def _experts_kernel(n_shared_steps, vexp_ref, vblk_ref, vcnt_ref, off_ref, nvis_ref, xs_ref, wg_hbm, wu_hbm,
                    wd_hbm, hp_ref, sg_ref, su_ref, sd_ref, ys_ref, ysh_ref,
                    wg_buf, wu_buf, wd_buf, acc_ref, slot_ref, sem):
    v = pl.program_id(0)
    tr = xs_ref.shape[0] // ROW_TILE
    nvis = nvis_ref[0]

    @pl.when(v < n_shared_steps)
    def _():
        hb = _unpack_bf16_pairs(_load_row_tiles(hp_ref, hp_ref.shape[0] // ROW_TILE, ROW_TILE)).astype(BF16)
        gate = _dot(hb, sg_ref[...])
        up = _dot(hb, su_ref[...])
        ysh_ref[...] = _dot((gate * jax.nn.sigmoid(gate) * up).astype(BF16), sd_ref[...])

    half = wd_buf.shape[1] // 2

    def weight_copies(e, slot):
        return ((pltpu.make_async_copy(wg_hbm.at[e], wg_buf.at[slot], sem.at[slot, 0]), 1),
                (pltpu.make_async_copy(wu_hbm.at[e], wu_buf.at[slot], sem.at[slot, 1]), 1),
                (pltpu.make_async_copy(wd_hbm.at[e, pl.ds(0, half)], wd_buf.at[slot, pl.ds(0, half)],
                                       sem.at[slot, 2]), 1),
                (pltpu.make_async_copy(wd_hbm.at[e, pl.ds(half, half)], wd_buf.at[slot, pl.ds(half, half)],
                                       sem.at[slot, 3]), 1))

    @pl.when(v < nvis)
    def _():
        e = vexp_ref[v]
        blk = vblk_ref[v]
        first_of_expert = jnp.logical_or(v == 0, vexp_ref[jnp.maximum(v - 1, 0)] != e)

        @pl.when(v == 0)
        def _():
            slot_ref[0] = 0
            for c, queue in weight_copies(e, 0):
                c.start(priority=queue)

        @pl.when(jnp.logical_and(first_of_expert, v > 0))
        def _():
            slot_ref[0] = 1 - slot_ref[0]

        slot = slot_ref[0]

        @pl.when(first_of_expert)
        def _():
            nxt = v + vcnt_ref[v]

            @pl.when(nxt < nvis)
            def _():
                for c, queue in weight_copies(vexp_ref[jnp.minimum(nxt, vexp_ref.shape[0] - 1)], 1 - slot):
                    c.start(priority=queue)

            for c, _ in weight_copies(e, slot):
                c.wait()

        lo = off_ref[e] - blk * tr
        hi = off_ref[e + 1] - blk * tr
        first_of_block = jnp.logical_or(v == 0, vblk_ref[jnp.maximum(v - 1, 0)] != blk)

        @pl.when(first_of_block)
        def _():
            acc_ref[...] = jnp.zeros_like(acc_ref)

        def swiglu_rows(r0, n):
            row = r0 + lax.broadcasted_iota(jnp.int32, (n, 1), 0)
            mine = (row >= lo) & (row < hi)
            xb = _unpack_bf16_pairs(_load_row_tiles(xs_ref, n, ROW_TILE, r0)).astype(BF16)
            gate = _dot(xb, wg_buf[slot].astype(BF16))
            up = _dot(xb, wu_buf[slot].astype(BF16))
            act = (gate * jax.nn.sigmoid(gate) * up).astype(BF16)
            acc_ref[r0:r0 + n, :] += jnp.where(mine, _dot(act, wd_buf[slot].astype(BF16)), 0.0)

        hr = tr // 2
        top_only = hi <= hr
        bottom_only = lo >= hr

        @pl.when(top_only)
        def _():
            swiglu_rows(0, hr)

        @pl.when(bottom_only)
        def _():
            swiglu_rows(hr, hr)

        @pl.when(jnp.logical_not(jnp.logical_or(top_only, bottom_only)))
        def _():
            swiglu_rows(0, tr)

        last_of_block = jnp.logical_or(v + 1 >= nvis, vblk_ref[jnp.minimum(v + 1, vblk_ref.shape[0] - 1)] != blk)

        @pl.when(last_of_block)
        def _():
            _store_row_tiles(ys_ref, _pack_bf16_pairs(acc_ref[...]))


def _experts(vexp, vblk, vcnt, off_ext, nvis, xs, w_gate, w_up, w_down, h1p, sg, su, sd):
    n_rows = xs.shape[0] // ROW_TILE
    d = D_MODEL
    nv = vexp.shape[0]
    t = h1p.shape[0] // ROW_TILE
    tm = min(SHARED_TM, t)
    n_shared_steps = t // tm
    assert n_shared_steps <= n_rows // MOE_TR

    def block_of(v, ve, vb, vc, off, nvis):
        return (vb[jnp.minimum(v, nvis[0] - 1)], 0)

    def shared_block(v, ve, vb, vc, off, nvis):
        return (jnp.minimum(v, n_shared_steps - 1), 0)

    fixed = lambda v, ve, vb, vc, off, nvis: (0, 0)
    resident = dict(pipeline_mode=pl.Buffered(1))
    grid_spec = pltpu.PrefetchScalarGridSpec(
        num_scalar_prefetch=5,
        grid=(nv,),
        in_specs=[
            pl.BlockSpec((MOE_TR * ROW_TILE, V7X_LANES), block_of),
            pl.BlockSpec(memory_space=pl.ANY),
            pl.BlockSpec(memory_space=pl.ANY),
            pl.BlockSpec(memory_space=pl.ANY),
            pl.BlockSpec((tm * ROW_TILE, V7X_LANES), shared_block),
            pl.BlockSpec((d, D_EXPERT), fixed, **resident),
            pl.BlockSpec((d, D_EXPERT), fixed, **resident),
            pl.BlockSpec((D_EXPERT, d), fixed, **resident),
        ],
        out_specs=[pl.BlockSpec((MOE_TR * ROW_TILE, V7X_LANES), block_of), pl.BlockSpec((tm, d), shared_block)],
        scratch_shapes=[
            pltpu.VMEM((2, d, D_EXPERT), F32),
            pltpu.VMEM((2, d, D_EXPERT), F32),
            pltpu.VMEM((2, D_EXPERT, d), F32),
            pltpu.VMEM((MOE_TR, d), F32),
            pltpu.SMEM((1,), jnp.int32),
            pltpu.SemaphoreType.DMA((2, 4)),
        ],
    )
    return pl.pallas_call(
        functools.partial(_experts_kernel, n_shared_steps),
        grid_spec=grid_spec,
        out_shape=[jax.ShapeDtypeStruct(xs.shape, jnp.uint32), jax.ShapeDtypeStruct((t, d), F32)],
        compiler_params=_params(("arbitrary",)),
        name="experts",
    )(vexp, vblk, vcnt, off_ext, nvis, xs, w_gate, w_up, w_down, h1p, sg, su, sd)


def _combine_kernel(dest_ref, dest_next_ref, topw_ref, h1_ref, ysh_ref, ys_ref, g2_ref, b2_ref, o_ref,
                    buf_ref, sem):
    i = pl.program_id(0)
    tm = h1_ref.shape[0]
    slot = lax.rem(i, 2)

    def issue_block(d_ref, s):
        def issue(t, carry):
            for k in range(TOP_K):
                d = d_ref[t * TOP_K + k]
                pltpu.make_async_copy(_row_tile(ys_ref, d), _row_tile(buf_ref.at[s, k], t),
                                      sem.at[s]).start(priority=k % 2)
            return carry

        lax.fori_loop(0, tm, issue, 0)

    @pl.when(i == 0)
    def _():
        issue_block(dest_ref, 0)

    @pl.when(i + 1 < pl.num_programs(0))
    def _():
        issue_block(dest_next_ref, 1 - slot)

    y = ALPHA * h1_ref[...] + ysh_ref[...]
    for k in range(TOP_K):
        pltpu.make_async_copy(ys_ref.at[pl.ds(0, tm * ROW_TILE)], buf_ref.at[slot, k], sem.at[slot]).wait()
    w = topw_ref[...]
    for k in range(TOP_K):
        y = y + _unpack_bf16_pairs(_load_row_tiles(buf_ref.at[slot, k], tm, ROW_TILE)) * w[:, k:k + 1]
    o_ref[...] = _layer_norm(y, g2_ref[...], b2_ref[...])


def _combine(dest_flat, topw, h1, ysh, ys, g2, b2):
    t, d = h1.shape
    tm = min(MOVE_TM, t)
    nb = t // tm
    row = lambda i: (i, 0)
    fixed = lambda i: (0, 0)
    return pl.pallas_call(
        _combine_kernel,
        grid=(nb,),
        in_specs=[
            pl.BlockSpec((tm * TOP_K,), lambda i: (i,), memory_space=pltpu.SMEM),
            pl.BlockSpec((tm * TOP_K,), lambda i: (jnp.minimum(i + 1, nb - 1),), memory_space=pltpu.SMEM),
            pl.BlockSpec((tm, TOP_K), row),
            pl.BlockSpec((tm, d), row),
            pl.BlockSpec((tm, d), row),
            pl.BlockSpec(memory_space=pl.ANY),
            pl.BlockSpec((1, d), fixed), pl.BlockSpec((1, d), fixed),
        ],
        out_specs=pl.BlockSpec((tm, d), row),
        out_shape=jax.ShapeDtypeStruct((t, d), F32),
        scratch_shapes=[pltpu.VMEM((2, TOP_K, tm * ROW_TILE, V7X_LANES), ys.dtype), pltpu.SemaphoreType.DMA((2,))],
        compiler_params=_params(("arbitrary",)),
        name="combine",
    )(dest_flat, dest_flat, topw, h1, ysh, ys, g2, b2)


def kernel(x, ln_emb_g, ln_emb_b, w_in, conv_w, conv_b, b_igate, b_fgate, mlstm_norm_g, lambda_q1, lambda_k1,
           lambda_q2, lambda_k2, rel_bias, w_out, ln1_g, ln1_b, w_router, router_bias, w_gate, w_up, w_down,
           ws_gate, ws_up, ws_down, ln2_g, ln2_b):
    bsz, s, d = x.shape
    assert bsz == 1 and w_in.shape[0] == DEPTH == 1
    x2 = x.reshape(s, d)
    row = lambda a: a.reshape(1, -1).astype(F32)

    w = w_in[0]
    c0 = 2 * M_QK + 2 * M_V
    c1 = c0 + 2 * MLSTM_HEADS
    c2 = c1 + D_QK
    c3 = c2 + D_QK
    w_main = jnp.concatenate([w[:, :c0], w[:, c1:c2], w[:, c3:]], axis=1).astype(BF16)
    w_k_t = w[:, c2:c3].T.astype(BF16)
    wi, wf = w[:, c0:c0 + MLSTM_HEADS], w[:, c0 + MLSTM_HEADS:c1]
    w_gates = jnp.zeros((d, MLSTM_HEADS, V7X_LANES), F32).at[:, :, 0].set(wi).at[:, :, 1].set(wf)
    w_gates = w_gates.reshape(d, MLSTM_HEADS * V7X_LANES).astype(BF16)
    w_gates_t = jnp.zeros((MLSTM_HEADS, V7X_SUBLANES, d), F32).at[:, 0].set(wi.T).at[:, 1].set(wf.T)
    w_gates_t = w_gates_t.reshape(MLSTM_HEADS * V7X_SUBLANES, d).astype(BF16)

    proj, gates, gates_t, k_t = _ln_proj(x2, row(ln_emb_g), row(ln_emb_b), w_main, w_gates, w_gates_t, w_k_t)

    hm = _mlstm(proj, gates, gates_t, conv_w[0], row(conv_b[0]), b_igate[0].astype(F32),
                b_fgate[0].astype(F32), row(mlstm_norm_g[0]))

    lam_vecs = jnp.stack([lambda_q1[0], lambda_k1[0], lambda_q2[0], lambda_k2[0]]).astype(F32)
    hd = _diff_attn(proj, k_t, lam_vecs, rel_bias.astype(F32).reshape(-1))

    wo = w_out[0].astype(BF16)
    h1, h1p, logits_t = _out_ln(hm, hd, x2, row(ln_emb_g), row(ln_emb_b), wo[:M_V], wo[M_V:], row(ln1_g[0]),
                           row(ln1_b[0]), w_router[0].T.astype(BF16))

    topi_t, pos_t, topw_t, counts = _route(logits_t, router_bias[0].reshape(-1, 1).astype(F32))
    off, vexp, vblk, vcnt, nvis, dest_t = _plan(counts.reshape(1, -1), counts, topi_t, pos_t)
    off_ext = jnp.concatenate([off.reshape(-1), jnp.full((1,), s * TOP_K, jnp.int32)])
    dest_flat = dest_t.T.reshape(-1)

    xs = _dispatch(dest_flat, h1p)
    ys, ysh = _experts(vexp.reshape(-1), vblk.reshape(-1), vcnt.reshape(-1), off_ext, nvis.reshape(-1), xs,
                       w_gate[0], w_up[0], w_down[0], h1p, ws_gate[0].astype(BF16), ws_up[0].astype(BF16),
                       ws_down[0].astype(BF16))
    out = _combine(dest_flat, topw_t.T, h1, ysh, ys, row(ln2_g[0]), row(ln2_b[0]))
    return out.reshape(bsz, s, d)
```

```python
import functools
import math

import jax
import jax.numpy as jnp
from jax import lax
from jax.experimental import pallas as pl
from jax.experimental.pallas import tpu as pltpu

F32 = jnp.float32
BF16 = jnp.bfloat16

DEPTH = 1
MLSTM_HEADS = 4
MLSTM_DQK = 128
MLSTM_DV = 256
CONV_WIDTH = 4
DIFF_HEADS = 8
DIFF_DH = 64
DIFF_DV = 2 * DIFF_DH
REL_BUCKETS = 32
REL_MAX_DIST = 128
N_EXPERTS = 256
TOP_K = 8
N_GROUPS = 8
TOPK_GROUPS = 4
D_EXPERT = 512
ROUTED_SCALE = 2.5
LN_EPS = 1e-5
ALPHA = (2 * DEPTH) ** 0.25
LAM_INIT = 0.8 - 0.6 * math.exp(-0.3 * 0)

M_QK = MLSTM_HEADS * MLSTM_DQK
M_V = MLSTM_HEADS * MLSTM_DV
D_QK = DIFF_HEADS * 2 * DIFF_DH
D_VW = DIFF_HEADS * DIFF_DV
PROJ_W = 2 * M_QK + 2 * M_V + D_QK + D_VW
D_MODEL = M_V + D_VW

V7X_LANES = 128
V7X_SUBLANES = 8
V7X_VMEM_LIMIT_BYTES = 56 * 1024 * 1024
ROW_TILE = D_MODEL // 2 // V7X_LANES

PROJ_TM = 1024
PROJ_TN = 512
MLSTM_L = 256
ATT_BK = 1024
ATT_BQ = 2048
ATT_STRIP = 256
ATT_HEAD_GROUP = 2
OUT_TM = 512
ROUTE_TM = 256
MOE_TR = 256
MOVE_TM = 256
SHARED_TM = 256
PLAN_TB = 512

NEG = -1e30
LOG2E = 1.4426950408889634


def _params(semantics):
    return pltpu.CompilerParams(dimension_semantics=semantics, vmem_limit_bytes=V7X_VMEM_LIMIT_BYTES)


def _layer_norm(x, g, b):
    mu = jnp.mean(x, -1, keepdims=True)
    xc = x - mu
    var = jnp.mean(xc * xc, -1, keepdims=True)
    return xc * lax.rsqrt(var + LN_EPS) * g + b


def _dot(a, b):
    return jnp.dot(a, b, preferred_element_type=F32)


def _dot_nt(a, b):
    return lax.dot_general(a, b, (((1,), (1,)), ((), ())), preferred_element_type=F32)


def _split3(a):
    a1 = a.astype(BF16)
    r1 = a - a1.astype(F32)
    a2 = r1.astype(BF16)
    a3 = (r1 - a2.astype(F32)).astype(BF16)
    return a1, a2, a3


def _pack_bf16_pairs(x):
    n = x.shape[1] // 2
    bits = lax.bitcast_convert_type(x.astype(BF16).astype(F32), jnp.uint32)
    return lax.shift_right_logical(bits[:, :n], jnp.uint32(16)) | (bits[:, n:] & jnp.uint32(0xFFFF0000))


def _unpack_bf16_pairs(u):
    lo = lax.bitcast_convert_type(lax.shift_left(u, jnp.uint32(16)), F32)
    hi = lax.bitcast_convert_type(u & jnp.uint32(0xFFFF0000), F32)
    return jnp.concatenate([lo, hi], axis=1)


def _store_row_tiles(ref, x, r0=0):
    m, n = x.shape[0], x.shape[1] // V7X_LANES
    for c in range(n):
        ref[pl.ds(r0 * n + c, m, stride=n), :] = x[:, c * V7X_LANES:(c + 1) * V7X_LANES]


def _load_row_tiles(ref, m, n, r0=0):
    return jnp.concatenate([ref[pl.ds(r0 * n + c, m, stride=n), :] for c in range(n)], axis=1)


def _log_sigmoid(x):
    return jnp.minimum(x, 0.0) - jnp.log(1.0 + jnp.exp(-jnp.abs(x)))


def _ln_proj_kernel(x_ref, g_ref, b_ref, w_ref, wg_ref, wgt_ref, wkt_ref, o_ref, gates_ref, gatest_ref, kt_ref,
                    hb_ref):
    @pl.when(pl.program_id(1) == 0)
    def _():
        hb = _layer_norm(x_ref[...], g_ref[...], b_ref[...]).astype(BF16)
        hb_ref[...] = hb
        gates_ref[...] = _dot(hb, wg_ref[...])
        gatest_ref[...] = _dot_nt(wgt_ref[...], hb)
        kt_ref[...] = _dot_nt(wkt_ref[...], hb).astype(kt_ref.dtype)

    o_ref[...] = _dot(hb_ref[...], w_ref[...]).astype(o_ref.dtype)


def _ln_proj(x2, g, b, w_main, w_gates, w_gates_t, w_k_t):
    s, d = x2.shape
    tm, tn = min(PROJ_TM, s), PROJ_TN
    ng = w_gates.shape[1]
    fixed = lambda i, j: (0, 0)
    return pl.pallas_call(
        _ln_proj_kernel,
        grid=(s // tm, PROJ_W // tn),
        in_specs=[
            pl.BlockSpec((tm, d), lambda i, j: (i, 0)),
            pl.BlockSpec((1, d), fixed),
            pl.BlockSpec((1, d), fixed),
            pl.BlockSpec((d, tn), lambda i, j: (0, j)),
            pl.BlockSpec((d, ng), fixed),
            pl.BlockSpec((w_gates_t.shape[0], d), fixed),
            pl.BlockSpec((D_QK, d), fixed),
        ],
        out_specs=[
            pl.BlockSpec((tm, tn), lambda i, j: (i, j)),
            pl.BlockSpec((tm, ng), lambda i, j: (i, 0)),
            pl.BlockSpec((w_gates_t.shape[0], tm), lambda i, j: (0, i)),
            pl.BlockSpec((D_QK, tm), lambda i, j: (0, i)),
        ],
        out_shape=[
            jax.ShapeDtypeStruct((s, PROJ_W), BF16),
            jax.ShapeDtypeStruct((s, ng), F32),
            jax.ShapeDtypeStruct((w_gates_t.shape[0], s), F32),
            jax.ShapeDtypeStruct((D_QK, s), BF16),
        ],
        scratch_shapes=[pltpu.VMEM((tm, d), BF16)],
        compiler_params=_params(("arbitrary", "arbitrary")),
        name="ln_proj",
    )(x2, g, b, w_main, w_gates, w_gates_t, w_k_t)


def _mlstm_kernel(bi_ref, bf_ref, mq_ref, mk_ref, mv_ref, mo_ref, gates_ref, gatest_ref,
                  cw_ref, cb_ref, ng_ref, o_ref, c_ref, n_ref, m_ref, tail_ref):
    L = mq_ref.shape[0]
    dk, dv = MLSTM_DQK, MLSTM_DV

    @pl.when(pl.program_id(0) == 0)
    def _():
        c_ref[...] = jnp.zeros_like(c_ref)
        n_ref[...] = jnp.zeros_like(n_ref)
        m_ref[...] = jnp.zeros_like(m_ref)
        tail_ref[...] = jnp.zeros_like(tail_ref)

    u = jnp.concatenate([mq_ref[...], mk_ref[...]], axis=1).astype(F32)
    tail = tail_ref[...]
    row8 = lax.broadcasted_iota(jnp.int32, (V7X_SUBLANES, u.shape[1]), 0)
    conv = cb_ref[...] + cw_ref[CONV_WIDTH - 1:CONV_WIDTH, :] * u
    for back in range(1, CONV_WIDTH):
        ur = pltpu.roll(u, back, 0)
        head = jnp.where(row8 < back, pltpu.roll(tail, back, 0), ur[:V7X_SUBLANES])
        shifted = jnp.concatenate([head, ur[V7X_SUBLANES:]], axis=0)
        conv = conv + cw_ref[CONV_WIDTH - 1 - back:CONV_WIDTH - back, :] * shifted
    tail_ref[...] = u[L - V7X_SUBLANES:]
    qk = conv * jax.nn.sigmoid(conv)
    q_all = qk[:, :M_QK]
    k_all = qk[:, M_QK:] * (dk ** -0.5)

    r_i = lax.broadcasted_iota(jnp.int32, (L, L), 0)
    c_i = lax.broadcasted_iota(jnp.int32, (L, L), 1)
    causal = c_i <= r_i
    tril = jnp.where(causal, 1.0, 0.0).astype(BF16)
    triu = jnp.where(r_i <= c_i, 1.0, 0.0).astype(BF16)

    for h in range(MLSTM_HEADS):
        q = q_all[:, h * dk:(h + 1) * dk]
        k = k_all[:, h * dk:(h + 1) * dk]
        qb, kb = q.astype(BF16), k.astype(BF16)
        vb = mv_ref[:, h * dv:(h + 1) * dv]
        b_i, b_f = bi_ref[h], bf_ref[h]

        gblk = gates_ref[:, h * V7X_LANES:(h + 1) * V7X_LANES]
        i_col = gblk[:, 0:1] + b_i
        lf_blk = _log_sigmoid(gblk + b_f)
        f1, f2, f3 = _split3(lf_blk)
        bcum_blk = _dot(tril, f1) + _dot(tril, f2) + _dot(tril, f3)
        b_col = bcum_blk[:, 1:2]
        gt = gatest_ref[h * V7X_SUBLANES:(h + 1) * V7X_SUBLANES, :]
        i_row = gt[0:1, :] + b_i
        lf_rows = _log_sigmoid(gt + b_f)
        g1, g2, g3 = _split3(lf_rows)
        b_row = (_dot(g1, triu) + _dot(g2, triu) + _dot(g3, triu))[1:2, :]

        m_prev = m_ref[h, 0:1, 0:1]
        dmat = jnp.where(causal, b_col - b_row + i_row, NEG)
        inter = b_col + m_prev
        m_t = jnp.maximum(inter, jnp.max(dmat, -1, keepdims=True))
        wts = jnp.exp(dmat - m_t)
        g = jnp.exp(inter - m_t)
        sqk = _dot_nt(qb, kb) * wts
        c_prev = c_ref[h]
        n_prev = n_ref[h, 0:1, :]
        num = g * _dot(qb, c_prev.astype(BF16)) + _dot(sqk.astype(BF16), vb)
        den = g * jnp.sum(q * n_prev, -1, keepdims=True) + jnp.sum(sqk, -1, keepdims=True)
        hh = num / jnp.maximum(jnp.abs(den), jnp.exp(-m_t))

        b_last = b_col[L - 1:L, :]
        w_last_row = b_last - b_row + i_row
        m_new = jnp.maximum(b_last + m_prev, jnp.max(w_last_row, -1, keepdims=True))
        decay = jnp.exp(b_last + m_prev - m_new)
        ws_col = jnp.exp(b_last - b_col + i_col - m_new)
        kw = k * ws_col
        c_ref[h] = decay * c_prev + _dot(kw.T.astype(BF16), vb)
        n_ref[h, 0:1, :] = decay * n_prev + jnp.sum(kw, 0, keepdims=True)
        m_ref[h] = jnp.broadcast_to(m_new, m_ref.shape[1:])

        mu = jnp.mean(hh, -1, keepdims=True)
        hc = hh - mu
        var = jnp.mean(hc * hc, -1, keepdims=True)
        hn = hc * lax.rsqrt(var + LN_EPS) * ng_ref[:, h * dv:(h + 1) * dv]
        og = jax.nn.sigmoid(mo_ref[:, h * dv:(h + 1) * dv].astype(F32))
        o_ref[:, h * dv:(h + 1) * dv] = (hn * og).astype(o_ref.dtype)


def _mlstm(proj, gates, gates_t, conv_w, conv_b, b_i, b_f, norm_g):
    s = proj.shape[0]
    L = min(MLSTM_L, s)
    smem = pl.BlockSpec(memory_space=pltpu.SMEM)
    return pl.pallas_call(
        _mlstm_kernel,
        grid=(s // L,),
        in_specs=[
            smem, smem,
            pl.BlockSpec((L, M_QK), lambda c: (c, 0)),
            pl.BlockSpec((L, M_QK), lambda c: (c, 1)),
            pl.BlockSpec((L, M_V), lambda c: (c, 1)),
            pl.BlockSpec((L, M_V), lambda c: (c, 2)),
            pl.BlockSpec((L, gates.shape[1]), lambda c: (c, 0)),
            pl.BlockSpec((gates_t.shape[0], L), lambda c: (0, c)),
            pl.BlockSpec((CONV_WIDTH, 2 * M_QK), lambda c: (0, 0)),
            pl.BlockSpec((1, 2 * M_QK), lambda c: (0, 0)),
            pl.BlockSpec((1, M_V), lambda c: (0, 0)),
        ],
        out_specs=pl.BlockSpec((L, M_V), lambda c: (c, 0)),
        out_shape=jax.ShapeDtypeStruct((s, M_V), BF16),
        scratch_shapes=[
            pltpu.VMEM((MLSTM_HEADS, MLSTM_DQK, MLSTM_DV), F32),
            pltpu.VMEM((MLSTM_HEADS, V7X_SUBLANES, MLSTM_DQK), F32),
            pltpu.VMEM((MLSTM_HEADS, V7X_SUBLANES, V7X_LANES), F32),
            pltpu.VMEM((V7X_SUBLANES, 2 * M_QK), F32),
        ],
        compiler_params=_params(("arbitrary",)),
        name="mlstm",
    )(b_i, b_f, proj, proj, proj, proj, gates, gates_t, conv_w, conv_b, norm_g)


def _t5_bias_tile(table_ref, h, offset):
    n_t = V7X_LANES
    r_i = lax.broadcasted_iota(jnp.int32, (n_t, n_t), 0)
    c_i = lax.broadcasted_iota(jnp.int32, (n_t, n_t), 1)
    n = jnp.maximum(offset + r_i - c_i, 0)
    max_exact = REL_BUCKETS // 2
    large = max_exact + (jnp.log(jnp.maximum(n, 1).astype(F32) / max_exact)
                         / math.log(REL_MAX_DIST / max_exact) * (REL_BUCKETS - max_exact)).astype(jnp.int32)
    large = jnp.minimum(large, REL_BUCKETS - 1)
    bucket = jnp.where(n < max_exact, n, large)
    out = jnp.zeros((n_t, n_t), F32)
    for b in range(REL_BUCKETS):
        out = jnp.where(bucket == b, table_ref[b * DIFF_HEADS + h], out)
    return out * LOG2E


def _diff_attn_kernel(qi_ref, kj_ref, q_ref, kt_ref, v_ref, lam_ref, table_ref, o_ref,
                      q1_ref, q2_ref, vx_ref, m_ref, acc_ref, bdiag_ref, bsub_ref):
    g = pl.program_id(0)
    p = pl.program_id(1)
    i = qi_ref[p]
    j = kj_ref[p]
    BQ = q_ref.shape[0]
    BK = v_ref.shape[0]
    n_hg = q_ref.shape[1] // V7X_LANES
    n_half = BQ // BK
    SR = min(ATT_STRIP, BK)
    strips_per_half = BK // SR
    n_sub = BK // V7X_LANES
    heads = [g * n_hg + u for u in range(n_hg)]
    c_far = [table_ref[(REL_BUCKETS - 1) * DIFF_HEADS + h] * LOG2E for h in heads]

    @pl.when(p == 0)
    def _():
        for u, h in enumerate(heads):
            p0 = _t5_bias_tile(table_ref, h, 0)
            p0 = jnp.where(lax.broadcasted_iota(jnp.int32, p0.shape, 1)
                           <= lax.broadcasted_iota(jnp.int32, p0.shape, 0), p0, NEG)
            p1 = _t5_bias_tile(table_ref, h, V7X_LANES)
            far = jnp.full((V7X_LANES, V7X_LANES), c_far[u], F32)
            neg = jnp.full((V7X_LANES, V7X_LANES), NEG, F32)
            for a in range(n_sub):
                for b in range(n_sub):
                    tile = p0 if a == b else p1 if a == b + 1 else far if a > b else neg
                    bdiag_ref[u, a * V7X_LANES:(a + 1) * V7X_LANES, b * V7X_LANES:(b + 1) * V7X_LANES] = tile
            bsub_ref[u] = jnp.full(bsub_ref.shape[1:], c_far[u], F32)
            bsub_ref[u, 0:V7X_LANES, BK - V7X_LANES:BK] = p1

    @pl.when(j == 0)
    def _():
        for u in range(n_hg):
            q = q_ref[:, u * V7X_LANES:(u + 1) * V7X_LANES].astype(F32) * ((DIFF_DH ** -0.5) * LOG2E)
            lane = lax.broadcasted_iota(jnp.int32, q.shape, 1)
            q1_ref[u] = jnp.where(lane < DIFF_DH, q, 0.0).astype(BF16)
            q2_ref[u] = jnp.where(lane >= DIFF_DH, q, 0.0).astype(BF16)
        m_ref[...] = jnp.full(m_ref.shape, NEG, F32)
        acc_ref[...] = jnp.zeros_like(acc_ref)

    for u in range(n_hg):
        vx_ref[u, :, 0:DIFF_DV] = v_ref[:, u * DIFF_DV:(u + 1) * DIFF_DV]
        vx_ref[u, :, DIFF_DV:] = jnp.ones((BK, DIFF_DV), BF16)

    def strip(u, r, kl, bias_tile=None):
        rows = slice(r * SR, (r + 1) * SR)
        kb = kt_ref[u * V7X_LANES:(u + 1) * V7X_LANES, 0:kl]
        vb = vx_ref[u, 0:kl, :]
        for a, qz_ref in enumerate((q1_ref, q2_ref)):
            st = 2 * u + a
            s = _dot(qz_ref[u, rows, :], kb)
            m_old = m_ref[st, rows, :]
            if bias_tile is None:
                m_new = jnp.maximum(m_old, jnp.max(s, -1, keepdims=True) + c_far[u])
                pm = jnp.exp2(s - (m_new - c_far[u]))
            else:
                s = s + bias_tile
                m_new = jnp.maximum(m_old, jnp.max(s, -1, keepdims=True))
                pm = jnp.exp2(s - m_new)
            alpha = jnp.exp2(m_old - m_new)
            acc_ref[st, rows, :] = alpha * acc_ref[st, rows, :] + _dot(pm.astype(BF16), vb)
            m_ref[st, rows, :] = m_new

    def key_block(delta):
        for r in range(n_half * strips_per_half):
            ro = (r % strips_per_half) * SR
            rel = None if delta is None else delta - r // strips_per_half
            for u in range(n_hg):
                if rel is None or rel <= -2:
                    strip(u, r, BK)
                elif rel == -1:
                    strip(u, r, BK, bsub_ref[u] if ro == 0 else None)
                elif rel == 0:
                    strip(u, r, ro + SR, bdiag_ref[u, ro:ro + SR, 0:ro + SR])

    d = j - i * n_half

    @pl.when(d <= -2)
    def _():
        key_block(None)

    for delta in range(-1, n_half):
        @pl.when(d == delta)
        def _(delta=delta):
            key_block(delta)
            if delta == n_half - 1:
                lq1, lk1, lq2, lk2 = (lam_ref[t:t + 1, :] for t in range(4))
                lam = (jnp.exp(jnp.sum(lq1 * lk1, -1, keepdims=True))
                       - jnp.exp(jnp.sum(lq2 * lk2, -1, keepdims=True)) + LAM_INIT)
                for u in range(n_hg):
                    o = (acc_ref[2 * u, :, 0:DIFF_DV] / acc_ref[2 * u, :, DIFF_DV:DIFF_DV + 1]
                         - lam * (acc_ref[2 * u + 1, :, 0:DIFF_DV] / acc_ref[2 * u + 1, :, DIFF_DV:DIFF_DV + 1]))
                    o = o * lax.rsqrt(jnp.mean(o * o, -1, keepdims=True) + LN_EPS)
                    o_ref[:, u * DIFF_DV:(u + 1) * DIFF_DV] = (o * (1.0 - LAM_INIT)).astype(o_ref.dtype)


def _diff_attn(proj, k_t, lam_vecs, table_flat):
    s = proj.shape[0]
    BK = min(ATT_BK, s)
    BQ = min(ATT_BQ, s)
    n_half = BQ // BK
    hg = ATT_HEAD_GROUP
    w = hg * V7X_LANES
    pairs = [(i, j) for i in range(s // BQ) for j in range((i + 1) * n_half)]
    qi = jnp.asarray([a for a, _ in pairs], jnp.int32)
    kj = jnp.asarray([b for _, b in pairs], jnp.int32)
    q_blk = (2 * M_QK + 2 * M_V) // w
    v_blk = q_blk + D_QK // w
    SR = min(ATT_STRIP, BK)
    grid_spec = pltpu.PrefetchScalarGridSpec(
        num_scalar_prefetch=2,
        grid=(DIFF_HEADS // hg, len(pairs)),
        in_specs=[
            pl.BlockSpec((BQ, w), lambda g, p, qi, kj: (qi[p], q_blk + g)),
            pl.BlockSpec((w, BK), lambda g, p, qi, kj: (g, kj[p])),
            pl.BlockSpec((BK, w), lambda g, p, qi, kj: (kj[p], v_blk + g)),
            pl.BlockSpec((4, DIFF_DH), lambda g, p, qi, kj: (0, 0)),
            pl.BlockSpec(memory_space=pltpu.SMEM),
        ],
        out_specs=pl.BlockSpec((BQ, w), lambda g, p, qi, kj: (qi[p], g)),
        scratch_shapes=[
            pltpu.VMEM((hg, BQ, V7X_LANES), BF16),
            pltpu.VMEM((hg, BQ, V7X_LANES), BF16),
            pltpu.VMEM((hg, BK, 2 * DIFF_DV), BF16),
            pltpu.VMEM((2 * hg, BQ, 1), F32),
            pltpu.VMEM((2 * hg, BQ, 2 * DIFF_DV), F32),
            pltpu.VMEM((hg, BK, BK), F32),
            pltpu.VMEM((hg, SR, BK), F32),
        ],
    )
    return pl.pallas_call(
        _diff_attn_kernel,
        grid_spec=grid_spec,
        out_shape=jax.ShapeDtypeStruct((s, D_VW), BF16),
        compiler_params=_params(("arbitrary", "arbitrary")),
        name="diff_attn",
    )(qi, kj, proj, k_t, proj, lam_vecs, table_flat)


def _out_ln_kernel(hm_ref, hd_ref, x_ref, g0_ref, b0_ref, wo1_ref, wo2_ref, g1_ref, b1_ref, wr_ref,
                   h1_ref, h1p_ref, logit_ref):
    mix = _dot(hm_ref[...], wo1_ref[...]) + _dot(hd_ref[...], wo2_ref[...])
    h0 = _layer_norm(x_ref[...], g0_ref[...], b0_ref[...])
    h1 = _layer_norm(ALPHA * h0 + mix, g1_ref[...], b1_ref[...])
    h1_ref[...] = h1
    _store_row_tiles(h1p_ref, _pack_bf16_pairs(h1))
    logit_ref[...] = _dot_nt(wr_ref[...], h1.astype(BF16))


def _out_ln(hm, hd, x2, g0, b0, wo1, wo2, g1, b1, wr):
    s, d = x2.shape
    tm = min(OUT_TM, s)
    row = lambda i: (i, 0)
    fixed = lambda i: (0, 0)
    return pl.pallas_call(
        _out_ln_kernel,
        grid=(s // tm,),
        in_specs=[
            pl.BlockSpec((tm, M_V), row), pl.BlockSpec((tm, D_VW), row), pl.BlockSpec((tm, d), row),
            pl.BlockSpec((1, d), fixed), pl.BlockSpec((1, d), fixed),
            pl.BlockSpec((M_V, d), fixed), pl.BlockSpec((D_VW, d), fixed),
            pl.BlockSpec((1, d), fixed), pl.BlockSpec((1, d), fixed),
            pl.BlockSpec((N_EXPERTS, d), fixed),
        ],
        out_specs=[pl.BlockSpec((tm, d), row), pl.BlockSpec((tm * ROW_TILE, V7X_LANES), row),
                   pl.BlockSpec((N_EXPERTS, tm), lambda i: (0, i))],
        out_shape=[jax.ShapeDtypeStruct((s, d), F32), jax.ShapeDtypeStruct((s * ROW_TILE, V7X_LANES), jnp.uint32),
                   jax.ShapeDtypeStruct((N_EXPERTS, s), F32)],
        compiler_params=_params(("arbitrary",)),
        name="out_ln",
    )(hm, hd, x2, g0, b0, wo1, wo2, g1, b1, wr)


def _route_kernel(logit_ref, rb_ref, topi_ref, pos_ref, topw_ref, cnt_ref, carry_ref):
    n_e, tm = logit_ref.shape
    gsz = n_e // N_GROUPS
    ninf = -jnp.inf

    @pl.when(pl.program_id(0) == 0)
    def _():
        carry_ref[...] = jnp.zeros_like(carry_ref)

    sc = jax.nn.sigmoid(logit_ref[...])
    sel = sc + rb_ref[...]

    sel3 = sel.reshape(N_GROUPS, gsz, tm)
    in_grp = lax.broadcasted_iota(jnp.int32, sel3.shape, 1)
    m1 = jnp.max(sel3, 1, keepdims=True)
    i1 = jnp.min(jnp.where(sel3 == m1, in_grp, gsz), 1, keepdims=True)
    m2 = jnp.max(jnp.where(in_grp == i1, ninf, sel3), 1, keepdims=True)
    gscore = (m1 + m2).reshape(N_GROUPS, tm)
    gid = lax.broadcasted_iota(jnp.int32, gscore.shape, 0)
    beaten = jnp.zeros(gscore.shape, F32)
    for o in range(1, N_GROUPS):
        other = pltpu.roll(gscore, o, 0)
        wins = (other > gscore) | ((other == gscore) & (gid >= o))
        beaten = beaten + jnp.where(wins, 1.0, 0.0)
    keep = jnp.where(beaten < TOPK_GROUPS, 1.0, 0.0)
    keep3 = jnp.broadcast_to(keep.reshape(N_GROUPS, 1, tm), sel3.shape)
    masked = jnp.where(keep3 > 0.5, sel3, ninf).reshape(n_e, tm)

    eid = lax.broadcasted_iota(jnp.int32, (n_e, tm), 0)
    onehots, idxs, ws = [], [], []
    for _ in range(TOP_K):
        m = jnp.max(masked, 0, keepdims=True)
        idx = jnp.min(jnp.where(masked == m, eid, n_e), 0, keepdims=True)
        oh = eid == idx
        onehots.append(oh)
        idxs.append(idx)
        ws.append(jnp.sum(jnp.where(oh, sc, 0.0), 0, keepdims=True))
        masked = jnp.where(oh, ninf, masked)
    wsum = ws[0]
    for w in ws[1:]:
        wsum = wsum + w

    chosen = jnp.zeros((n_e, tm), F32)
    for oh in onehots:
        chosen = jnp.where(oh, 1.0, chosen)
    r_i = lax.broadcasted_iota(jnp.int32, (tm, tm), 0)
    c_i = lax.broadcasted_iota(jnp.int32, (tm, tm), 1)
    before = jnp.where(r_i < c_i, 1.0, 0.0).astype(BF16)
    rank = _dot(chosen.astype(BF16), before) + carry_ref[...]
    carry_new = carry_ref[...] + jnp.sum(chosen, 1, keepdims=True)
    carry_ref[...] = carry_new
    cnt_ref[...] = carry_new

    kid = lax.broadcasted_iota(jnp.int32, (TOP_K, tm), 0)
    topi = jnp.zeros((TOP_K, tm), jnp.int32)
    pos = jnp.zeros((TOP_K, tm), jnp.int32)
    topw = jnp.zeros((TOP_K, tm), F32)
    for k in range(TOP_K):
        pk = jnp.sum(jnp.where(onehots[k], rank, 0.0), 0, keepdims=True)
        topi = jnp.where(kid == k, idxs[k], topi)
        pos = jnp.where(kid == k, pk.astype(jnp.int32), pos)
        topw = jnp.where(kid == k, ws[k] / wsum * ROUTED_SCALE, topw)
    topi_ref[...] = topi
    pos_ref[...] = pos
    topw_ref[...] = topw


def _route(logits_t, router_bias_col):
    n_e, t = logits_t.shape
    tm = min(ROUTE_TM, t)
    col = lambda i: (0, i)
    fixed = lambda i: (0, 0)
    return pl.pallas_call(
        _route_kernel,
        grid=(t // tm,),
        in_specs=[pl.BlockSpec((n_e, tm), col), pl.BlockSpec((n_e, 1), fixed)],
        out_specs=[pl.BlockSpec((TOP_K, tm), col), pl.BlockSpec((TOP_K, tm), col),
                   pl.BlockSpec((TOP_K, tm), col), pl.BlockSpec((n_e, 1), fixed)],
        out_shape=[jax.ShapeDtypeStruct((TOP_K, t), jnp.int32), jax.ShapeDtypeStruct((TOP_K, t), jnp.int32),
                   jax.ShapeDtypeStruct((TOP_K, t), F32), jax.ShapeDtypeStruct((n_e, 1), F32)],
        scratch_shapes=[pltpu.VMEM((n_e, 1), F32)],
        compiler_params=_params(("arbitrary",)),
        name="route",
    )(logits_t, router_bias_col)


def _plan_kernel(cnt_ref, cntc_ref, topi_ref, pos_ref, off_ref, vexp_ref, vblk_ref, vcnt_ref, nvis_ref, dest_ref):
    nv = vexp_ref.shape[0]
    tr = float(MOE_TR)
    cnt = jnp.broadcast_to(cnt_ref[...], (V7X_SUBLANES, N_EXPERTS))
    r_i = lax.broadcasted_iota(jnp.int32, (N_EXPERTS, N_EXPERTS), 0)
    c_i = lax.broadcasted_iota(jnp.int32, (N_EXPERTS, N_EXPERTS), 1)
    upper = jnp.where(r_i <= c_i, 1.0, 0.0).astype(BF16)

    def cumsum_lanes(a):
        a1, a2, a3 = _split3(a)
        return _dot(a1, upper) + _dot(a2, upper) + _dot(a3, upper)

    end = cumsum_lanes(cnt)
    start = end - cnt
    first_blk = jnp.floor(start / tr)
    last_blk = jnp.floor((end - 1.0) / tr)
    nvis_e = jnp.where(cnt > 0.0, last_blk - first_blk + 1.0, 0.0)
    vend = cumsum_lanes(nvis_e)
    vstart = vend - nvis_e
    off_ref[...] = start[0:1, :].astype(jnp.int32)
    nvis_ref[...] = vend[0:1, N_EXPERTS - 1:N_EXPERTS].astype(jnp.int32)

    v = lax.broadcasted_iota(jnp.int32, (nv, N_EXPERTS), 0).astype(F32)
    ve = jnp.broadcast_to(vend[0:1, :], (nv, N_EXPERTS))
    expert = jnp.sum(jnp.where(ve <= v, 1.0, 0.0), -1, keepdims=True)
    expert = jnp.minimum(expert, N_EXPERTS - 1.0)
    lane = lax.broadcasted_iota(jnp.int32, (nv, N_EXPERTS), 1).astype(F32)
    mine = lane == expert
    fb = jnp.sum(jnp.where(mine, jnp.broadcast_to(first_blk[0:1, :], (nv, N_EXPERTS)), 0.0), -1, keepdims=True)
    vs = jnp.sum(jnp.where(mine, jnp.broadcast_to(vstart[0:1, :], (nv, N_EXPERTS)), 0.0), -1, keepdims=True)
    vc = jnp.sum(jnp.where(mine, jnp.broadcast_to(nvis_e[0:1, :], (nv, N_EXPERTS)), 0.0), -1, keepdims=True)
    vexp_ref[...] = expert.astype(jnp.int32)
    vblk_ref[...] = (fb + (v[:, 0:1] - vs)).astype(jnp.int32)
    vcnt_ref[...] = vc.astype(jnp.int32)

    lower = jnp.where(c_i < r_i, 1.0, 0.0).astype(BF16)
    c1, c2, c3 = _split3(jnp.broadcast_to(cntc_ref[...], (N_EXPERTS, V7X_LANES)))
    start_col = (_dot(lower, c1) + _dot(lower, c2) + _dot(lower, c3))[:, 0:1]
    n_tok = topi_ref.shape[1]
    tb = min(PLAN_TB, n_tok)
    eid = lax.broadcasted_iota(jnp.int32, (N_EXPERTS, tb), 0)
    kid = lax.broadcasted_iota(jnp.int32, (TOP_K, tb), 0)

    def dest_block(i, carry):
        cols = pl.ds(pl.multiple_of(i * tb, tb), tb)
        ti = topi_ref[:, cols]
        first_row = jnp.zeros((TOP_K, tb), F32)
        for k in range(TOP_K):
            fr = jnp.sum(jnp.where(eid == ti[k:k + 1, :], start_col, 0.0), 0, keepdims=True)
            first_row = jnp.where(kid == k, fr, first_row)
        dest_ref[:, cols] = first_row.astype(jnp.int32) + pos_ref[:, cols]
        return carry

    lax.fori_loop(0, n_tok // tb, dest_block, 0)


def _plan(counts_row, counts_col, topi_t, pos_t):
    n_rows = topi_t.shape[0] * topi_t.shape[1]
    nv = n_rows // MOE_TR + N_EXPERTS
    return pl.pallas_call(
        _plan_kernel,
        out_shape=[jax.ShapeDtypeStruct((1, N_EXPERTS), jnp.int32), jax.ShapeDtypeStruct((nv, 1), jnp.int32),
                   jax.ShapeDtypeStruct((nv, 1), jnp.int32), jax.ShapeDtypeStruct((nv, 1), jnp.int32),
                   jax.ShapeDtypeStruct((1, 1), jnp.int32), jax.ShapeDtypeStruct(topi_t.shape, jnp.int32)],
        compiler_params=pltpu.CompilerParams(vmem_limit_bytes=V7X_VMEM_LIMIT_BYTES),
        name="plan",
    )(counts_row, counts_col, topi_t, pos_t)


def _row_tile(ref, r):
    return ref.at[pl.ds(pl.multiple_of(r * ROW_TILE, ROW_TILE), ROW_TILE)]


def _dispatch_kernel(dest_ref, x_ref, xs_ref, sem):
    tm = x_ref.shape[0] // ROW_TILE

    def issue(t, carry):
        for k in range(TOP_K):
            d = dest_ref[t * TOP_K + k]
            pltpu.make_async_copy(_row_tile(x_ref, t), _row_tile(xs_ref, d), sem).start(priority=k % 2)
        return carry

    lax.fori_loop(0, tm, issue, 0)
    for _ in range(TOP_K):
        pltpu.make_async_copy(x_ref, xs_ref.at[pl.ds(0, tm * ROW_TILE)], sem).wait()


def _dispatch(dest_flat, rows):
    t = rows.shape[0] // ROW_TILE
    tm = min(MOVE_TM, t)
    return pl.pallas_call(
        _dispatch_kernel,
        grid=(t // tm,),
        in_specs=[
            pl.BlockSpec((tm * TOP_K,), lambda i: (i,), memory_space=pltpu.SMEM),
            pl.BlockSpec((tm * ROW_TILE, V7X_LANES), lambda i: (i, 0)),
        ],
        out_specs=pl.BlockSpec(memory_space=pl.ANY),
        out_shape=jax.ShapeDtypeStruct((t * TOP_K * ROW_TILE, V7X_LANES), rows.dtype),
        scratch_shapes=[pltpu.SemaphoreType.DMA(())],
        compiler_params=_params(("arbitrary",)),
        name="dispatch",
    )(dest_flat, rows)


def _experts_kernel(n_shared_steps, vexp_ref, vblk_ref, vcnt_ref, off_ref, nvis_ref, xs_ref, wg_hbm, wu_hbm,
                    wd_hbm, hp_ref, sg_ref, su_ref, sd_ref, ys_ref, ysh_ref,
                    wg_buf, wu_buf, wd_buf, acc_ref, slot_ref, sem):
    v = pl.program_id(0)
    tr = xs_ref.shape[0] // ROW_TILE
    nvis = nvis_ref[0]

    @pl.when(v < n_shared_steps)
    def _():
        hb = _unpack_bf16_pairs(_load_row_tiles(hp_ref, hp_ref.shape[0] // ROW_TILE, ROW_TILE)).astype(BF16)
        gate = _dot(hb, sg_ref[...])
        up = _dot(hb, su_ref[...])
        ysh_ref[...] = _dot((gate * jax.nn.sigmoid(gate) * up).astype(BF16), sd_ref[...])

    half = wd_buf.shape[1] // 2

    def weight_copies(e, slot):
        return ((pltpu.make_async_copy(wg_hbm.at[e], wg_buf.at[slot], sem.at[slot, 0]), 1),
                (pltpu.make_async_copy(wu_hbm.at[e], wu_buf.at[slot], sem.at[slot, 1]), 1),
                (pltpu.make_async_copy(wd_hbm.at[e, pl.ds(0, half)], wd_buf.at[slot, pl.ds(0, half)],
                                       sem.at[slot, 2]), 1),
                (pltpu.make_async_copy(wd_hbm.at[e, pl.ds(half, half)], wd_buf.at[slot, pl.ds(half, half)],
                                       sem.at[slot, 3]), 1))

    @pl.when(v < nvis)
    def _():
        e = vexp_ref[v]
        blk = vblk_ref[v]
        first_of_expert = jnp.logical_or(v == 0, vexp_ref[jnp.maximum(v - 1, 0)] != e)

        @pl.when(v == 0)
        def _():
            slot_ref[0] = 0
            for c, queue in weight_copies(e, 0):
                c.start(priority=queue)

        @pl.when(jnp.logical_and(first_of_expert, v > 0))
        def _():
            slot_ref[0] = 1 - slot_ref[0]

        slot = slot_ref[0]

        @pl.when(first_of_expert)
        def _():
            nxt = v + vcnt_ref[v]

            @pl.when(nxt < nvis)
            def _():
                for c, queue in weight_copies(vexp_ref[jnp.minimum(nxt, vexp_ref.shape[0] - 1)], 1 - slot):
                    c.start(priority=queue)

            for c, _ in weight_copies(e, slot):
                c.wait()

        lo = off_ref[e] - blk * tr
        hi = off_ref[e + 1] - blk * tr
        first_of_block = jnp.logical_or(v == 0, vblk_ref[jnp.maximum(v - 1, 0)] != blk)

        @pl.when(first_of_block)
        def _():
            acc_ref[...] = jnp.zeros_like(acc_ref)

        def swiglu_rows(r0, n):
            row = r0 + lax.broadcasted_iota(jnp.int32, (n, 1), 0)
            mine = (row >= lo) & (row < hi)
            xb = _unpack_bf16_pairs(_load_row_tiles(xs_ref, n, ROW_TILE, r0)).astype(BF16)
            gate = _dot(xb, wg_buf[slot].astype(BF16))
            up = _dot(xb, wu_buf[slot].astype(BF16))
            act = (gate * jax.nn.sigmoid(gate) * up).astype(BF16)
            acc_ref[r0:r0 + n, :] += jnp.where(mine, _dot(act, wd_buf[slot].astype(BF16)), 0.0)

        hr = tr // 2
        top_only = hi <= hr
        bottom_only = lo >= hr

        @pl.when(top_only)
        def _():
            swiglu_rows(0, hr)

        @pl.when(bottom_only)
        def _():
            swiglu_rows(hr, hr)

        @pl.when(jnp.logical_not(jnp.logical_or(top_only, bottom_only)))
        def _():
            swiglu_rows(0, tr)

        last_of_block = jnp.logical_or(v + 1 >= nvis, vblk_ref[jnp.minimum(v + 1, vblk_ref.shape[0] - 1)] != blk)

        @pl.when(last_of_block)
        def _():
            _store_row_tiles(ys_ref, _pack_bf16_pairs(acc_ref[...]))


def _experts(vexp, vblk, vcnt, off_ext, nvis, xs, w_gate, w_up, w_down, h1p, sg, su, sd):
    n_rows = xs.shape[0] // ROW_TILE
    d = D_MODEL
    nv = vexp.shape[0]
    t = h1p.shape[0] // ROW_TILE
    tm = min(SHARED_TM, t)
    n_shared_steps = t // tm
    assert n_shared_steps <= n_rows // MOE_TR

    def block_of(v, ve, vb, vc, off, nvis):
        return (vb[jnp.minimum(v, nvis[0] - 1)], 0)

    def shared_block(v, ve, vb, vc, off, nvis):
        return (jnp.minimum(v, n_shared_steps - 1), 0)

    fixed = lambda v, ve, vb, vc, off, nvis: (0, 0)
    resident = dict(pipeline_mode=pl.Buffered(1))
    grid_spec = pltpu.PrefetchScalarGridSpec(
        num_scalar_prefetch=5,
        grid=(nv,),
        in_specs=[
            pl.BlockSpec((MOE_TR * ROW_TILE, V7X_LANES), block_of),
            pl.BlockSpec(memory_space=pl.ANY),
            pl.BlockSpec(memory_space=pl.ANY),
            pl.BlockSpec(memory_space=pl.ANY),
            pl.BlockSpec((tm * ROW_TILE, V7X_LANES), shared_block),
            pl.BlockSpec((d, D_EXPERT), fixed, **resident),
            pl.BlockSpec((d, D_EXPERT), fixed, **resident),
            pl.BlockSpec((D_EXPERT, d), fixed, **resident),
        ],
        out_specs=[pl.BlockSpec((MOE_TR * ROW_TILE, V7X_LANES), block_of), pl.BlockSpec((tm, d), shared_block)],
        scratch_shapes=[
            pltpu.VMEM((2, d, D_EXPERT), F32),
            pltpu.VMEM((2, d, D_EXPERT), F32),
            pltpu.VMEM((2, D_EXPERT, d), F32),
            pltpu.VMEM((MOE_TR, d), F32),
            pltpu.SMEM((1,), jnp.int32),
            pltpu.SemaphoreType.DMA((2, 4)),
        ],
    )
    return pl.pallas_call(
        functools.partial(_experts_kernel, n_shared_steps),
        grid_spec=grid_spec,
        out_shape=[jax.ShapeDtypeStruct(xs.shape, jnp.uint32), jax.ShapeDtypeStruct((t, d), F32)],
        compiler_params=_params(("arbitrary",)),
        name="experts",
    )(vexp, vblk, vcnt, off_ext, nvis, xs, w_gate, w_up, w_down, h1p, sg, su, sd)


def _combine_kernel(dest_ref, dest_next_ref, topw_ref, h1_ref, ysh_ref, ys_ref, g2_ref, b2_ref, o_ref,
                    buf_ref, sem):
    i = pl.program_id(0)
    tm = h1_ref.shape[0]
    slot = lax.rem(i, 2)

    def issue_block(d_ref, s):
        def issue(t, carry):
            for k in range(TOP_K):
                d = d_ref[t * TOP_K + k]
                pltpu.make_async_copy(_row_tile(ys_ref, d), _row_tile(buf_ref.at[s, k], t),
                                      sem.at[s]).start(priority=k % 2)
            return carry

        lax.fori_loop(0, tm, issue, 0)

    @pl.when(i == 0)
    def _():
        issue_block(dest_ref, 0)

    @pl.when(i + 1 < pl.num_programs(0))
    def _():
        issue_block(dest_next_ref, 1 - slot)

    y = ALPHA * h1_ref[...] + ysh_ref[...]
    for k in range(TOP_K):
        pltpu.make_async_copy(ys_ref.at[pl.ds(0, tm * ROW_TILE)], buf_ref.at[slot, k], sem.at[slot]).wait()
    w = topw_ref[...]
    for k in range(TOP_K):
        y = y + _unpack_bf16_pairs(_load_row_tiles(buf_ref.at[slot, k], tm, ROW_TILE)) * w[:, k:k + 1]
    o_ref[...] = _layer_norm(y, g2_ref[...], b2_ref[...])


def _combine(dest_flat, topw, h1, ysh, ys, g2, b2):
    t, d = h1.shape
    tm = min(MOVE_TM, t)
    nb = t // tm
    row = lambda i: (i, 0)
    fixed = lambda i: (0, 0)
    return pl.pallas_call(
        _combine_kernel,
        grid=(nb,),
        in_specs=[
            pl.BlockSpec((tm * TOP_K,), lambda i: (i,), memory_space=pltpu.SMEM),
            pl.BlockSpec((tm * TOP_K,), lambda i: (jnp.minimum(i + 1, nb - 1),), memory_space=pltpu.SMEM),
            pl.BlockSpec((tm, TOP_K), row),
            pl.BlockSpec((tm, d), row),
            pl.BlockSpec((tm, d), row),
            pl.BlockSpec(memory_space=pl.ANY),
            pl.BlockSpec((1, d), fixed), pl.BlockSpec((1, d), fixed),
        ],
        out_specs=pl.BlockSpec((tm, d), row),
        out_shape=jax.ShapeDtypeStruct((t, d), F32),
        scratch_shapes=[pltpu.VMEM((2, TOP_K, tm * ROW_TILE, V7X_LANES), ys.dtype), pltpu.SemaphoreType.DMA((2,))],
        compiler_params=_params(("arbitrary",)),
        name="combine",
    )(dest_flat, dest_flat, topw, h1, ysh, ys, g2, b2)


def kernel(x, ln_emb_g, ln_emb_b, w_in, conv_w, conv_b, b_igate, b_fgate, mlstm_norm_g, lambda_q1, lambda_k1,
           lambda_q2, lambda_k2, rel_bias, w_out, ln1_g, ln1_b, w_router, router_bias, w_gate, w_up, w_down,
           ws_gate, ws_up, ws_down, ln2_g, ln2_b):
    bsz, s, d = x.shape
    assert bsz == 1 and w_in.shape[0] == DEPTH == 1
    x2 = x.reshape(s, d)
    row = lambda a: a.reshape(1, -1).astype(F32)

    w = w_in[0]
    c0 = 2 * M_QK + 2 * M_V
    c1 = c0 + 2 * MLSTM_HEADS
    c2 = c1 + D_QK
    c3 = c2 + D_QK
    w_main = jnp.concatenate([w[:, :c0], w[:, c1:c2], w[:, c3:]], axis=1).astype(BF16)
    w_k_t = w[:, c2:c3].T.astype(BF16)
    wi, wf = w[:, c0:c0 + MLSTM_HEADS], w[:, c0 + MLSTM_HEADS:c1]
    w_gates = jnp.zeros((d, MLSTM_HEADS, V7X_LANES), F32).at[:, :, 0].set(wi).at[:, :, 1].set(wf)
    w_gates = w_gates.reshape(d, MLSTM_HEADS * V7X_LANES).astype(BF16)
    w_gates_t = jnp.zeros((MLSTM_HEADS, V7X_SUBLANES, d), F32).at[:, 0].set(wi.T).at[:, 1].set(wf.T)
    w_gates_t = w_gates_t.reshape(MLSTM_HEADS * V7X_SUBLANES, d).astype(BF16)

    proj, gates, gates_t, k_t = _ln_proj(x2, row(ln_emb_g), row(ln_emb_b), w_main, w_gates, w_gates_t, w_k_t)

    hm = _mlstm(proj, gates, gates_t, conv_w[0], row(conv_b[0]), b_igate[0].astype(F32),
                b_fgate[0].astype(F32), row(mlstm_norm_g[0]))

    lam_vecs = jnp.stack([lambda_q1[0], lambda_k1[0], lambda_q2[0], lambda_k2[0]]).astype(F32)
    hd = _diff_attn(proj, k_t, lam_vecs, rel_bias.astype(F32).reshape(-1))

    wo = w_out[0].astype(BF16)
    h1, h1p, logits_t = _out_ln(hm, hd, x2, row(ln_emb_g), row(ln_emb_b), wo[:M_V], wo[M_V:], row(ln1_g[0]),
                           row(ln1_b[0]), w_router[0].T.astype(BF16))

    topi_t, pos_t, topw_t, counts = _route(logits_t, router_bias[0].reshape(-1, 1).astype(F32))
    off, vexp, vblk, vcnt, nvis, dest_t = _plan(counts.reshape(1, -1), counts, topi_t, pos_t)
    off_ext = jnp.concatenate([off.reshape(-1), jnp.full((1,), s * TOP_K, jnp.int32)])
    dest_flat = dest_t.T.reshape(-1)

    xs = _dispatch(dest_flat, h1p)
    ys, ysh = _experts(vexp.reshape(-1), vblk.reshape(-1), vcnt.reshape(-1), off_ext, nvis.reshape(-1), xs,
                       w_gate[0], w_up[0], w_down[0], h1p, ws_gate[0].astype(BF16), ws_up[0].astype(BF16),
                       ws_down[0].astype(BF16))
    out = _combine(dest_flat, topw_t.T, h1, ysh, ys, row(ln2_g[0]), row(ln2_b[0]))
    return out.reshape(bsz, s, d)
```

```python
import functools
import math

import jax
import jax.numpy as jnp
from jax import lax
from jax.experimental import pallas as pl
from jax.experimental.pallas import tpu as pltpu

F32 = jnp.float32
BF16 = jnp.bfloat16

DEPTH = 1
MLSTM_HEADS = 4
MLSTM_DQK = 128
MLSTM_DV = 256
CONV_WIDTH = 4
DIFF_HEADS = 8
DIFF_DH = 64
DIFF_DV = 2 * DIFF_DH
REL_BUCKETS = 32
REL_MAX_DIST = 128
N_EXPERTS = 256
TOP_K = 8
N_GROUPS = 8
TOPK_GROUPS = 4
D_EXPERT = 512
ROUTED_SCALE = 2.5
LN_EPS = 1e-5
ALPHA = (2 * DEPTH) ** 0.25
LAM_INIT = 0.8 - 0.6 * math.exp(-0.3 * 0)

M_QK = MLSTM_HEADS * MLSTM_DQK
M_V = MLSTM_HEADS * MLSTM_DV
D_QK = DIFF_HEADS * 2 * DIFF_DH
D_VW = DIFF_HEADS * DIFF_DV
PROJ_W = 2 * M_QK + 2 * M_V + D_QK + D_VW
D_MODEL = M_V + D_VW

V7X_LANES = 128
V7X_SUBLANES = 8
V7X_VMEM_LIMIT_BYTES = 56 * 1024 * 1024
ROW_TILE = D_MODEL // 2 // V7X_LANES

PROJ_TM = 1024
PROJ_TN = 512
MLSTM_L = 256
ATT_BK = 1024
ATT_BQ = 2048
ATT_STRIP = 256
OUT_TM = 512
ROUTE_TM = 512
MOE_TR = 256
DISPATCH_TM = 512
MOVE_TM = 256
SHARED_TM = 256
PLAN_TB = 512

NEG = -1e30
LOG2E = 1.4426950408889634


def _params(semantics):
    return pltpu.CompilerParams(dimension_semantics=semantics, vmem_limit_bytes=V7X_VMEM_LIMIT_BYTES)


def _layer_norm(x, g, b):
    mu = jnp.mean(x, -1, keepdims=True)
    xc = x - mu
    var = jnp.mean(xc * xc, -1, keepdims=True)
    return xc * lax.rsqrt(var + LN_EPS) * g + b


def _dot(a, b):
    return jnp.dot(a, b, preferred_element_type=F32)


def _dot_nt(a, b):
    return lax.dot_general(a, b, (((1,), (1,)), ((), ())), preferred_element_type=F32)


def _split3(a):
    a1 = a.astype(BF16)
    r1 = a - a1.astype(F32)
    a2 = r1.astype(BF16)
    a3 = (r1 - a2.astype(F32)).astype(BF16)
    return a1, a2, a3


def _pack_bf16_pairs(x):
    n = x.shape[1] // 2
    bits = lax.bitcast_convert_type(x.astype(BF16).astype(F32), jnp.uint32)
    return lax.shift_right_logical(bits[:, :n], jnp.uint32(16)) | (bits[:, n:] & jnp.uint32(0xFFFF0000))


def _unpack_bf16_pairs(u):
    lo = lax.bitcast_convert_type(lax.shift_left(u, jnp.uint32(16)), F32)
    hi = lax.bitcast_convert_type(u & jnp.uint32(0xFFFF0000), F32)
    return jnp.concatenate([lo, hi], axis=1)


def _store_row_tiles(ref, x, r0=0):
    m, n = x.shape[0], x.shape[1] // V7X_LANES
    for c in range(n):
        ref[pl.ds(r0 * n + c, m, stride=n), :] = x[:, c * V7X_LANES:(c + 1) * V7X_LANES]


def _load_row_tiles(ref, m, n, r0=0):
    return jnp.concatenate([ref[pl.ds(r0 * n + c, m, stride=n), :] for c in range(n)], axis=1)


def _log_sigmoid(x):
    return jnp.minimum(x, 0.0) - jnp.log(1.0 + jnp.exp(-jnp.abs(x)))


def _ln_proj_kernel(x_ref, g_ref, b_ref, w_ref, wg_ref, wgt_ref, wkt_ref, o_ref, gates_ref, gatest_ref, kt_ref,
                    hb_ref):
    @pl.when(pl.program_id(1) == 0)
    def _():
        hb = _layer_norm(x_ref[...], g_ref[...], b_ref[...]).astype(BF16)
        hb_ref[...] = hb
        gates_ref[...] = _dot(hb, wg_ref[...])
        gatest_ref[...] = _dot_nt(wgt_ref[...], hb)
        kt_ref[...] = _dot_nt(wkt_ref[...], hb).astype(kt_ref.dtype)

    o_ref[...] = _dot(hb_ref[...], w_ref[...]).astype(o_ref.dtype)


def _ln_proj(x2, g, b, w_main, w_gates, w_gates_t, w_k_t):
    s, d = x2.shape
    tm, tn = min(PROJ_TM, s), PROJ_TN
    ng = w_gates.shape[1]
    fixed = lambda i, j: (0, 0)
    return pl.pallas_call(
        _ln_proj_kernel,
        grid=(s // tm, PROJ_W // tn),
        in_specs=[
            pl.BlockSpec((tm, d), lambda i, j: (i, 0)),
            pl.BlockSpec((1, d), fixed),
            pl.BlockSpec((1, d), fixed),
            pl.BlockSpec((d, tn), lambda i, j: (0, j)),
            pl.BlockSpec((d, ng), fixed),
            pl.BlockSpec((w_gates_t.shape[0], d), fixed),
            pl.BlockSpec((D_QK, d), fixed),
        ],
        out_specs=[
            pl.BlockSpec((tm, tn), lambda i, j: (i, j)),
            pl.BlockSpec((tm, ng), lambda i, j: (i, 0)),
            pl.BlockSpec((w_gates_t.shape[0], tm), lambda i, j: (0, i)),
            pl.BlockSpec((D_QK, tm), lambda i, j: (0, i)),
        ],
        out_shape=[
            jax.ShapeDtypeStruct((s, PROJ_W), BF16),
            jax.ShapeDtypeStruct((s, ng), F32),
            jax.ShapeDtypeStruct((w_gates_t.shape[0], s), F32),
            jax.ShapeDtypeStruct((D_QK, s), BF16),
        ],
        scratch_shapes=[pltpu.VMEM((tm, d), BF16)],
        compiler_params=_params(("arbitrary", "arbitrary")),
        name="ln_proj",
    )(x2, g, b, w_main, w_gates, w_gates_t, w_k_t)


def _mlstm_kernel(bi_ref, bf_ref, mq_ref, mk_ref, mv_ref, mo_ref, gates_ref, gatest_ref,
                  cw_ref, cb_ref, ng_ref, o_ref, c_ref, n_ref, m_ref, tail_ref):
    L = mq_ref.shape[0]
    dk, dv = MLSTM_DQK, MLSTM_DV

    @pl.when(pl.program_id(0) == 0)
    def _():
        c_ref[...] = jnp.zeros_like(c_ref)
        n_ref[...] = jnp.zeros_like(n_ref)
        m_ref[...] = jnp.zeros_like(m_ref)
        tail_ref[...] = jnp.zeros_like(tail_ref)

    u = jnp.concatenate([mq_ref[...], mk_ref[...]], axis=1).astype(F32)
    tail = tail_ref[...]
    row8 = lax.broadcasted_iota(jnp.int32, (V7X_SUBLANES, u.shape[1]), 0)
    conv = cb_ref[...] + cw_ref[CONV_WIDTH - 1:CONV_WIDTH, :] * u
    for back in range(1, CONV_WIDTH):
        ur = pltpu.roll(u, back, 0)
        head = jnp.where(row8 < back, pltpu.roll(tail, back, 0), ur[:V7X_SUBLANES])
        shifted = jnp.concatenate([head, ur[V7X_SUBLANES:]], axis=0)
        conv = conv + cw_ref[CONV_WIDTH - 1 - back:CONV_WIDTH - back, :] * shifted
    tail_ref[...] = u[L - V7X_SUBLANES:]
    qk = conv * jax.nn.sigmoid(conv)
    q_all = qk[:, :M_QK]
    k_all = qk[:, M_QK:] * (dk ** -0.5)

    r_i = lax.broadcasted_iota(jnp.int32, (L, L), 0)
    c_i = lax.broadcasted_iota(jnp.int32, (L, L), 1)
    causal = c_i <= r_i
    tril = jnp.where(causal, 1.0, 0.0).astype(BF16)
    triu = jnp.where(r_i <= c_i, 1.0, 0.0).astype(BF16)

    for h in range(MLSTM_HEADS):
        q = q_all[:, h * dk:(h + 1) * dk]
        k = k_all[:, h * dk:(h + 1) * dk]
        qb, kb = q.astype(BF16), k.astype(BF16)
        vb = mv_ref[:, h * dv:(h + 1) * dv]
        b_i, b_f = bi_ref[h], bf_ref[h]

        gblk = gates_ref[:, h * V7X_LANES:(h + 1) * V7X_LANES]
        i_col = gblk[:, 0:1] + b_i
        lf_blk = _log_sigmoid(gblk + b_f)
        f1, f2, f3 = _split3(lf_blk)
        bcum_blk = _dot(tril, f1) + _dot(tril, f2) + _dot(tril, f3)
        b_col = bcum_blk[:, 1:2]
        gt = gatest_ref[h * V7X_SUBLANES:(h + 1) * V7X_SUBLANES, :]
        i_row = gt[0:1, :] + b_i
        lf_rows = _log_sigmoid(gt + b_f)
        g1, g2, g3 = _split3(lf_rows)
        b_row = (_dot(g1, triu) + _dot(g2, triu) + _dot(g3, triu))[1:2, :]

        m_prev = m_ref[h, 0:1, 0:1]
        dmat = jnp.where(causal, b_col - b_row + i_row, NEG)
        inter = b_col + m_prev
        m_t = jnp.maximum(inter, jnp.max(dmat, -1, keepdims=True))
        wts = jnp.exp(dmat - m_t)
        g = jnp.exp(inter - m_t)
        sqk = _dot_nt(qb, kb) * wts
        c_prev = c_ref[h]
        n_prev = n_ref[h, 0:1, :]
        num = g * _dot(qb, c_prev.astype(BF16)) + _dot(sqk.astype(BF16), vb)
        den = g * jnp.sum(q * n_prev, -1, keepdims=True) + jnp.sum(sqk, -1, keepdims=True)
        hh = num / jnp.maximum(jnp.abs(den), jnp.exp(-m_t))

        b_last = b_col[L - 1:L, :]
        w_last_row = b_last - b_row + i_row
        m_new = jnp.maximum(b_last + m_prev, jnp.max(w_last_row, -1, keepdims=True))
        decay = jnp.exp(b_last + m_prev - m_new)
        ws_col = jnp.exp(b_last - b_col + i_col - m_new)
        kw = k * ws_col
        c_ref[h] = decay * c_prev + _dot(kw.T.astype(BF16), vb)
        n_ref[h, 0:1, :] = decay * n_prev + jnp.sum(kw, 0, keepdims=True)
        m_ref[h] = jnp.broadcast_to(m_new, m_ref.shape[1:])

        mu = jnp.mean(hh, -1, keepdims=True)
        hc = hh - mu
        var = jnp.mean(hc * hc, -1, keepdims=True)
        hn = hc * lax.rsqrt(var + LN_EPS) * ng_ref[:, h * dv:(h + 1) * dv]
        og = jax.nn.sigmoid(mo_ref[:, h * dv:(h + 1) * dv].astype(F32))
        o_ref[:, h * dv:(h + 1) * dv] = (hn * og).astype(o_ref.dtype)


def _mlstm(proj, gates, gates_t, conv_w, conv_b, b_i, b_f, norm_g):
    s = proj.shape[0]
    L = min(MLSTM_L, s)
    smem = pl.BlockSpec(memory_space=pltpu.SMEM)
    return pl.pallas_call(
        _mlstm_kernel,
        grid=(s // L,),
        in_specs=[
            smem, smem,
            pl.BlockSpec((L, M_QK), lambda c: (c, 0)),
            pl.BlockSpec((L, M_QK), lambda c: (c, 1)),
            pl.BlockSpec((L, M_V), lambda c: (c, 1)),
            pl.BlockSpec((L, M_V), lambda c: (c, 2)),
            pl.BlockSpec((L, gates.shape[1]), lambda c: (c, 0)),
            pl.BlockSpec((gates_t.shape[0], L), lambda c: (0, c)),
            pl.BlockSpec((CONV_WIDTH, 2 * M_QK), lambda c: (0, 0)),
            pl.BlockSpec((1, 2 * M_QK), lambda c: (0, 0)),
            pl.BlockSpec((1, M_V), lambda c: (0, 0)),
        ],
        out_specs=pl.BlockSpec((L, M_V), lambda c: (c, 0)),
        out_shape=jax.ShapeDtypeStruct((s, M_V), BF16),
        scratch_shapes=[
            pltpu.VMEM((MLSTM_HEADS, MLSTM_DQK, MLSTM_DV), F32),
            pltpu.VMEM((MLSTM_HEADS, V7X_SUBLANES, MLSTM_DQK), F32),
            pltpu.VMEM((MLSTM_HEADS, V7X_SUBLANES, V7X_LANES), F32),
            pltpu.VMEM((V7X_SUBLANES, 2 * M_QK), F32),
        ],
        compiler_params=_params(("arbitrary",)),
        name="mlstm",
    )(b_i, b_f, proj, proj, proj, proj, gates, gates_t, conv_w, conv_b, norm_g)


def _t5_bias_tile(table_ref, h, offset):
    n_t = V7X_LANES
    r_i = lax.broadcasted_iota(jnp.int32, (n_t, n_t), 0)
    c_i = lax.broadcasted_iota(jnp.int32, (n_t, n_t), 1)
    n = jnp.maximum(offset + r_i - c_i, 0)
    max_exact = REL_BUCKETS // 2
    large = max_exact + (jnp.log(jnp.maximum(n, 1).astype(F32) / max_exact)
                         / math.log(REL_MAX_DIST / max_exact) * (REL_BUCKETS - max_exact)).astype(jnp.int32)
    large = jnp.minimum(large, REL_BUCKETS - 1)
    bucket = jnp.where(n < max_exact, n, large)
    out = jnp.zeros((n_t, n_t), F32)
    for b in range(REL_BUCKETS):
        out = jnp.where(bucket == b, table_ref[b * DIFF_HEADS + h], out)
    return out * LOG2E


def _diff_attn_kernel(qi_ref, kj_ref, q_ref, kt_ref, v_ref, lam_ref, table_ref, o_ref,
                      q1_ref, q2_ref, vx_ref, m_ref, acc_ref, bdiag_ref, bsub_ref):
    h = pl.program_id(0)
    p = pl.program_id(1)
    i = qi_ref[p]
    j = kj_ref[p]
    BQ = q_ref.shape[0]
    BK = v_ref.shape[0]
    n_half = BQ // BK
    SR = min(ATT_STRIP, BK)
    strips_per_half = BK // SR
    n_sub = BK // V7X_LANES
    c_far = table_ref[(REL_BUCKETS - 1) * DIFF_HEADS + h] * LOG2E

    @pl.when(p == 0)
    def _():
        p0 = _t5_bias_tile(table_ref, h, 0)
        p0 = jnp.where(lax.broadcasted_iota(jnp.int32, p0.shape, 1) <= lax.broadcasted_iota(jnp.int32, p0.shape, 0),
                       p0, NEG)
        p1 = _t5_bias_tile(table_ref, h, V7X_LANES)
        far = jnp.full((V7X_LANES, V7X_LANES), c_far, F32)
        neg = jnp.full((V7X_LANES, V7X_LANES), NEG, F32)
        for a in range(n_sub):
            for b in range(n_sub):
                tile = p0 if a == b else p1 if a == b + 1 else far if a > b else neg
                bdiag_ref[a * V7X_LANES:(a + 1) * V7X_LANES, b * V7X_LANES:(b + 1) * V7X_LANES] = tile
        bsub_ref[...] = jnp.full(bsub_ref.shape, c_far, F32)
        bsub_ref[0:V7X_LANES, BK - V7X_LANES:BK] = p1

    @pl.when(j == 0)
    def _():
        q = q_ref[...].astype(F32) * ((DIFF_DH ** -0.5) * LOG2E)
        lane = lax.broadcasted_iota(jnp.int32, q.shape, 1)
        q1_ref[...] = jnp.where(lane < DIFF_DH, q, 0.0).astype(BF16)
        q2_ref[...] = jnp.where(lane >= DIFF_DH, q, 0.0).astype(BF16)
        m_ref[...] = jnp.full(m_ref.shape, NEG, F32)
        acc_ref[...] = jnp.zeros_like(acc_ref)

    vx_ref[:, 0:DIFF_DV] = v_ref[...]
    vx_ref[:, DIFF_DV:] = jnp.ones((BK, DIFF_DV), BF16)

    def strip(r, kl, bias_tile=None):
        rows = slice(r * SR, (r + 1) * SR)
        kb = kt_ref[:, 0:kl]
        vb = vx_ref[0:kl, :]
        for a, qz_ref in enumerate((q1_ref, q2_ref)):
            s = _dot(qz_ref[rows, :], kb)
            m_old = m_ref[a, rows, :]
            if bias_tile is None:
                m_new = jnp.maximum(m_old, jnp.max(s, -1, keepdims=True) + c_far)
                pm = jnp.exp2(s - (m_new - c_far))
            else:
                s = s + bias_tile
                m_new = jnp.maximum(m_old, jnp.max(s, -1, keepdims=True))
                pm = jnp.exp2(s - m_new)
            alpha = jnp.exp2(m_old - m_new)
            acc_ref[a, rows, :] = alpha * acc_ref[a, rows, :] + _dot(pm.astype(BF16), vb)
            m_ref[a, rows, :] = m_new

    def key_block(delta):
        for r in range(n_half * strips_per_half):
            ro = (r % strips_per_half) * SR
            rel = None if delta is None else delta - r // strips_per_half
            if rel is None or rel <= -2:
                strip(r, BK)
            elif rel == -1:
                strip(r, BK, bsub_ref[...] if ro == 0 else None)
            elif rel == 0:
                strip(r, ro + SR, bdiag_ref[ro:ro + SR, 0:ro + SR])

    d = j - i * n_half

    @pl.when(d <= -2)
    def _():
        key_block(None)

    for delta in range(-1, n_half):
        @pl.when(d == delta)
        def _(delta=delta):
            key_block(delta)
            if delta == n_half - 1:
                lq1, lk1, lq2, lk2 = (lam_ref[t:t + 1, :] for t in range(4))
                lam = (jnp.exp(jnp.sum(lq1 * lk1, -1, keepdims=True))
                       - jnp.exp(jnp.sum(lq2 * lk2, -1, keepdims=True)) + LAM_INIT)
                o = (acc_ref[0, :, 0:DIFF_DV] / acc_ref[0, :, DIFF_DV:DIFF_DV + 1]
                     - lam * (acc_ref[1, :, 0:DIFF_DV] / acc_ref[1, :, DIFF_DV:DIFF_DV + 1]))
                o = o * lax.rsqrt(jnp.mean(o * o, -1, keepdims=True) + LN_EPS)
                o_ref[...] = (o * (1.0 - LAM_INIT)).astype(o_ref.dtype)


def _diff_attn(proj, k_t, lam_vecs, table_flat):
    s = proj.shape[0]
    BK = min(ATT_BK, s)
    BQ = min(ATT_BQ, s)
    n_half = BQ // BK
    pairs = [(i, j) for i in range(s // BQ) for j in range((i + 1) * n_half)]
    qi = jnp.asarray([a for a, _ in pairs], jnp.int32)
    kj = jnp.asarray([b for _, b in pairs], jnp.int32)
    q_blk = (2 * M_QK + 2 * M_V) // V7X_LANES
    v_blk = q_blk + D_QK // V7X_LANES
    SR = min(ATT_STRIP, BK)
    grid_spec = pltpu.PrefetchScalarGridSpec(
        num_scalar_prefetch=2,
        grid=(DIFF_HEADS, len(pairs)),
        in_specs=[
            pl.BlockSpec((BQ, V7X_LANES), lambda h, p, qi, kj: (qi[p], q_blk + h)),
            pl.BlockSpec((2 * DIFF_DH, BK), lambda h, p, qi, kj: (h, kj[p])),
            pl.BlockSpec((BK, V7X_LANES), lambda h, p, qi, kj: (kj[p], v_blk + h)),
            pl.BlockSpec((4, DIFF_DH), lambda h, p, qi, kj: (0, 0)),
            pl.BlockSpec(memory_space=pltpu.SMEM),
        ],
        out_specs=pl.BlockSpec((BQ, V7X_LANES), lambda h, p, qi, kj: (qi[p], h)),
        scratch_shapes=[
            pltpu.VMEM((BQ, V7X_LANES), BF16),
            pltpu.VMEM((BQ, V7X_LANES), BF16),
            pltpu.VMEM((BK, 2 * DIFF_DV), BF16),
            pltpu.VMEM((2, BQ, 1), F32),
            pltpu.VMEM((2, BQ, 2 * DIFF_DV), F32),
            pltpu.VMEM((BK, BK), F32),
            pltpu.VMEM((SR, BK), F32),
        ],
    )
    return pl.pallas_call(
        _diff_attn_kernel,
        grid_spec=grid_spec,
        out_shape=jax.ShapeDtypeStruct((s, D_VW), BF16),
        compiler_params=_params(("arbitrary", "arbitrary")),
        name="diff_attn",
    )(qi, kj, proj, k_t, proj, lam_vecs, table_flat)


def _out_ln_kernel(hm_ref, hd_ref, x_ref, g0_ref, b0_ref, wo1_ref, wo2_ref, g1_ref, b1_ref, wr_ref,
                   h1_ref, h1p_ref, logit_ref):
    mix = _dot(hm_ref[...], wo1_ref[...]) + _dot(hd_ref[...], wo2_ref[...])
    h0 = _layer_norm(x_ref[...], g0_ref[...], b0_ref[...])
    h1 = _layer_norm(ALPHA * h0 + mix, g1_ref[...], b1_ref[...])
    h1_ref[...] = h1
    _store_row_tiles(h1p_ref, _pack_bf16_pairs(h1))
    logit_ref[...] = _dot_nt(wr_ref[...], h1.astype(BF16))


def _out_ln(hm, hd, x2, g0, b0, wo1, wo2, g1, b1, wr):
    s, d = x2.shape
    tm = min(OUT_TM, s)
    row = lambda i: (i, 0)
    fixed = lambda i: (0, 0)
    return pl.pallas_call(
        _out_ln_kernel,
        grid=(s // tm,),
        in_specs=[
            pl.BlockSpec((tm, M_V), row), pl.BlockSpec((tm, D_VW), row), pl.BlockSpec((tm, d), row),
            pl.BlockSpec((1, d), fixed), pl.BlockSpec((1, d), fixed),
            pl.BlockSpec((M_V, d), fixed), pl.BlockSpec((D_VW, d), fixed),
            pl.BlockSpec((1, d), fixed), pl.BlockSpec((1, d), fixed),
            pl.BlockSpec((N_EXPERTS, d), fixed),
        ],
        out_specs=[pl.BlockSpec((tm, d), row), pl.BlockSpec((tm * ROW_TILE, V7X_LANES), row),
                   pl.BlockSpec((N_EXPERTS, tm), lambda i: (0, i))],
        out_shape=[jax.ShapeDtypeStruct((s, d), F32), jax.ShapeDtypeStruct((s * ROW_TILE, V7X_LANES), jnp.uint32),
                   jax.ShapeDtypeStruct((N_EXPERTS, s), F32)],
        compiler_params=_params(("arbitrary",)),
        name="out_ln",
    )(hm, hd, x2, g0, b0, wo1, wo2, g1, b1, wr)


def _route_kernel(logit_ref, rb_ref, topi_ref, pos_ref, topw_ref, cnt_ref, carry_ref):
    n_e, tm = logit_ref.shape
    gsz = n_e // N_GROUPS
    ninf = -jnp.inf

    @pl.when(pl.program_id(0) == 0)
    def _():
        carry_ref[...] = jnp.zeros_like(carry_ref)

    sc = jax.nn.sigmoid(logit_ref[...])
    sel = sc + rb_ref[...]

    sel3 = sel.reshape(N_GROUPS, gsz, tm)
    in_grp = lax.broadcasted_iota(jnp.int32, sel3.shape, 1)
    m1 = jnp.max(sel3, 1, keepdims=True)
    i1 = jnp.min(jnp.where(sel3 == m1, in_grp, gsz), 1, keepdims=True)
    m2 = jnp.max(jnp.where(in_grp == i1, ninf, sel3), 1, keepdims=True)
    gscore = (m1 + m2).reshape(N_GROUPS, tm)
    gid = lax.broadcasted_iota(jnp.int32, gscore.shape, 0)
    beaten = jnp.zeros(gscore.shape, F32)
    for o in range(1, N_GROUPS):
        other = pltpu.roll(gscore, o, 0)
        wins = (other > gscore) | ((other == gscore) & (gid >= o))
        beaten = beaten + jnp.where(wins, 1.0, 0.0)
    keep = jnp.where(beaten < TOPK_GROUPS, 1.0, 0.0)
    keep3 = jnp.broadcast_to(keep.reshape(N_GROUPS, 1, tm), sel3.shape)
    masked = jnp.where(keep3 > 0.5, sel3, ninf).reshape(n_e, tm)

    eid = lax.broadcasted_iota(jnp.int32, (n_e, tm), 0)
    onehots, idxs, ws = [], [], []
    for _ in range(TOP_K):
        m = jnp.max(masked, 0, keepdims=True)
        idx = jnp.min(jnp.where(masked == m, eid, n_e), 0, keepdims=True)
        oh = eid == idx
        onehots.append(oh)
        idxs.append(idx)
        ws.append(jnp.sum(jnp.where(oh, sc, 0.0), 0, keepdims=True))
        masked = jnp.where(oh, ninf, masked)
    wsum = ws[0]
    for w in ws[1:]:
        wsum = wsum + w

    chosen = jnp.zeros((n_e, tm), F32)
    for oh in onehots:
        chosen = jnp.where(oh, 1.0, chosen)
    r_i = lax.broadcasted_iota(jnp.int32, (tm, tm), 0)
    c_i = lax.broadcasted_iota(jnp.int32, (tm, tm), 1)
    before = jnp.where(r_i < c_i, 1.0, 0.0).astype(BF16)
    rank = _dot(chosen.astype(BF16), before) + carry_ref[...]
    carry_new = carry_ref[...] + jnp.sum(chosen, 1, keepdims=True)
    carry_ref[...] = carry_new
    cnt_ref[...] = carry_new

    kid = lax.broadcasted_iota(jnp.int32, (TOP_K, tm), 0)
    topi = jnp.zeros((TOP_K, tm), jnp.int32)
    pos = jnp.zeros((TOP_K, tm), jnp.int32)
    topw = jnp.zeros((TOP_K, tm), F32)
    for k in range(TOP_K):
        pk = jnp.sum(jnp.where(onehots[k], rank, 0.0), 0, keepdims=True)
        topi = jnp.where(kid == k, idxs[k], topi)
        pos = jnp.where(kid == k, pk.astype(jnp.int32), pos)
        topw = jnp.where(kid == k, ws[k] / wsum * ROUTED_SCALE, topw)
    topi_ref[...] = topi
    pos_ref[...] = pos
    topw_ref[...] = topw


def _route(logits_t, router_bias_col):
    n_e, t = logits_t.shape
    tm = min(ROUTE_TM, t)
    col = lambda i: (0, i)
    fixed = lambda i: (0, 0)
    return pl.pallas_call(
        _route_kernel,
        grid=(t // tm,),
        in_specs=[pl.BlockSpec((n_e, tm), col), pl.BlockSpec((n_e, 1), fixed)],
        out_specs=[pl.BlockSpec((TOP_K, tm), col), pl.BlockSpec((TOP_K, tm), col),
                   pl.BlockSpec((TOP_K, tm), col), pl.BlockSpec((n_e, 1), fixed)],
        out_shape=[jax.ShapeDtypeStruct((TOP_K, t), jnp.int32), jax.ShapeDtypeStruct((TOP_K, t), jnp.int32),
                   jax.ShapeDtypeStruct((TOP_K, t), F32), jax.ShapeDtypeStruct((n_e, 1), F32)],
        scratch_shapes=[pltpu.VMEM((n_e, 1), F32)],
        compiler_params=_params(("arbitrary",)),
        name="route",
    )(logits_t, router_bias_col)


def _plan_kernel(cnt_ref, cntc_ref, topi_ref, pos_ref, off_ref, vexp_ref, vblk_ref, vcnt_ref, nvis_ref, dest_ref):
    nv = vexp_ref.shape[0]
    tr = float(MOE_TR)
    cnt = jnp.broadcast_to(cnt_ref[...], (V7X_SUBLANES, N_EXPERTS))
    r_i = lax.broadcasted_iota(jnp.int32, (N_EXPERTS, N_EXPERTS), 0)
    c_i = lax.broadcasted_iota(jnp.int32, (N_EXPERTS, N_EXPERTS), 1)
    upper = jnp.where(r_i <= c_i, 1.0, 0.0).astype(BF16)

    def cumsum_lanes(a):
        a1, a2, a3 = _split3(a)
        return _dot(a1, upper) + _dot(a2, upper) + _dot(a3, upper)

    end = cumsum_lanes(cnt)
    start = end - cnt
    first_blk = jnp.floor(start / tr)
    last_blk = jnp.floor((end - 1.0) / tr)
    nvis_e = jnp.where(cnt > 0.0, last_blk - first_blk + 1.0, 0.0)
    vend = cumsum_lanes(nvis_e)
    vstart = vend - nvis_e
    off_ref[...] = start[0:1, :].astype(jnp.int32)
    nvis_ref[...] = vend[0:1, N_EXPERTS - 1:N_EXPERTS].astype(jnp.int32)

    v = lax.broadcasted_iota(jnp.int32, (nv, N_EXPERTS), 0).astype(F32)
    ve = jnp.broadcast_to(vend[0:1, :], (nv, N_EXPERTS))
    expert = jnp.sum(jnp.where(ve <= v, 1.0, 0.0), -1, keepdims=True)
    expert = jnp.minimum(expert, N_EXPERTS - 1.0)
    lane = lax.broadcasted_iota(jnp.int32, (nv, N_EXPERTS), 1).astype(F32)
    mine = lane == expert
    fb = jnp.sum(jnp.where(mine, jnp.broadcast_to(first_blk[0:1, :], (nv, N_EXPERTS)), 0.0), -1, keepdims=True)
    vs = jnp.sum(jnp.where(mine, jnp.broadcast_to(vstart[0:1, :], (nv, N_EXPERTS)), 0.0), -1, keepdims=True)
    vc = jnp.sum(jnp.where(mine, jnp.broadcast_to(nvis_e[0:1, :], (nv, N_EXPERTS)), 0.0), -1, keepdims=True)
    vexp_ref[...] = expert.astype(jnp.int32)
    vblk_ref[...] = (fb + (v[:, 0:1] - vs)).astype(jnp.int32)
    vcnt_ref[...] = vc.astype(jnp.int32)

    lower = jnp.where(c_i < r_i, 1.0, 0.0).astype(BF16)
    c1, c2, c3 = _split3(jnp.broadcast_to(cntc_ref[...], (N_EXPERTS, V7X_LANES)))
    start_col = (_dot(lower, c1) + _dot(lower, c2) + _dot(lower, c3))[:, 0:1]
    n_tok = topi_ref.shape[1]
    tb = min(PLAN_TB, n_tok)
    eid = lax.broadcasted_iota(jnp.int32, (N_EXPERTS, tb), 0)
    kid = lax.broadcasted_iota(jnp.int32, (TOP_K, tb), 0)

    def dest_block(i, carry):
        cols = pl.ds(pl.multiple_of(i * tb, tb), tb)
        ti = topi_ref[:, cols]
        first_row = jnp.zeros((TOP_K, tb), F32)
        for k in range(TOP_K):
            fr = jnp.sum(jnp.where(eid == ti[k:k + 1, :], start_col, 0.0), 0, keepdims=True)
            first_row = jnp.where(kid == k, fr, first_row)
        dest_ref[:, cols] = first_row.astype(jnp.int32) + pos_ref[:, cols]
        return carry

    lax.fori_loop(0, n_tok // tb, dest_block, 0)


def _plan(counts_row, counts_col, topi_t, pos_t):
    n_rows = topi_t.shape[0] * topi_t.shape[1]
    nv = n_rows // MOE_TR + N_EXPERTS
    return pl.pallas_call(
        _plan_kernel,
        out_shape=[jax.ShapeDtypeStruct((1, N_EXPERTS), jnp.int32), jax.ShapeDtypeStruct((nv, 1), jnp.int32),
                   jax.ShapeDtypeStruct((nv, 1), jnp.int32), jax.ShapeDtypeStruct((nv, 1), jnp.int32),
                   jax.ShapeDtypeStruct((1, 1), jnp.int32), jax.ShapeDtypeStruct(topi_t.shape, jnp.int32)],
        compiler_params=pltpu.CompilerParams(vmem_limit_bytes=V7X_VMEM_LIMIT_BYTES),
        name="plan",
    )(counts_row, counts_col, topi_t, pos_t)


def _row_tile(ref, r):
    return ref.at[pl.ds(pl.multiple_of(r * ROW_TILE, ROW_TILE), ROW_TILE)]


def _dispatch_kernel(dest_ref, x_ref, xs_ref, sem):
    tm = x_ref.shape[0] // ROW_TILE

    def issue(t, carry):
        for k in range(TOP_K):
            d = dest_ref[t * TOP_K + k]
            pltpu.make_async_copy(_row_tile(x_ref, t), _row_tile(xs_ref, d), sem).start(priority=k % 2)
        return carry

    lax.fori_loop(0, tm, issue, 0)
    for _ in range(TOP_K):
        pltpu.make_async_copy(x_ref, xs_ref.at[pl.ds(0, tm * ROW_TILE)], sem).wait()


def _dispatch(dest_flat, rows):
    t = rows.shape[0] // ROW_TILE
    tm = min(DISPATCH_TM, t)
    return pl.pallas_call(
        _dispatch_kernel,
        grid=(t // tm,),
        in_specs=[
            pl.BlockSpec((tm * TOP_K,), lambda i: (i,), memory_space=pltpu.SMEM),
            pl.BlockSpec((tm * ROW_TILE, V7X_LANES), lambda i: (i, 0)),
        ],
        out_specs=pl.BlockSpec(memory_space=pl.ANY),
        out_shape=jax.ShapeDtypeStruct((t * TOP_K * ROW_TILE, V7X_LANES), rows.dtype),
        scratch_shapes=[pltpu.SemaphoreType.DMA(())],
        compiler_params=_params(("arbitrary",)),
        name="dispatch",
    )(dest_flat, rows)


def _experts_kernel(n_shared_steps, vexp_ref, vblk_ref, vcnt_ref, off_ref, nvis_ref, xs_ref, wg_hbm, wu_hbm,
                    wd_hbm, hp_ref, sg_ref, su_ref, sd_ref, ys_ref, ysh_ref,
                    wg_buf, wu_buf, wd_buf, acc_ref, slot_ref, sem):
    v = pl.program_id(0)
    tr = xs_ref.shape[0] // ROW_TILE
    nvis = nvis_ref[0]

    @pl.when(v < n_shared_steps)
    def _():
        hb = _unpack_bf16_pairs(_load_row_tiles(hp_ref, hp_ref.shape[0] // ROW_TILE, ROW_TILE)).astype(BF16)
        gate = _dot(hb, sg_ref[...])
        up = _dot(hb, su_ref[...])
        ysh_ref[...] = _dot((gate * jax.nn.sigmoid(gate) * up).astype(BF16), sd_ref[...])

    half = wd_buf.shape[1] // 2

    def weight_copies(e, slot):
        return ((pltpu.make_async_copy(wg_hbm.at[e], wg_buf.at[slot], sem.at[slot, 0]), 1),
                (pltpu.make_async_copy(wu_hbm.at[e], wu_buf.at[slot], sem.at[slot, 1]), 1),
                (pltpu.make_async_copy(wd_hbm.at[e, pl.ds(0, half)], wd_buf.at[slot, pl.ds(0, half)],
                                       sem.at[slot, 2]), 1),
                (pltpu.make_async_copy(wd_hbm.at[e, pl.ds(half, half)], wd_buf.at[slot, pl.ds(half, half)],
                                       sem.at[slot, 3]), 1))

    @pl.when(v < nvis)
    def _():
        e = vexp_ref[v]
        blk = vblk_ref[v]
        first_of_expert = jnp.logical_or(v == 0, vexp_ref[jnp.maximum(v - 1, 0)] != e)

        @pl.when(v == 0)
        def _():
            slot_ref[0] = 0
            for c, queue in weight_copies(e, 0):
                c.start(priority=queue)

        @pl.when(jnp.logical_and(first_of_expert, v > 0))
        def _():
            slot_ref[0] = 1 - slot_ref[0]

        slot = slot_ref[0]

        @pl.when(first_of_expert)
        def _():
            nxt = v + vcnt_ref[v]

            @pl.when(nxt < nvis)
            def _():
                for c, queue in weight_copies(vexp_ref[jnp.minimum(nxt, vexp_ref.shape[0] - 1)], 1 - slot):
                    c.start(priority=queue)

            for c, _ in weight_copies(e, slot):
                c.wait()

        lo = off_ref[e] - blk * tr
        hi = off_ref[e + 1] - blk * tr
        first_of_block = jnp.logical_or(v == 0, vblk_ref[jnp.maximum(v - 1, 0)] != blk)

        @pl.when(first_of_block)
        def _():
            acc_ref[...] = jnp.zeros_like(acc_ref)

        def swiglu_rows(r0, n):
            row = r0 + lax.broadcasted_iota(jnp.int32, (n, 1), 0)
            mine = (row >= lo) & (row < hi)
            xb = _unpack_bf16_pairs(_load_row_tiles(xs_ref, n, ROW_TILE, r0)).astype(BF16)
            gate = _dot(xb, wg_buf[slot].astype(BF16))
            up = _dot(xb, wu_buf[slot].astype(BF16))
            act = (gate * jax.nn.sigmoid(gate) * up).astype(BF16)
            acc_ref[r0:r0 + n, :] += jnp.where(mine, _dot(act, wd_buf[slot].astype(BF16)), 0.0)

        hr = tr // 2
        top_only = hi <= hr
        bottom_only = lo >= hr

        @pl.when(top_only)
        def _():
            swiglu_rows(0, hr)

        @pl.when(bottom_only)
        def _():
            swiglu_rows(hr, hr)

        @pl.when(jnp.logical_not(jnp.logical_or(top_only, bottom_only)))
        def _():
            swiglu_rows(0, tr)

        last_of_block = jnp.logical_or(v + 1 >= nvis, vblk_ref[jnp.minimum(v + 1, vblk_ref.shape[0] - 1)] != blk)

        @pl.when(last_of_block)
        def _():
            _store_row_tiles(ys_ref, _pack_bf16_pairs(acc_ref[...]))


def _experts(vexp, vblk, vcnt, off_ext, nvis, xs, w_gate, w_up, w_down, h1p, sg, su, sd):
    n_rows = xs.shape[0] // ROW_TILE
    d = D_MODEL
    nv = vexp.shape[0]
    t = h1p.shape[0] // ROW_TILE
    tm = min(SHARED_TM, t)
    n_shared_steps = t // tm
    assert n_shared_steps <= n_rows // MOE_TR

    def block_of(v, ve, vb, vc, off, nvis):
        return (vb[jnp.minimum(v, nvis[0] - 1)], 0)

    def shared_block(v, ve, vb, vc, off, nvis):
        return (jnp.minimum(v, n_shared_steps - 1), 0)

    fixed = lambda v, ve, vb, vc, off, nvis: (0, 0)
    resident = dict(pipeline_mode=pl.Buffered(1))
    grid_spec = pltpu.PrefetchScalarGridSpec(
        num_scalar_prefetch=5,
        grid=(nv,),
        in_specs=[
            pl.BlockSpec((MOE_TR * ROW_TILE, V7X_LANES), block_of),
            pl.BlockSpec(memory_space=pl.ANY),
            pl.BlockSpec(memory_space=pl.ANY),
            pl.BlockSpec(memory_space=pl.ANY),
            pl.BlockSpec((tm * ROW_TILE, V7X_LANES), shared_block),
            pl.BlockSpec((d, D_EXPERT), fixed, **resident),
            pl.BlockSpec((d, D_EXPERT), fixed, **resident),
            pl.BlockSpec((D_EXPERT, d), fixed, **resident),
        ],
        out_specs=[pl.BlockSpec((MOE_TR * ROW_TILE, V7X_LANES), block_of), pl.BlockSpec((tm, d), shared_block)],
        scratch_shapes=[
            pltpu.VMEM((2, d, D_EXPERT), F32),
            pltpu.VMEM((2, d, D_EXPERT), F32),
            pltpu.VMEM((2, D_EXPERT, d), F32),
            pltpu.VMEM((MOE_TR, d), F32),
            pltpu.SMEM((1,), jnp.int32),
            pltpu.SemaphoreType.DMA((2, 4)),
        ],
    )
    return pl.pallas_call(
        functools.partial(_experts_kernel, n_shared_steps),
        grid_spec=grid_spec,
        out_shape=[jax.ShapeDtypeStruct(xs.shape, jnp.uint32), jax.ShapeDtypeStruct((t, d), F32)],
        compiler_params=_params(("arbitrary",)),
        name="experts",
    )(vexp, vblk, vcnt, off_ext, nvis, xs, w_gate, w_up, w_down, h1p, sg, su, sd)


def _combine_kernel(dest_ref, dest_next_ref, topw_ref, h1_ref, ysh_ref, ys_ref, g2_ref, b2_ref, o_ref,
                    buf_ref, sem):
    i = pl.program_id(0)
    tm = h1_ref.shape[0]
    slot = lax.rem(i, 2)

    def issue_block(d_ref, s):
        def issue(t, carry):
            for k in range(TOP_K):
                d = d_ref[t * TOP_K + k]
                pltpu.make_async_copy(_row_tile(ys_ref, d), _row_tile(buf_ref.at[s, k], t),
                                      sem.at[s]).start(priority=k % 2)
            return carry

        lax.fori_loop(0, tm, issue, 0)

    @pl.when(i == 0)
    def _():
        issue_block(dest_ref, 0)

    @pl.when(i + 1 < pl.num_programs(0))
    def _():
        issue_block(dest_next_ref, 1 - slot)

    y = ALPHA * h1_ref[...] + ysh_ref[...]
    for k in range(TOP_K):
        pltpu.make_async_copy(ys_ref.at[pl.ds(0, tm * ROW_TILE)], buf_ref.at[slot, k], sem.at[slot]).wait()
    w = topw_ref[...]
    for k in range(TOP_K):
        y = y + _unpack_bf16_pairs(_load_row_tiles(buf_ref.at[slot, k], tm, ROW_TILE)) * w[:, k:k + 1]
    o_ref[...] = _layer_norm(y, g2_ref[...], b2_ref[...])


def _combine(dest_flat, topw, h1, ysh, ys, g2, b2):
    t, d = h1.shape
    tm = min(MOVE_TM, t)
    nb = t // tm
    row = lambda i: (i, 0)
    fixed = lambda i: (0, 0)
    return pl.pallas_call(
        _combine_kernel,
        grid=(nb,),
        in_specs=[
            pl.BlockSpec((tm * TOP_K,), lambda i: (i,), memory_space=pltpu.SMEM),
            pl.BlockSpec((tm * TOP_K,), lambda i: (jnp.minimum(i + 1, nb - 1),), memory_space=pltpu.SMEM),
            pl.BlockSpec((tm, TOP_K), row),
            pl.BlockSpec((tm, d), row),
            pl.BlockSpec((tm, d), row),
            pl.BlockSpec(memory_space=pl.ANY),
            pl.BlockSpec((1, d), fixed), pl.BlockSpec((1, d), fixed),
        ],
        out_specs=pl.BlockSpec((tm, d), row),
        out_shape=jax.ShapeDtypeStruct((t, d), F32),
        scratch_shapes=[pltpu.VMEM((2, TOP_K, tm * ROW_TILE, V7X_LANES), ys.dtype), pltpu.SemaphoreType.DMA((2,))],
        compiler_params=_params(("arbitrary",)),
        name="combine",
    )(dest_flat, dest_flat, topw, h1, ysh, ys, g2, b2)


def kernel(x, ln_emb_g, ln_emb_b, w_in, conv_w, conv_b, b_igate, b_fgate, mlstm_norm_g, lambda_q1, lambda_k1,
           lambda_q2, lambda_k2, rel_bias, w_out, ln1_g, ln1_b, w_router, router_bias, w_gate, w_up, w_down,
           ws_gate, ws_up, ws_down, ln2_g, ln2_b):
    bsz, s, d = x.shape
    assert bsz == 1 and w_in.shape[0] == DEPTH == 1
    x2 = x.reshape(s, d)
    row = lambda a: a.reshape(1, -1).astype(F32)

    w = w_in[0]
    c0 = 2 * M_QK + 2 * M_V
    c1 = c0 + 2 * MLSTM_HEADS
    c2 = c1 + D_QK
    c3 = c2 + D_QK
    w_main = jnp.concatenate([w[:, :c0], w[:, c1:c2], w[:, c3:]], axis=1).astype(BF16)
    w_k_t = w[:, c2:c3].T.astype(BF16)
    wi, wf = w[:, c0:c0 + MLSTM_HEADS], w[:, c0 + MLSTM_HEADS:c1]
    w_gates = jnp.zeros((d, MLSTM_HEADS, V7X_LANES), F32).at[:, :, 0].set(wi).at[:, :, 1].set(wf)
    w_gates = w_gates.reshape(d, MLSTM_HEADS * V7X_LANES).astype(BF16)
    w_gates_t = jnp.zeros((MLSTM_HEADS, V7X_SUBLANES, d), F32).at[:, 0].set(wi.T).at[:, 1].set(wf.T)
    w_gates_t = w_gates_t.reshape(MLSTM_HEADS * V7X_SUBLANES, d).astype(BF16)

    proj, gates, gates_t, k_t = _ln_proj(x2, row(ln_emb_g), row(ln_emb_b), w_main, w_gates, w_gates_t, w_k_t)

    hm = _mlstm(proj, gates, gates_t, conv_w[0], row(conv_b[0]), b_igate[0].astype(F32),
                b_fgate[0].astype(F32), row(mlstm_norm_g[0]))

    lam_vecs = jnp.stack([lambda_q1[0], lambda_k1[0], lambda_q2[0], lambda_k2[0]]).astype(F32)
    hd = _diff_attn(proj, k_t, lam_vecs, rel_bias.astype(F32).reshape(-1))

    wo = w_out[0].astype(BF16)
    h1, h1p, logits_t = _out_ln(hm, hd, x2, row(ln_emb_g), row(ln_emb_b), wo[:M_V], wo[M_V:], row(ln1_g[0]),
                           row(ln1_b[0]), w_router[0].T.astype(BF16))

    topi_t, pos_t, topw_t, counts = _route(logits_t, router_bias[0].reshape(-1, 1).astype(F32))
    off, vexp, vblk, vcnt, nvis, dest_t = _plan(counts.reshape(1, -1), counts, topi_t, pos_t)
    off_ext = jnp.concatenate([off.reshape(-1), jnp.full((1,), s * TOP_K, jnp.int32)])
    dest_flat = dest_t.T.reshape(-1)

    xs = _dispatch(dest_flat, h1p)
    ys, ysh = _experts(vexp.reshape(-1), vblk.reshape(-1), vcnt.reshape(-1), off_ext, nvis.reshape(-1), xs,
                       w_gate[0], w_up[0], w_down[0], h1p, ws_gate[0].astype(BF16), ws_up[0].astype(BF16),
                       ws_down[0].astype(BF16))
    out = _combine(dest_flat, topw_t.T, h1, ysh, ys, row(ln2_g[0]), row(ln2_b[0]))
    return out.reshape(bsz, s, d)
```

```python
import functools
import math

import jax
import jax.numpy as jnp
from jax import lax
from jax.experimental import pallas as pl
from jax.experimental.pallas import tpu as pltpu

F32 = jnp.float32
BF16 = jnp.bfloat16

DEPTH = 1
MLSTM_HEADS = 4
MLSTM_DQK = 128
MLSTM_DV = 256
CONV_WIDTH = 4
DIFF_HEADS = 8
DIFF_DH = 64
DIFF_DV = 2 * DIFF_DH
REL_BUCKETS = 32
REL_MAX_DIST = 128
N_EXPERTS = 256
TOP_K = 8
N_GROUPS = 8
TOPK_GROUPS = 4
D_EXPERT = 512
ROUTED_SCALE = 2.5
LN_EPS = 1e-5
ALPHA = (2 * DEPTH) ** 0.25
LAM_INIT = 0.8 - 0.6 * math.exp(-0.3 * 0)

M_QK = MLSTM_HEADS * MLSTM_DQK
M_V = MLSTM_HEADS * MLSTM_DV
D_QK = DIFF_HEADS * 2 * DIFF_DH
D_VW = DIFF_HEADS * DIFF_DV
PROJ_W = 2 * M_QK + 2 * M_V + D_QK + D_VW
D_MODEL = M_V + D_VW

V7X_LANES = 128
V7X_SUBLANES = 8
V7X_VMEM_LIMIT_BYTES = 56 * 1024 * 1024
ROW_TILE = D_MODEL // 2 // V7X_LANES

PROJ_TM = 1024
PROJ_TN = 512
MLSTM_L = 256
ATT_BK = 1024
ATT_BQ = 2048
ATT_STRIP = 256
OUT_TM = 512
ROUTE_TM = 512
MOE_TR = 256
DISPATCH_TM = 512
MOVE_TM = 256
SHARED_TM = 256
PLAN_TB = 512

NEG = -1e30
LOG2E = 1.4426950408889634


def _params(semantics, fusible_inputs=None):
    return pltpu.CompilerParams(dimension_semantics=semantics, vmem_limit_bytes=V7X_VMEM_LIMIT_BYTES,
                                allow_input_fusion=fusible_inputs)


def _layer_norm(x, g, b):
    mu = jnp.mean(x, -1, keepdims=True)
    xc = x - mu
    var = jnp.mean(xc * xc, -1, keepdims=True)
    return xc * lax.rsqrt(var + LN_EPS) * g + b


def _dot(a, b):
    return jnp.dot(a, b, preferred_element_type=F32)


def _dot_nt(a, b):
    return lax.dot_general(a, b, (((1,), (1,)), ((), ())), preferred_element_type=F32)


def _split3(a):
    a1 = a.astype(BF16)
    r1 = a - a1.astype(F32)
    a2 = r1.astype(BF16)
    a3 = (r1 - a2.astype(F32)).astype(BF16)
    return a1, a2, a3


def _pack_bf16_pairs(x):
    n = x.shape[1] // 2
    bits = lax.bitcast_convert_type(x.astype(BF16).astype(F32), jnp.uint32)
    return lax.shift_right_logical(bits[:, :n], jnp.uint32(16)) | (bits[:, n:] & jnp.uint32(0xFFFF0000))


def _unpack_bf16_pairs(u):
    lo = lax.bitcast_convert_type(lax.shift_left(u, jnp.uint32(16)), F32)
    hi = lax.bitcast_convert_type(u & jnp.uint32(0xFFFF0000), F32)
    return jnp.concatenate([lo, hi], axis=1)


def _store_row_tiles(ref, x, r0=0):
    m, n = x.shape[0], x.shape[1] // V7X_LANES
    for c in range(n):
        ref[pl.ds(r0 * n + c, m, stride=n), :] = x[:, c * V7X_LANES:(c + 1) * V7X_LANES]


def _load_row_tiles(ref, m, n, r0=0):
    return jnp.concatenate([ref[pl.ds(r0 * n + c, m, stride=n), :] for c in range(n)], axis=1)


def _log_sigmoid(x):
    return jnp.minimum(x, 0.0) - jnp.log(1.0 + jnp.exp(-jnp.abs(x)))


def _ln_proj_kernel(x_ref, g_ref, b_ref, w_ref, wg_ref, wgt_ref, wkt_ref, o_ref, gates_ref, gatest_ref, kt_ref,
                    hb_ref):
    @pl.when(pl.program_id(1) == 0)
    def _():
        hb = _layer_norm(x_ref[...], g_ref[...], b_ref[...]).astype(BF16)
        hb_ref[...] = hb
        gates_ref[...] = _dot(hb, wg_ref[...])
        gatest_ref[...] = _dot_nt(wgt_ref[...], hb)
        kt_ref[...] = _dot_nt(wkt_ref[...], hb).astype(kt_ref.dtype)

    o_ref[...] = _dot(hb_ref[...], w_ref[...]).astype(o_ref.dtype)


def _ln_proj(x2, g, b, w_main, w_gates, w_gates_t, w_k_t):
    s, d = x2.shape
    tm, tn = min(PROJ_TM, s), PROJ_TN
    ng = w_gates.shape[1]
    fixed = lambda i, j: (0, 0)
    return pl.pallas_call(
        _ln_proj_kernel,
        grid=(s // tm, PROJ_W // tn),
        in_specs=[
            pl.BlockSpec((tm, d), lambda i, j: (i, 0)),
            pl.BlockSpec((1, d), fixed),
            pl.BlockSpec((1, d), fixed),
            pl.BlockSpec((d, tn), lambda i, j: (0, j)),
            pl.BlockSpec((d, ng), fixed),
            pl.BlockSpec((w_gates_t.shape[0], d), fixed),
            pl.BlockSpec((D_QK, d), fixed),
        ],
        out_specs=[
            pl.BlockSpec((tm, tn), lambda i, j: (i, j)),
            pl.BlockSpec((tm, ng), lambda i, j: (i, 0)),
            pl.BlockSpec((w_gates_t.shape[0], tm), lambda i, j: (0, i)),
            pl.BlockSpec((D_QK, tm), lambda i, j: (0, i)),
        ],
        out_shape=[
            jax.ShapeDtypeStruct((s, PROJ_W), BF16),
            jax.ShapeDtypeStruct((s, ng), F32),
            jax.ShapeDtypeStruct((w_gates_t.shape[0], s), F32),
            jax.ShapeDtypeStruct((D_QK, s), BF16),
        ],
        scratch_shapes=[pltpu.VMEM((tm, d), BF16)],
        compiler_params=_params(("arbitrary", "arbitrary"), [False, False, False, True, True, True, True]),
        name="ln_proj",
    )(x2, g, b, w_main, w_gates, w_gates_t, w_k_t)


def _mlstm_kernel(bi_ref, bf_ref, mq_ref, mk_ref, mv_ref, mo_ref, gates_ref, gatest_ref,
                  cw_ref, cb_ref, ng_ref, o_ref, c_ref, n_ref, m_ref, tail_ref):
    L = mq_ref.shape[0]
    dk, dv = MLSTM_DQK, MLSTM_DV

    @pl.when(pl.program_id(0) == 0)
    def _():
        c_ref[...] = jnp.zeros_like(c_ref)
        n_ref[...] = jnp.zeros_like(n_ref)
        m_ref[...] = jnp.zeros_like(m_ref)
        tail_ref[...] = jnp.zeros_like(tail_ref)

    u = jnp.concatenate([mq_ref[...], mk_ref[...]], axis=1).astype(F32)
    tail = tail_ref[...]
    row8 = lax.broadcasted_iota(jnp.int32, (V7X_SUBLANES, u.shape[1]), 0)
    conv = cb_ref[...] + cw_ref[CONV_WIDTH - 1:CONV_WIDTH, :] * u
    for back in range(1, CONV_WIDTH):
        ur = pltpu.roll(u, back, 0)
        head = jnp.where(row8 < back, pltpu.roll(tail, back, 0), ur[:V7X_SUBLANES])
        shifted = jnp.concatenate([head, ur[V7X_SUBLANES:]], axis=0)
        conv = conv + cw_ref[CONV_WIDTH - 1 - back:CONV_WIDTH - back, :] * shifted
    tail_ref[...] = u[L - V7X_SUBLANES:]
    qk = conv * jax.nn.sigmoid(conv)
    q_all = qk[:, :M_QK]
    k_all = qk[:, M_QK:] * (dk ** -0.5)

    r_i = lax.broadcasted_iota(jnp.int32, (L, L), 0)
    c_i = lax.broadcasted_iota(jnp.int32, (L, L), 1)
    causal = c_i <= r_i
    tril = jnp.where(causal, 1.0, 0.0).astype(BF16)
    triu = jnp.where(r_i <= c_i, 1.0, 0.0).astype(BF16)

    for h in range(MLSTM_HEADS):
        q = q_all[:, h * dk:(h + 1) * dk]
        k = k_all[:, h * dk:(h + 1) * dk]
        qb, kb = q.astype(BF16), k.astype(BF16)
        vb = mv_ref[:, h * dv:(h + 1) * dv]
        b_i, b_f = bi_ref[h], bf_ref[h]

        gblk = gates_ref[:, h * V7X_LANES:(h + 1) * V7X_LANES]
        i_col = gblk[:, 0:1] + b_i
        lf_blk = _log_sigmoid(gblk + b_f)
        f1, f2, f3 = _split3(lf_blk)
        bcum_blk = _dot(tril, f1) + _dot(tril, f2) + _dot(tril, f3)
        b_col = bcum_blk[:, 1:2]
        gt = gatest_ref[h * V7X_SUBLANES:(h + 1) * V7X_SUBLANES, :]
        i_row = gt[0:1, :] + b_i
        lf_rows = _log_sigmoid(gt + b_f)
        g1, g2, g3 = _split3(lf_rows)
        b_row = (_dot(g1, triu) + _dot(g2, triu) + _dot(g3, triu))[1:2, :]

        m_prev = m_ref[h, 0:1, 0:1]
        dmat = jnp.where(causal, b_col - b_row + i_row, NEG)
        inter = b_col + m_prev
        m_t = jnp.maximum(inter, jnp.max(dmat, -1, keepdims=True))
        wts = jnp.exp(dmat - m_t)
        g = jnp.exp(inter - m_t)
        sqk = _dot_nt(qb, kb) * wts
        c_prev = c_ref[h]
        n_prev = n_ref[h, 0:1, :]
        num = g * _dot(qb, c_prev.astype(BF16)) + _dot(sqk.astype(BF16), vb)
        den = g * jnp.sum(q * n_prev, -1, keepdims=True) + jnp.sum(sqk, -1, keepdims=True)
        hh = num / jnp.maximum(jnp.abs(den), jnp.exp(-m_t))

        b_last = b_col[L - 1:L, :]
        w_last_row = b_last - b_row + i_row
        m_new = jnp.maximum(b_last + m_prev, jnp.max(w_last_row, -1, keepdims=True))
        decay = jnp.exp(b_last + m_prev - m_new)
        ws_col = jnp.exp(b_last - b_col + i_col - m_new)
        kw = k * ws_col
        c_ref[h] = decay * c_prev + _dot(kw.T.astype(BF16), vb)
        n_ref[h, 0:1, :] = decay * n_prev + jnp.sum(kw, 0, keepdims=True)
        m_ref[h] = jnp.broadcast_to(m_new, m_ref.shape[1:])

        mu = jnp.mean(hh, -1, keepdims=True)
        hc = hh - mu
        var = jnp.mean(hc * hc, -1, keepdims=True)
        hn = hc * lax.rsqrt(var + LN_EPS) * ng_ref[:, h * dv:(h + 1) * dv]
        og = jax.nn.sigmoid(mo_ref[:, h * dv:(h + 1) * dv].astype(F32))
        o_ref[:, h * dv:(h + 1) * dv] = (hn * og).astype(o_ref.dtype)


def _mlstm(proj, gates, gates_t, conv_w, conv_b, b_i, b_f, norm_g):
    s = proj.shape[0]
    L = min(MLSTM_L, s)
    smem = pl.BlockSpec(memory_space=pltpu.SMEM)
    return pl.pallas_call(
        _mlstm_kernel,
        grid=(s // L,),
        in_specs=[
            smem, smem,
            pl.BlockSpec((L, M_QK), lambda c: (c, 0)),
            pl.BlockSpec((L, M_QK), lambda c: (c, 1)),
            pl.BlockSpec((L, M_V), lambda c: (c, 1)),
            pl.BlockSpec((L, M_V), lambda c: (c, 2)),
            pl.BlockSpec((L, gates.shape[1]), lambda c: (c, 0)),
            pl.BlockSpec((gates_t.shape[0], L), lambda c: (0, c)),
            pl.BlockSpec((CONV_WIDTH, 2 * M_QK), lambda c: (0, 0)),
            pl.BlockSpec((1, 2 * M_QK), lambda c: (0, 0)),
            pl.BlockSpec((1, M_V), lambda c: (0, 0)),
        ],
        out_specs=pl.BlockSpec((L, M_V), lambda c: (c, 0)),
        out_shape=jax.ShapeDtypeStruct((s, M_V), BF16),
        scratch_shapes=[
            pltpu.VMEM((MLSTM_HEADS, MLSTM_DQK, MLSTM_DV), F32),
            pltpu.VMEM((MLSTM_HEADS, V7X_SUBLANES, MLSTM_DQK), F32),
            pltpu.VMEM((MLSTM_HEADS, V7X_SUBLANES, V7X_LANES), F32),
            pltpu.VMEM((V7X_SUBLANES, 2 * M_QK), F32),
        ],
        compiler_params=_params(("arbitrary",)),
        name="mlstm",
    )(b_i, b_f, proj, proj, proj, proj, gates, gates_t, conv_w, conv_b, norm_g)


def _t5_bias_tile(table_ref, h, offset):
    n_t = V7X_LANES
    r_i = lax.broadcasted_iota(jnp.int32, (n_t, n_t), 0)
    c_i = lax.broadcasted_iota(jnp.int32, (n_t, n_t), 1)
    n = jnp.maximum(offset + r_i - c_i, 0)
    max_exact = REL_BUCKETS // 2
    large = max_exact + (jnp.log(jnp.maximum(n, 1).astype(F32) / max_exact)
                         / math.log(REL_MAX_DIST / max_exact) * (REL_BUCKETS - max_exact)).astype(jnp.int32)
    large = jnp.minimum(large, REL_BUCKETS - 1)
    bucket = jnp.where(n < max_exact, n, large)
    out = jnp.zeros((n_t, n_t), F32)
    for b in range(REL_BUCKETS):
        out = jnp.where(bucket == b, table_ref[b * DIFF_HEADS + h], out)
    return out * LOG2E


def _diff_attn_kernel(qi_ref, kj_ref, q_ref, kt_ref, v_ref, lam_ref, table_ref, o_ref,
                      q1_ref, q2_ref, vx_ref, m_ref, acc_ref, bdiag_ref, bsub_ref):
    h = pl.program_id(0)
    p = pl.program_id(1)
    i = qi_ref[p]
    j = kj_ref[p]
    BQ = q_ref.shape[0]
    BK = v_ref.shape[0]
    n_half = BQ // BK
    SR = min(ATT_STRIP, BK)
    strips_per_half = BK // SR
    n_sub = BK // V7X_LANES
    c_far = table_ref[(REL_BUCKETS - 1) * DIFF_HEADS + h] * LOG2E

    @pl.when(p == 0)
    def _():
        p0 = _t5_bias_tile(table_ref, h, 0)
        p0 = jnp.where(lax.broadcasted_iota(jnp.int32, p0.shape, 1) <= lax.broadcasted_iota(jnp.int32, p0.shape, 0),
                       p0, NEG)
        p1 = _t5_bias_tile(table_ref, h, V7X_LANES)
        far = jnp.full((V7X_LANES, V7X_LANES), c_far, F32)
        neg = jnp.full((V7X_LANES, V7X_LANES), NEG, F32)
        for a in range(n_sub):
            for b in range(n_sub):
                tile = p0 if a == b else p1 if a == b + 1 else far if a > b else neg
                bdiag_ref[a * V7X_LANES:(a + 1) * V7X_LANES, b * V7X_LANES:(b + 1) * V7X_LANES] = tile
        bsub_ref[...] = jnp.full(bsub_ref.shape, c_far, F32)
        bsub_ref[0:V7X_LANES, BK - V7X_LANES:BK] = p1

    @pl.when(j == 0)
    def _():
        q = q_ref[...].astype(F32) * ((DIFF_DH ** -0.5) * LOG2E)
        lane = lax.broadcasted_iota(jnp.int32, q.shape, 1)
        q1_ref[...] = jnp.where(lane < DIFF_DH, q, 0.0).astype(BF16)
        q2_ref[...] = jnp.where(lane >= DIFF_DH, q, 0.0).astype(BF16)
        m_ref[...] = jnp.full(m_ref.shape, NEG, F32)
        acc_ref[...] = jnp.zeros_like(acc_ref)

    vx_ref[:, 0:DIFF_DV] = v_ref[...]
    vx_ref[:, DIFF_DV:] = jnp.ones((BK, DIFF_DV), BF16)

    def strip(r, kl, bias_tile=None):
        rows = slice(r * SR, (r + 1) * SR)
        kb = kt_ref[:, 0:kl]
        vb = vx_ref[0:kl, :]
        for a, qz_ref in enumerate((q1_ref, q2_ref)):
            s = _dot(qz_ref[rows, :], kb)
            m_old = m_ref[a, rows, :]
            if bias_tile is None:
                m_new = jnp.maximum(m_old, jnp.max(s, -1, keepdims=True) + c_far)
                pm = jnp.exp2(s - (m_new - c_far))
            else:
                s = s + bias_tile
                m_new = jnp.maximum(m_old, jnp.max(s, -1, keepdims=True))
                pm = jnp.exp2(s - m_new)
            alpha = jnp.exp2(m_old - m_new)
            acc_ref[a, rows, :] = alpha * acc_ref[a, rows, :] + _dot(pm.astype(BF16), vb)
            m_ref[a, rows, :] = m_new

    def key_block(delta):
        for r in range(n_half * strips_per_half):
            ro = (r % strips_per_half) * SR
            rel = None if delta is None else delta - r // strips_per_half
            if rel is None or rel <= -2:
                strip(r, BK)
            elif rel == -1:
                strip(r, BK, bsub_ref[...] if ro == 0 else None)
            elif rel == 0:
                strip(r, ro + SR, bdiag_ref[ro:ro + SR, 0:ro + SR])

    d = j - i * n_half

    @pl.when(d <= -2)
    def _():
        key_block(None)

    for delta in range(-1, n_half):
        @pl.when(d == delta)
        def _(delta=delta):
            key_block(delta)
            if delta == n_half - 1:
                lq1, lk1, lq2, lk2 = (lam_ref[t:t + 1, :] for t in range(4))
                lam = (jnp.exp(jnp.sum(lq1 * lk1, -1, keepdims=True))
                       - jnp.exp(jnp.sum(lq2 * lk2, -1, keepdims=True)) + LAM_INIT)
                o = (acc_ref[0, :, 0:DIFF_DV] / acc_ref[0, :, DIFF_DV:DIFF_DV + 1]
                     - lam * (acc_ref[1, :, 0:DIFF_DV] / acc_ref[1, :, DIFF_DV:DIFF_DV + 1]))
                o = o * lax.rsqrt(jnp.mean(o * o, -1, keepdims=True) + LN_EPS)
                o_ref[...] = (o * (1.0 - LAM_INIT)).astype(o_ref.dtype)


def _diff_attn(proj, k_t, lam_vecs, table_flat):
    s = proj.shape[0]
    BK = min(ATT_BK, s)
    BQ = min(ATT_BQ, s)
    n_half = BQ // BK
    pairs = [(i, j) for i in range(s // BQ) for j in range((i + 1) * n_half)]
    qi = jnp.asarray([a for a, _ in pairs], jnp.int32)
    kj = jnp.asarray([b for _, b in pairs], jnp.int32)
    q_blk = (2 * M_QK + 2 * M_V) // V7X_LANES
    v_blk = q_blk + D_QK // V7X_LANES
    SR = min(ATT_STRIP, BK)
    grid_spec = pltpu.PrefetchScalarGridSpec(
        num_scalar_prefetch=2,
        grid=(DIFF_HEADS, len(pairs)),
        in_specs=[
            pl.BlockSpec((BQ, V7X_LANES), lambda h, p, qi, kj: (qi[p], q_blk + h)),
            pl.BlockSpec((2 * DIFF_DH, BK), lambda h, p, qi, kj: (h, kj[p])),
            pl.BlockSpec((BK, V7X_LANES), lambda h, p, qi, kj: (kj[p], v_blk + h)),
            pl.BlockSpec((4, DIFF_DH), lambda h, p, qi, kj: (0, 0)),
            pl.BlockSpec(memory_space=pltpu.SMEM),
        ],
        out_specs=pl.BlockSpec((BQ, V7X_LANES), lambda h, p, qi, kj: (qi[p], h)),
        scratch_shapes=[
            pltpu.VMEM((BQ, V7X_LANES), BF16),
            pltpu.VMEM((BQ, V7X_LANES), BF16),
            pltpu.VMEM((BK, 2 * DIFF_DV), BF16),
            pltpu.VMEM((2, BQ, 1), F32),
            pltpu.VMEM((2, BQ, 2 * DIFF_DV), F32),
            pltpu.VMEM((BK, BK), F32),
            pltpu.VMEM((SR, BK), F32),
        ],
    )
    return pl.pallas_call(
        _diff_attn_kernel,
        grid_spec=grid_spec,
        out_shape=jax.ShapeDtypeStruct((s, D_VW), BF16),
        compiler_params=_params(("arbitrary", "arbitrary")),
        name="diff_attn",
    )(qi, kj, proj, k_t, proj, lam_vecs, table_flat)


def _out_ln_kernel(hm_ref, hd_ref, x_ref, g0_ref, b0_ref, wo1_ref, wo2_ref, g1_ref, b1_ref, wr_ref,
                   h1_ref, h1p_ref, logit_ref):
    mix = _dot(hm_ref[...], wo1_ref[...]) + _dot(hd_ref[...], wo2_ref[...])
    h0 = _layer_norm(x_ref[...], g0_ref[...], b0_ref[...])
    h1 = _layer_norm(ALPHA * h0 + mix, g1_ref[...], b1_ref[...])
    h1_ref[...] = h1
    _store_row_tiles(h1p_ref, _pack_bf16_pairs(h1))
    logit_ref[...] = _dot_nt(wr_ref[...], h1.astype(BF16))


def _out_ln(hm, hd, x2, g0, b0, wo1, wo2, g1, b1, wr):
    s, d = x2.shape
    tm = min(OUT_TM, s)
    row = lambda i: (i, 0)
    fixed = lambda i: (0, 0)
    return pl.pallas_call(
        _out_ln_kernel,
        grid=(s // tm,),
        in_specs=[
            pl.BlockSpec((tm, M_V), row), pl.BlockSpec((tm, D_VW), row), pl.BlockSpec((tm, d), row),
            pl.BlockSpec((1, d), fixed), pl.BlockSpec((1, d), fixed),
            pl.BlockSpec((M_V, d), fixed), pl.BlockSpec((D_VW, d), fixed),
            pl.BlockSpec((1, d), fixed), pl.BlockSpec((1, d), fixed),
            pl.BlockSpec((N_EXPERTS, d), fixed),
        ],
        out_specs=[pl.BlockSpec((tm, d), row), pl.BlockSpec((tm * ROW_TILE, V7X_LANES), row),
                   pl.BlockSpec((N_EXPERTS, tm), lambda i: (0, i))],
        out_shape=[jax.ShapeDtypeStruct((s, d), F32), jax.ShapeDtypeStruct((s * ROW_TILE, V7X_LANES), jnp.uint32),
                   jax.ShapeDtypeStruct((N_EXPERTS, s), F32)],
        compiler_params=_params(("arbitrary",), [False] * 5 + [True, True, False, False, True]),
        name="out_ln",
    )(hm, hd, x2, g0, b0, wo1, wo2, g1, b1, wr)


def _route_kernel(logit_ref, rb_ref, topi_ref, pos_ref, topw_ref, cnt_ref, carry_ref):
    n_e, tm = logit_ref.shape
    gsz = n_e // N_GROUPS
    ninf = -jnp.inf

    @pl.when(pl.program_id(0) == 0)
    def _():
        carry_ref[...] = jnp.zeros_like(carry_ref)

    sc = jax.nn.sigmoid(logit_ref[...])
    sel = sc + rb_ref[...]

    sel3 = sel.reshape(N_GROUPS, gsz, tm)
    in_grp = lax.broadcasted_iota(jnp.int32, sel3.shape, 1)
    m1 = jnp.max(sel3, 1, keepdims=True)
    i1 = jnp.min(jnp.where(sel3 == m1, in_grp, gsz), 1, keepdims=True)
    m2 = jnp.max(jnp.where(in_grp == i1, ninf, sel3), 1, keepdims=True)
    gscore = (m1 + m2).reshape(N_GROUPS, tm)
    gid = lax.broadcasted_iota(jnp.int32, gscore.shape, 0)
    beaten = jnp.zeros(gscore.shape, F32)
    for o in range(1, N_GROUPS):
        other = pltpu.roll(gscore, o, 0)
        wins = (other > gscore) | ((other == gscore) & (gid >= o))
        beaten = beaten + jnp.where(wins, 1.0, 0.0)
    keep = jnp.where(beaten < TOPK_GROUPS, 1.0, 0.0)
    keep3 = jnp.broadcast_to(keep.reshape(N_GROUPS, 1, tm), sel3.shape)
    masked = jnp.where(keep3 > 0.5, sel3, ninf).reshape(n_e, tm)

    eid = lax.broadcasted_iota(jnp.int32, (n_e, tm), 0)
    onehots, idxs, ws = [], [], []
    for _ in range(TOP_K):
        m = jnp.max(masked, 0, keepdims=True)
        idx = jnp.min(jnp.where(masked == m, eid, n_e), 0, keepdims=True)
        oh = eid == idx
        onehots.append(oh)
        idxs.append(idx)
        ws.append(jnp.sum(jnp.where(oh, sc, 0.0), 0, keepdims=True))
        masked = jnp.where(oh, ninf, masked)
    wsum = ws[0]
    for w in ws[1:]:
        wsum = wsum + w

    chosen = jnp.zeros((n_e, tm), F32)
    for oh in onehots:
        chosen = jnp.where(oh, 1.0, chosen)
    r_i = lax.broadcasted_iota(jnp.int32, (tm, tm), 0)
    c_i = lax.broadcasted_iota(jnp.int32, (tm, tm), 1)
    before = jnp.where(r_i < c_i, 1.0, 0.0).astype(BF16)
    rank = _dot(chosen.astype(BF16), before) + carry_ref[...]
    carry_new = carry_ref[...] + jnp.sum(chosen, 1, keepdims=True)
    carry_ref[...] = carry_new
    cnt_ref[...] = carry_new

    kid = lax.broadcasted_iota(jnp.int32, (TOP_K, tm), 0)
    topi = jnp.zeros((TOP_K, tm), jnp.int32)
    pos = jnp.zeros((TOP_K, tm), jnp.int32)
    topw = jnp.zeros((TOP_K, tm), F32)
    for k in range(TOP_K):
        pk = jnp.sum(jnp.where(onehots[k], rank, 0.0), 0, keepdims=True)
        topi = jnp.where(kid == k, idxs[k], topi)
        pos = jnp.where(kid == k, pk.astype(jnp.int32), pos)
        topw = jnp.where(kid == k, ws[k] / wsum * ROUTED_SCALE, topw)
    topi_ref[...] = topi
    pos_ref[...] = pos
    topw_ref[...] = topw


def _route(logits_t, router_bias_col):
    n_e, t = logits_t.shape
    tm = min(ROUTE_TM, t)
    col = lambda i: (0, i)
    fixed = lambda i: (0, 0)
    return pl.pallas_call(
        _route_kernel,
        grid=(t // tm,),
        in_specs=[pl.BlockSpec((n_e, tm), col), pl.BlockSpec((n_e, 1), fixed)],
        out_specs=[pl.BlockSpec((TOP_K, tm), col), pl.BlockSpec((TOP_K, tm), col),
                   pl.BlockSpec((TOP_K, tm), col), pl.BlockSpec((n_e, 1), fixed)],
        out_shape=[jax.ShapeDtypeStruct((TOP_K, t), jnp.int32), jax.ShapeDtypeStruct((TOP_K, t), jnp.int32),
                   jax.ShapeDtypeStruct((TOP_K, t), F32), jax.ShapeDtypeStruct((n_e, 1), F32)],
        scratch_shapes=[pltpu.VMEM((n_e, 1), F32)],
        compiler_params=_params(("arbitrary",)),
        name="route",
    )(logits_t, router_bias_col)


def _plan_kernel(cnt_ref, cntc_ref, topi_ref, pos_ref, off_ref, vexp_ref, vblk_ref, vcnt_ref, nvis_ref, dest_ref):
    nv = vexp_ref.shape[0]
    tr = float(MOE_TR)
    cnt = jnp.broadcast_to(cnt_ref[...], (V7X_SUBLANES, N_EXPERTS))
    r_i = lax.broadcasted_iota(jnp.int32, (N_EXPERTS, N_EXPERTS), 0)
    c_i = lax.broadcasted_iota(jnp.int32, (N_EXPERTS, N_EXPERTS), 1)
    upper = jnp.where(r_i <= c_i, 1.0, 0.0).astype(BF16)

    def cumsum_lanes(a):
        a1, a2, a3 = _split3(a)
        return _dot(a1, upper) + _dot(a2, upper) + _dot(a3, upper)

    end = cumsum_lanes(cnt)
    start = end - cnt
    first_blk = jnp.floor(start / tr)
    last_blk = jnp.floor((end - 1.0) / tr)
    nvis_e = jnp.where(cnt > 0.0, last_blk - first_blk + 1.0, 0.0)
    vend = cumsum_lanes(nvis_e)
    vstart = vend - nvis_e
    off_ref[...] = start[0:1, :].astype(jnp.int32)
    nvis_ref[...] = vend[0:1, N_EXPERTS - 1:N_EXPERTS].astype(jnp.int32)

    v = lax.broadcasted_iota(jnp.int32, (nv, N_EXPERTS), 0).astype(F32)
    ve = jnp.broadcast_to(vend[0:1, :], (nv, N_EXPERTS))
    expert = jnp.sum(jnp.where(ve <= v, 1.0, 0.0), -1, keepdims=True)
    expert = jnp.minimum(expert, N_EXPERTS - 1.0)
    lane = lax.broadcasted_iota(jnp.int32, (nv, N_EXPERTS), 1).astype(F32)
    mine = lane == expert
    fb = jnp.sum(jnp.where(mine, jnp.broadcast_to(first_blk[0:1, :], (nv, N_EXPERTS)), 0.0), -1, keepdims=True)
    vs = jnp.sum(jnp.where(mine, jnp.broadcast_to(vstart[0:1, :], (nv, N_EXPERTS)), 0.0), -1, keepdims=True)
    vc = jnp.sum(jnp.where(mine, jnp.broadcast_to(nvis_e[0:1, :], (nv, N_EXPERTS)), 0.0), -1, keepdims=True)
    vexp_ref[...] = expert.astype(jnp.int32)
    vblk_ref[...] = (fb + (v[:, 0:1] - vs)).astype(jnp.int32)
    vcnt_ref[...] = vc.astype(jnp.int32)

    lower = jnp.where(c_i < r_i, 1.0, 0.0).astype(BF16)
    c1, c2, c3 = _split3(jnp.broadcast_to(cntc_ref[...], (N_EXPERTS, V7X_LANES)))
    start_col = (_dot(lower, c1) + _dot(lower, c2) + _dot(lower, c3))[:, 0:1]
    n_tok = topi_ref.shape[1]
    tb = min(PLAN_TB, n_tok)
    eid = lax.broadcasted_iota(jnp.int32, (N_EXPERTS, tb), 0)
    kid = lax.broadcasted_iota(jnp.int32, (TOP_K, tb), 0)

    def dest_block(i, carry):
        cols = pl.ds(pl.multiple_of(i * tb, tb), tb)
        ti = topi_ref[:, cols]
        first_row = jnp.zeros((TOP_K, tb), F32)
        for k in range(TOP_K):
            fr = jnp.sum(jnp.where(eid == ti[k:k + 1, :], start_col, 0.0), 0, keepdims=True)
            first_row = jnp.where(kid == k, fr, first_row)
        dest_ref[:, cols] = first_row.astype(jnp.int32) + pos_ref[:, cols]
        return carry

    lax.fori_loop(0, n_tok // tb, dest_block, 0)


def _plan(counts_row, counts_col, topi_t, pos_t):
    n_rows = topi_t.shape[0] * topi_t.shape[1]
    nv = n_rows // MOE_TR + N_EXPERTS
    return pl.pallas_call(
        _plan_kernel,
        out_shape=[jax.ShapeDtypeStruct((1, N_EXPERTS), jnp.int32), jax.ShapeDtypeStruct((nv, 1), jnp.int32),
                   jax.ShapeDtypeStruct((nv, 1), jnp.int32), jax.ShapeDtypeStruct((nv, 1), jnp.int32),
                   jax.ShapeDtypeStruct((1, 1), jnp.int32), jax.ShapeDtypeStruct(topi_t.shape, jnp.int32)],
        compiler_params=pltpu.CompilerParams(vmem_limit_bytes=V7X_VMEM_LIMIT_BYTES),
        name="plan",
    )(counts_row, counts_col, topi_t, pos_t)


def _row_tile(ref, r):
    return ref.at[pl.ds(pl.multiple_of(r * ROW_TILE, ROW_TILE), ROW_TILE)]


def _dispatch_kernel(dest_ref, x_ref, xs_ref, sem):
    tm = x_ref.shape[0] // ROW_TILE

    def issue(t, carry):
        for k in range(TOP_K):
            d = dest_ref[t * TOP_K + k]
            pltpu.make_async_copy(_row_tile(x_ref, t), _row_tile(xs_ref, d), sem).start(priority=k % 2)
        return carry

    lax.fori_loop(0, tm, issue, 0)
    for _ in range(TOP_K):
        pltpu.make_async_copy(x_ref, xs_ref.at[pl.ds(0, tm * ROW_TILE)], sem).wait()


def _dispatch(dest_flat, rows):
    t = rows.shape[0] // ROW_TILE
    tm = min(DISPATCH_TM, t)
    return pl.pallas_call(
        _dispatch_kernel,
        grid=(t // tm,),
        in_specs=[
            pl.BlockSpec((tm * TOP_K,), lambda i: (i,), memory_space=pltpu.SMEM),
            pl.BlockSpec((tm * ROW_TILE, V7X_LANES), lambda i: (i, 0)),
        ],
        out_specs=pl.BlockSpec(memory_space=pl.ANY),
        out_shape=jax.ShapeDtypeStruct((t * TOP_K * ROW_TILE, V7X_LANES), rows.dtype),
        scratch_shapes=[pltpu.SemaphoreType.DMA(())],
        compiler_params=_params(("arbitrary",)),
        name="dispatch",
    )(dest_flat, rows)


def _experts_kernel(n_shared_steps, vexp_ref, vblk_ref, vcnt_ref, off_ref, nvis_ref, xs_ref, wg_hbm, wu_hbm,
                    wd_hbm, hp_ref, sg_ref, su_ref, sd_ref, ys_ref, ysh_ref,
                    wg_buf, wu_buf, wd_buf, acc_ref, slot_ref, sem):
    v = pl.program_id(0)
    tr = xs_ref.shape[0] // ROW_TILE
    nvis = nvis_ref[0]

    @pl.when(v < n_shared_steps)
    def _():
        hb = _unpack_bf16_pairs(_load_row_tiles(hp_ref, hp_ref.shape[0] // ROW_TILE, ROW_TILE)).astype(BF16)
        gate = _dot(hb, sg_ref[...])
        up = _dot(hb, su_ref[...])
        ysh_ref[...] = _dot((gate * jax.nn.sigmoid(gate) * up).astype(BF16), sd_ref[...])

    half = wd_buf.shape[1] // 2

    def weight_copies(e, slot):
        return ((pltpu.make_async_copy(wg_hbm.at[e], wg_buf.at[slot], sem.at[slot, 0]), 1),
                (pltpu.make_async_copy(wu_hbm.at[e], wu_buf.at[slot], sem.at[slot, 1]), 1),
                (pltpu.make_async_copy(wd_hbm.at[e, pl.ds(0, half)], wd_buf.at[slot, pl.ds(0, half)],
                                       sem.at[slot, 2]), 1),
                (pltpu.make_async_copy(wd_hbm.at[e, pl.ds(half, half)], wd_buf.at[slot, pl.ds(half, half)],
                                       sem.at[slot, 3]), 1))

    @pl.when(v < nvis)
    def _():
        e = vexp_ref[v]
        blk = vblk_ref[v]
        first_of_expert = jnp.logical_or(v == 0, vexp_ref[jnp.maximum(v - 1, 0)] != e)

        @pl.when(v == 0)
        def _():
            slot_ref[0] = 0
            for c, queue in weight_copies(e, 0):
                c.start(priority=queue)

        @pl.when(jnp.logical_and(first_of_expert, v > 0))
        def _():
            slot_ref[0] = 1 - slot_ref[0]

        slot = slot_ref[0]

        @pl.when(first_of_expert)
        def _():
            nxt = v + vcnt_ref[v]

            @pl.when(nxt < nvis)
            def _():
                for c, queue in weight_copies(vexp_ref[jnp.minimum(nxt, vexp_ref.shape[0] - 1)], 1 - slot):
                    c.start(priority=queue)

            for c, _ in weight_copies(e, slot):
                c.wait()

        lo = off_ref[e] - blk * tr
        hi = off_ref[e + 1] - blk * tr
        first_of_block = jnp.logical_or(v == 0, vblk_ref[jnp.maximum(v - 1, 0)] != blk)

        @pl.when(first_of_block)
        def _():
            acc_ref[...] = jnp.zeros_like(acc_ref)

        def swiglu_rows(r0, n):
            row = r0 + lax.broadcasted_iota(jnp.int32, (n, 1), 0)
            mine = (row >= lo) & (row < hi)
            xb = _unpack_bf16_pairs(_load_row_tiles(xs_ref, n, ROW_TILE, r0)).astype(BF16)
            gate = _dot(xb, wg_buf[slot].astype(BF16))
            up = _dot(xb, wu_buf[slot].astype(BF16))
            act = (gate * jax.nn.sigmoid(gate) * up).astype(BF16)
            acc_ref[r0:r0 + n, :] += jnp.where(mine, _dot(act, wd_buf[slot].astype(BF16)), 0.0)

        hr = tr // 2
        top_only = hi <= hr
        bottom_only = lo >= hr

        @pl.when(top_only)
        def _():
            swiglu_rows(0, hr)

        @pl.when(bottom_only)
        def _():
            swiglu_rows(hr, hr)

        @pl.when(jnp.logical_not(jnp.logical_or(top_only, bottom_only)))
        def _():
            swiglu_rows(0, tr)

        last_of_block = jnp.logical_or(v + 1 >= nvis, vblk_ref[jnp.minimum(v + 1, vblk_ref.shape[0] - 1)] != blk)

        @pl.when(last_of_block)
        def _():
            _store_row_tiles(ys_ref, _pack_bf16_pairs(acc_ref[...]))


def _experts(vexp, vblk, vcnt, off_ext, nvis, xs, w_gate, w_up, w_down, h1p, sg, su, sd):
    n_rows = xs.shape[0] // ROW_TILE
    d = D_MODEL
    nv = vexp.shape[0]
    t = h1p.shape[0] // ROW_TILE
    tm = min(SHARED_TM, t)
    n_shared_steps = t // tm
    assert n_shared_steps <= n_rows // MOE_TR

    def block_of(v, ve, vb, vc, off, nvis):
        return (vb[jnp.minimum(v, nvis[0] - 1)], 0)

    def shared_block(v, ve, vb, vc, off, nvis):
        return (jnp.minimum(v, n_shared_steps - 1), 0)

    fixed = lambda v, ve, vb, vc, off, nvis: (0, 0)
    resident = dict(pipeline_mode=pl.Buffered(1))
    grid_spec = pltpu.PrefetchScalarGridSpec(
        num_scalar_prefetch=5,
        grid=(nv,),
        in_specs=[
            pl.BlockSpec((MOE_TR * ROW_TILE, V7X_LANES), block_of),
            pl.BlockSpec(memory_space=pl.ANY),
            pl.BlockSpec(memory_space=pl.ANY),
            pl.BlockSpec(memory_space=pl.ANY),
            pl.BlockSpec((tm * ROW_TILE, V7X_LANES), shared_block),
            pl.BlockSpec((d, D_EXPERT), fixed, **resident),
            pl.BlockSpec((d, D_EXPERT), fixed, **resident),
            pl.BlockSpec((D_EXPERT, d), fixed, **resident),
        ],
        out_specs=[pl.BlockSpec((MOE_TR * ROW_TILE, V7X_LANES), block_of), pl.BlockSpec((tm, d), shared_block)],
        scratch_shapes=[
            pltpu.VMEM((2, d, D_EXPERT), F32),
            pltpu.VMEM((2, d, D_EXPERT), F32),
            pltpu.VMEM((2, D_EXPERT, d), F32),
            pltpu.VMEM((MOE_TR, d), F32),
            pltpu.SMEM((1,), jnp.int32),
            pltpu.SemaphoreType.DMA((2, 4)),
        ],
    )
    return pl.pallas_call(
        functools.partial(_experts_kernel, n_shared_steps),
        grid_spec=grid_spec,
        out_shape=[jax.ShapeDtypeStruct(xs.shape, jnp.uint32), jax.ShapeDtypeStruct((t, d), F32)],
        compiler_params=_params(("arbitrary",)),
        name="experts",
    )(vexp, vblk, vcnt, off_ext, nvis, xs, w_gate, w_up, w_down, h1p, sg, su, sd)


def _combine_kernel(dest_ref, dest_next_ref, topw_ref, h1_ref, ysh_ref, ys_ref, g2_ref, b2_ref, o_ref,
                    buf_ref, sem):
    i = pl.program_id(0)
    tm = h1_ref.shape[0]
    slot = lax.rem(i, 2)

    def issue_block(d_ref, s):
        def issue(t, carry):
            for k in range(TOP_K):
                d = d_ref[t * TOP_K + k]
                pltpu.make_async_copy(_row_tile(ys_ref, d), _row_tile(buf_ref.at[s, k], t),
                                      sem.at[s]).start(priority=k % 2)
            return carry

        lax.fori_loop(0, tm, issue, 0)

    @pl.when(i == 0)
    def _():
        issue_block(dest_ref, 0)

    @pl.when(i + 1 < pl.num_programs(0))
    def _():
        issue_block(dest_next_ref, 1 - slot)

    y = ALPHA * h1_ref[...] + ysh_ref[...]
    for k in range(TOP_K):
        pltpu.make_async_copy(ys_ref.at[pl.ds(0, tm * ROW_TILE)], buf_ref.at[slot, k], sem.at[slot]).wait()
    w = topw_ref[...]
    for k in range(TOP_K):
        y = y + _unpack_bf16_pairs(_load_row_tiles(buf_ref.at[slot, k], tm, ROW_TILE)) * w[:, k:k + 1]
    o_ref[...] = _layer_norm(y, g2_ref[...], b2_ref[...])


def _combine(dest_flat, topw, h1, ysh, ys, g2, b2):
    t, d = h1.shape
    tm = min(MOVE_TM, t)
    nb = t // tm
    row = lambda i: (i, 0)
    fixed = lambda i: (0, 0)
    return pl.pallas_call(
        _combine_kernel,
        grid=(nb,),
        in_specs=[
            pl.BlockSpec((tm * TOP_K,), lambda i: (i,), memory_space=pltpu.SMEM),
            pl.BlockSpec((tm * TOP_K,), lambda i: (jnp.minimum(i + 1, nb - 1),), memory_space=pltpu.SMEM),
            pl.BlockSpec((tm, TOP_K), row),
            pl.BlockSpec((tm, d), row),
            pl.BlockSpec((tm, d), row),
            pl.BlockSpec(memory_space=pl.ANY),
            pl.BlockSpec((1, d), fixed), pl.BlockSpec((1, d), fixed),
        ],
        out_specs=pl.BlockSpec((tm, d), row),
        out_shape=jax.ShapeDtypeStruct((t, d), F32),
        scratch_shapes=[pltpu.VMEM((2, TOP_K, tm * ROW_TILE, V7X_LANES), ys.dtype), pltpu.SemaphoreType.DMA((2,))],
        compiler_params=_params(("arbitrary",)),
        name="combine",
    )(dest_flat, dest_flat, topw, h1, ysh, ys, g2, b2)


def kernel(x, ln_emb_g, ln_emb_b, w_in, conv_w, conv_b, b_igate, b_fgate, mlstm_norm_g, lambda_q1, lambda_k1,
           lambda_q2, lambda_k2, rel_bias, w_out, ln1_g, ln1_b, w_router, router_bias, w_gate, w_up, w_down,
           ws_gate, ws_up, ws_down, ln2_g, ln2_b):
    bsz, s, d = x.shape
    assert bsz == 1 and w_in.shape[0] == DEPTH == 1
    x2 = x.reshape(s, d)
    row = lambda a: a.reshape(1, -1).astype(F32)

    w = w_in[0]
    c0 = 2 * M_QK + 2 * M_V
    c1 = c0 + 2 * MLSTM_HEADS
    c2 = c1 + D_QK
    c3 = c2 + D_QK
    w_main = jnp.concatenate([w[:, :c0], w[:, c1:c2], w[:, c3:]], axis=1).astype(BF16)
    w_k_t = w[:, c2:c3].T.astype(BF16)
    wi, wf = w[:, c0:c0 + MLSTM_HEADS], w[:, c0 + MLSTM_HEADS:c1]
    w_gates = jnp.zeros((d, MLSTM_HEADS, V7X_LANES), F32).at[:, :, 0].set(wi).at[:, :, 1].set(wf)
    w_gates = w_gates.reshape(d, MLSTM_HEADS * V7X_LANES).astype(BF16)
    w_gates_t = jnp.zeros((MLSTM_HEADS, V7X_SUBLANES, d), F32).at[:, 0].set(wi.T).at[:, 1].set(wf.T)
    w_gates_t = w_gates_t.reshape(MLSTM_HEADS * V7X_SUBLANES, d).astype(BF16)

    proj, gates, gates_t, k_t = _ln_proj(x2, row(ln_emb_g), row(ln_emb_b), w_main, w_gates, w_gates_t, w_k_t)

    hm = _mlstm(proj, gates, gates_t, conv_w[0], row(conv_b[0]), b_igate[0].astype(F32),
                b_fgate[0].astype(F32), row(mlstm_norm_g[0]))

    lam_vecs = jnp.stack([lambda_q1[0], lambda_k1[0], lambda_q2[0], lambda_k2[0]]).astype(F32)
    hd = _diff_attn(proj, k_t, lam_vecs, rel_bias.astype(F32).reshape(-1))

    wo = w_out[0].astype(BF16)
    h1, h1p, logits_t = _out_ln(hm, hd, x2, row(ln_emb_g), row(ln_emb_b), wo[:M_V], wo[M_V:], row(ln1_g[0]),
                           row(ln1_b[0]), w_router[0].T.astype(BF16))

    topi_t, pos_t, topw_t, counts = _route(logits_t, router_bias[0].reshape(-1, 1).astype(F32))
    off, vexp, vblk, vcnt, nvis, dest_t = _plan(counts.reshape(1, -1), counts, topi_t, pos_t)
    off_ext = jnp.concatenate([off.reshape(-1), jnp.full((1,), s * TOP_K, jnp.int32)])
    dest_flat = dest_t.T.reshape(-1)

    xs = _dispatch(dest_flat, h1p)
    ys, ysh = _experts(vexp.reshape(-1), vblk.reshape(-1), vcnt.reshape(-1), off_ext, nvis.reshape(-1), xs,
                       w_gate[0], w_up[0], w_down[0], h1p, ws_gate[0].astype(BF16), ws_up[0].astype(BF16),
                       ws_down[0].astype(BF16))
    out = _combine(dest_flat, topw_t.T, h1, ysh, ys, row(ln2_g[0]), row(ln2_b[0]))
    return out.reshape(bsz, s, d)
```

```python
import functools
import math

import jax
import jax.numpy as jnp
from jax import lax
from jax.experimental import pallas as pl
from jax.experimental.pallas import tpu as pltpu

F32 = jnp.float32
BF16 = jnp.bfloat16

DEPTH = 1
MLSTM_HEADS = 4
MLSTM_DQK = 128
MLSTM_DV = 256
CONV_WIDTH = 4
DIFF_HEADS = 8
DIFF_DH = 64
DIFF_DV = 2 * DIFF_DH
REL_BUCKETS = 32
REL_MAX_DIST = 128
N_EXPERTS = 256
TOP_K = 8
N_GROUPS = 8
TOPK_GROUPS = 4
D_EXPERT = 512
ROUTED_SCALE = 2.5
LN_EPS = 1e-5
ALPHA = (2 * DEPTH) ** 0.25
LAM_INIT = 0.8 - 0.6 * math.exp(-0.3 * 0)

M_QK = MLSTM_HEADS * MLSTM_DQK
M_V = MLSTM_HEADS * MLSTM_DV
D_QK = DIFF_HEADS * 2 * DIFF_DH
D_VW = DIFF_HEADS * DIFF_DV
PROJ_W = 2 * M_QK + 2 * M_V + D_QK + D_VW
D_MODEL = M_V + D_VW

V7X_LANES = 128
V7X_SUBLANES = 8
V7X_VMEM_LIMIT_BYTES = 56 * 1024 * 1024
ROW_TILE = D_MODEL // 2 // V7X_LANES

PROJ_TM = 1024
PROJ_TN = 512
MLSTM_L = 256
ATT_BK = 1024
ATT_BQ = 2048
ATT_STRIP = 256
OUT_TM = 512
ROUTE_TM = 512
MOE_TR = 256
DISPATCH_TM = 512
MOVE_TM = 256
SHARED_TM = 256
PLAN_TB = 512

NEG = -1e30
LOG2E = 1.4426950408889634


def _params(semantics):
    return pltpu.CompilerParams(dimension_semantics=semantics, vmem_limit_bytes=V7X_VMEM_LIMIT_BYTES)


def _layer_norm(x, g, b):
    mu = jnp.mean(x, -1, keepdims=True)
    xc = x - mu
    var = jnp.mean(xc * xc, -1, keepdims=True)
    return xc * lax.rsqrt(var + LN_EPS) * g + b


def _dot(a, b):
    return jnp.dot(a, b, preferred_element_type=F32)


def _dot_nt(a, b):
    return lax.dot_general(a, b, (((1,), (1,)), ((), ())), preferred_element_type=F32)


def _split3(a):
    a1 = a.astype(BF16)
    r1 = a - a1.astype(F32)
    a2 = r1.astype(BF16)
    a3 = (r1 - a2.astype(F32)).astype(BF16)
    return a1, a2, a3


def _pack_bf16_pairs(x):
    n = x.shape[1] // 2
    bits = lax.bitcast_convert_type(x.astype(BF16).astype(F32), jnp.uint32)
    return lax.shift_right_logical(bits[:, :n], jnp.uint32(16)) | (bits[:, n:] & jnp.uint32(0xFFFF0000))


def _unpack_bf16_pairs(u):
    lo = lax.bitcast_convert_type(lax.shift_left(u, jnp.uint32(16)), F32)
    hi = lax.bitcast_convert_type(u & jnp.uint32(0xFFFF0000), F32)
    return jnp.concatenate([lo, hi], axis=1)


def _store_row_tiles(ref, x, r0=0):
    m, n = x.shape[0], x.shape[1] // V7X_LANES
    for c in range(n):
        ref[pl.ds(r0 * n + c, m, stride=n), :] = x[:, c * V7X_LANES:(c + 1) * V7X_LANES]


def _load_row_tiles(ref, m, n, r0=0):
    return jnp.concatenate([ref[pl.ds(r0 * n + c, m, stride=n), :] for c in range(n)], axis=1)


def _log_sigmoid(x):
    return jnp.minimum(x, 0.0) - jnp.log(1.0 + jnp.exp(-jnp.abs(x)))


def _ln_proj_kernel(x_ref, g_ref, b_ref, w_ref, wg_ref, wgt_ref, wkt_ref, o_ref, gates_ref, gatest_ref, kt_ref,
                    hb_ref):
    @pl.when(pl.program_id(1) == 0)
    def _():
        hb = _layer_norm(x_ref[...], g_ref[...], b_ref[...]).astype(BF16)
        hb_ref[...] = hb
        gates_ref[...] = _dot(hb, wg_ref[...])
        gatest_ref[...] = _dot_nt(wgt_ref[...], hb)
        kt_ref[...] = _dot(hb, wkt_ref[...]).T.astype(kt_ref.dtype)

    o_ref[...] = _dot(hb_ref[...], w_ref[...]).astype(o_ref.dtype)


def _ln_proj(x2, g, b, w_main, w_gates, w_gates_t, w_k_t):
    s, d = x2.shape
    tm, tn = min(PROJ_TM, s), PROJ_TN
    ng = w_gates.shape[1]
    fixed = lambda i, j: (0, 0)
    return pl.pallas_call(
        _ln_proj_kernel,
        grid=(s // tm, PROJ_W // tn),
        in_specs=[
            pl.BlockSpec((tm, d), lambda i, j: (i, 0)),
            pl.BlockSpec((1, d), fixed),
            pl.BlockSpec((1, d), fixed),
            pl.BlockSpec((d, tn), lambda i, j: (0, j)),
            pl.BlockSpec((d, ng), fixed),
            pl.BlockSpec((w_gates_t.shape[0], d), fixed),
            pl.BlockSpec((d, D_QK), fixed),
        ],
        out_specs=[
            pl.BlockSpec((tm, tn), lambda i, j: (i, j)),
            pl.BlockSpec((tm, ng), lambda i, j: (i, 0)),
            pl.BlockSpec((w_gates_t.shape[0], tm), lambda i, j: (0, i)),
            pl.BlockSpec((D_QK, tm), lambda i, j: (0, i)),
        ],
        out_shape=[
            jax.ShapeDtypeStruct((s, PROJ_W), BF16),
            jax.ShapeDtypeStruct((s, ng), F32),
            jax.ShapeDtypeStruct((w_gates_t.shape[0], s), F32),
            jax.ShapeDtypeStruct((D_QK, s), BF16),
        ],
        scratch_shapes=[pltpu.VMEM((tm, d), BF16)],
        compiler_params=_params(("arbitrary", "arbitrary")),
        name="ln_proj",
    )(x2, g, b, w_main, w_gates, w_gates_t, w_k_t)


def _mlstm_kernel(bi_ref, bf_ref, mq_ref, mk_ref, mv_ref, mo_ref, gates_ref, gatest_ref,
                  cw_ref, cb_ref, ng_ref, o_ref, c_ref, n_ref, m_ref, tail_ref):
    L = mq_ref.shape[0]
    dk, dv = MLSTM_DQK, MLSTM_DV

    @pl.when(pl.program_id(0) == 0)
    def _():
        c_ref[...] = jnp.zeros_like(c_ref)
        n_ref[...] = jnp.zeros_like(n_ref)
        m_ref[...] = jnp.zeros_like(m_ref)
        tail_ref[...] = jnp.zeros_like(tail_ref)

    u = jnp.concatenate([mq_ref[...], mk_ref[...]], axis=1).astype(F32)
    tail = tail_ref[...]
    row8 = lax.broadcasted_iota(jnp.int32, (V7X_SUBLANES, u.shape[1]), 0)
    conv = cb_ref[...] + cw_ref[CONV_WIDTH - 1:CONV_WIDTH, :] * u
    for back in range(1, CONV_WIDTH):
        ur = pltpu.roll(u, back, 0)
        head = jnp.where(row8 < back, pltpu.roll(tail, back, 0), ur[:V7X_SUBLANES])
        shifted = jnp.concatenate([head, ur[V7X_SUBLANES:]], axis=0)
        conv = conv + cw_ref[CONV_WIDTH - 1 - back:CONV_WIDTH - back, :] * shifted
    tail_ref[...] = u[L - V7X_SUBLANES:]
    qk = conv * jax.nn.sigmoid(conv)
    q_all = qk[:, :M_QK]
    k_all = qk[:, M_QK:] * (dk ** -0.5)

    r_i = lax.broadcasted_iota(jnp.int32, (L, L), 0)
    c_i = lax.broadcasted_iota(jnp.int32, (L, L), 1)
    causal = c_i <= r_i
    tril = jnp.where(causal, 1.0, 0.0).astype(BF16)
    triu = jnp.where(r_i <= c_i, 1.0, 0.0).astype(BF16)

    for h in range(MLSTM_HEADS):
        q = q_all[:, h * dk:(h + 1) * dk]
        k = k_all[:, h * dk:(h + 1) * dk]
        qb, kb = q.astype(BF16), k.astype(BF16)
        vb = mv_ref[:, h * dv:(h + 1) * dv]
        b_i, b_f = bi_ref[h], bf_ref[h]

        gblk = gates_ref[:, h * V7X_LANES:(h + 1) * V7X_LANES]
        i_col = gblk[:, 0:1] + b_i
        lf_blk = _log_sigmoid(gblk + b_f)
        f1, f2, f3 = _split3(lf_blk)
        bcum_blk = _dot(tril, f1) + _dot(tril, f2) + _dot(tril, f3)
        b_col = bcum_blk[:, 1:2]
        gt = gatest_ref[h * V7X_SUBLANES:(h + 1) * V7X_SUBLANES, :]
        i_row = gt[0:1, :] + b_i
        lf_rows = _log_sigmoid(gt + b_f)
        g1, g2, g3 = _split3(lf_rows)
        b_row = (_dot(g1, triu) + _dot(g2, triu) + _dot(g3, triu))[1:2, :]

        m_prev = m_ref[h, 0:1, 0:1]
        dmat = jnp.where(causal, b_col - b_row + i_row, NEG)
        inter = b_col + m_prev
        m_t = jnp.maximum(inter, jnp.max(dmat, -1, keepdims=True))
        wts = jnp.exp(dmat - m_t)
        g = jnp.exp(inter - m_t)
        sqk = _dot_nt(qb, kb) * wts
        c_prev = c_ref[h]
        n_prev = n_ref[h, 0:1, :]
        num = g * _dot(qb, c_prev.astype(BF16)) + _dot(sqk.astype(BF16), vb)
        den = g * jnp.sum(q * n_prev, -1, keepdims=True) + jnp.sum(sqk, -1, keepdims=True)
        hh = num / jnp.maximum(jnp.abs(den), jnp.exp(-m_t))

        b_last = b_col[L - 1:L, :]
        w_last_row = b_last - b_row + i_row
        m_new = jnp.maximum(b_last + m_prev, jnp.max(w_last_row, -1, keepdims=True))
        decay = jnp.exp(b_last + m_prev - m_new)
        ws_col = jnp.exp(b_last - b_col + i_col - m_new)
        kw = k * ws_col
        c_ref[h] = decay * c_prev + _dot(kw.T.astype(BF16), vb)
        n_ref[h, 0:1, :] = decay * n_prev + jnp.sum(kw, 0, keepdims=True)
        m_ref[h] = jnp.broadcast_to(m_new, m_ref.shape[1:])

        mu = jnp.mean(hh, -1, keepdims=True)
        hc = hh - mu
        var = jnp.mean(hc * hc, -1, keepdims=True)
        hn = hc * lax.rsqrt(var + LN_EPS) * ng_ref[:, h * dv:(h + 1) * dv]
        og = jax.nn.sigmoid(mo_ref[:, h * dv:(h + 1) * dv].astype(F32))
        o_ref[:, h * dv:(h + 1) * dv] = (hn * og).astype(o_ref.dtype)


def _mlstm(proj, gates, gates_t, conv_w, conv_b, b_i, b_f, norm_g):
    s = proj.shape[0]
    L = min(MLSTM_L, s)
    smem = pl.BlockSpec(memory_space=pltpu.SMEM)
    return pl.pallas_call(
        _mlstm_kernel,
        grid=(s // L,),
        in_specs=[
            smem, smem,
            pl.BlockSpec((L, M_QK), lambda c: (c, 0)),
            pl.BlockSpec((L, M_QK), lambda c: (c, 1)),
            pl.BlockSpec((L, M_V), lambda c: (c, 1)),
            pl.BlockSpec((L, M_V), lambda c: (c, 2)),
            pl.BlockSpec((L, gates.shape[1]), lambda c: (c, 0)),
            pl.BlockSpec((gates_t.shape[0], L), lambda c: (0, c)),
            pl.BlockSpec((CONV_WIDTH, 2 * M_QK), lambda c: (0, 0)),
            pl.BlockSpec((1, 2 * M_QK), lambda c: (0, 0)),
            pl.BlockSpec((1, M_V), lambda c: (0, 0)),
        ],
        out_specs=pl.BlockSpec((L, M_V), lambda c: (c, 0)),
        out_shape=jax.ShapeDtypeStruct((s, M_V), BF16),
        scratch_shapes=[
            pltpu.VMEM((MLSTM_HEADS, MLSTM_DQK, MLSTM_DV), F32),
            pltpu.VMEM((MLSTM_HEADS, V7X_SUBLANES, MLSTM_DQK), F32),
            pltpu.VMEM((MLSTM_HEADS, V7X_SUBLANES, V7X_LANES), F32),
            pltpu.VMEM((V7X_SUBLANES, 2 * M_QK), F32),
        ],
        compiler_params=_params(("arbitrary",)),
        name="mlstm",
    )(b_i, b_f, proj, proj, proj, proj, gates, gates_t, conv_w, conv_b, norm_g)


def _t5_bias_tile(table_ref, h, offset):
    n_t = V7X_LANES
    r_i = lax.broadcasted_iota(jnp.int32, (n_t, n_t), 0)
    c_i = lax.broadcasted_iota(jnp.int32, (n_t, n_t), 1)
    n = jnp.maximum(offset + r_i - c_i, 0)
    max_exact = REL_BUCKETS // 2
    large = max_exact + (jnp.log(jnp.maximum(n, 1).astype(F32) / max_exact)
                         / math.log(REL_MAX_DIST / max_exact) * (REL_BUCKETS - max_exact)).astype(jnp.int32)
    large = jnp.minimum(large, REL_BUCKETS - 1)
    bucket = jnp.where(n < max_exact, n, large)
    out = jnp.zeros((n_t, n_t), F32)
    for b in range(REL_BUCKETS):
        out = jnp.where(bucket == b, table_ref[b * DIFF_HEADS + h], out)
    return out * LOG2E


def _diff_attn_kernel(qi_ref, kj_ref, q_ref, kt_ref, v_ref, lam_ref, table_ref, o_ref,
                      q1_ref, q2_ref, vx_ref, m_ref, acc_ref, bdiag_ref, bsub_ref):
    h = pl.program_id(0)
    p = pl.program_id(1)
    i = qi_ref[p]
    j = kj_ref[p]
    BQ = q_ref.shape[0]
    BK = v_ref.shape[0]
    n_half = BQ // BK
    SR = min(ATT_STRIP, BK)
    strips_per_half = BK // SR
    n_sub = BK // V7X_LANES
    c_far = table_ref[(REL_BUCKETS - 1) * DIFF_HEADS + h] * LOG2E

    @pl.when(p == 0)
    def _():
        p0 = _t5_bias_tile(table_ref, h, 0)
        p0 = jnp.where(lax.broadcasted_iota(jnp.int32, p0.shape, 1) <= lax.broadcasted_iota(jnp.int32, p0.shape, 0),
                       p0, NEG)
        p1 = _t5_bias_tile(table_ref, h, V7X_LANES)
        far = jnp.full((V7X_LANES, V7X_LANES), c_far, F32)
        neg = jnp.full((V7X_LANES, V7X_LANES), NEG, F32)
        for a in range(n_sub):
            for b in range(n_sub):
                tile = p0 if a == b else p1 if a == b + 1 else far if a > b else neg
                bdiag_ref[a * V7X_LANES:(a + 1) * V7X_LANES, b * V7X_LANES:(b + 1) * V7X_LANES] = tile
        bsub_ref[...] = jnp.full(bsub_ref.shape, c_far, F32)
        bsub_ref[0:V7X_LANES, BK - V7X_LANES:BK] = p1

    @pl.when(j == 0)
    def _():
        q = q_ref[...].astype(F32) * ((DIFF_DH ** -0.5) * LOG2E)
        lane = lax.broadcasted_iota(jnp.int32, q.shape, 1)
        q1_ref[...] = jnp.where(lane < DIFF_DH, q, 0.0).astype(BF16)
        q2_ref[...] = jnp.where(lane >= DIFF_DH, q, 0.0).astype(BF16)
        m_ref[...] = jnp.full(m_ref.shape, NEG, F32)
        acc_ref[...] = jnp.zeros_like(acc_ref)

    vx_ref[:, 0:DIFF_DV] = v_ref[...]
    vx_ref[:, DIFF_DV:] = jnp.ones((BK, DIFF_DV), BF16)

    def strip(r, kl, bias_tile=None):
        rows = slice(r * SR, (r + 1) * SR)
        kb = kt_ref[:, 0:kl]
        vb = vx_ref[0:kl, :]
        for a, qz_ref in enumerate((q1_ref, q2_ref)):
            s = _dot(qz_ref[rows, :], kb)
            m_old = m_ref[a, rows, :]
            if bias_tile is None:
                m_new = jnp.maximum(m_old, jnp.max(s, -1, keepdims=True) + c_far)
                pm = jnp.exp2(s - (m_new - c_far))
            else:
                s = s + bias_tile
                m_new = jnp.maximum(m_old, jnp.max(s, -1, keepdims=True))
                pm = jnp.exp2(s - m_new)
            alpha = jnp.exp2(m_old - m_new)
            acc_ref[a, rows, :] = alpha * acc_ref[a, rows, :] + _dot(pm.astype(BF16), vb)
            m_ref[a, rows, :] = m_new

    def key_block(delta):
        for r in range(n_half * strips_per_half):
            ro = (r % strips_per_half) * SR
            rel = None if delta is None else delta - r // strips_per_half
            if rel is None or rel <= -2:
                strip(r, BK)
            elif rel == -1:
                strip(r, BK, bsub_ref[...] if ro == 0 else None)
            elif rel == 0:
                strip(r, ro + SR, bdiag_ref[ro:ro + SR, 0:ro + SR])

    d = j - i * n_half

    @pl.when(d <= -2)
    def _():
        key_block(None)

    for delta in range(-1, n_half):
        @pl.when(d == delta)
        def _(delta=delta):
            key_block(delta)
            if delta == n_half - 1:
                lq1, lk1, lq2, lk2 = (lam_ref[t:t + 1, :] for t in range(4))
                lam = (jnp.exp(jnp.sum(lq1 * lk1, -1, keepdims=True))
                       - jnp.exp(jnp.sum(lq2 * lk2, -1, keepdims=True)) + LAM_INIT)
                o = (acc_ref[0, :, 0:DIFF_DV] / acc_ref[0, :, DIFF_DV:DIFF_DV + 1]
                     - lam * (acc_ref[1, :, 0:DIFF_DV] / acc_ref[1, :, DIFF_DV:DIFF_DV + 1]))
                o = o * lax.rsqrt(jnp.mean(o * o, -1, keepdims=True) + LN_EPS)
                o_ref[...] = (o * (1.0 - LAM_INIT)).astype(o_ref.dtype)


def _diff_attn(proj, k_t, lam_vecs, table_flat):
    s = proj.shape[0]
    BK = min(ATT_BK, s)
    BQ = min(ATT_BQ, s)
    n_half = BQ // BK
    pairs = [(i, j) for i in range(s // BQ) for j in range((i + 1) * n_half)]
    qi = jnp.asarray([a for a, _ in pairs], jnp.int32)
    kj = jnp.asarray([b for _, b in pairs], jnp.int32)
    q_blk = (2 * M_QK + 2 * M_V) // V7X_LANES
    v_blk = q_blk + D_QK // V7X_LANES
    SR = min(ATT_STRIP, BK)
    grid_spec = pltpu.PrefetchScalarGridSpec(
        num_scalar_prefetch=2,
        grid=(DIFF_HEADS, len(pairs)),
        in_specs=[
            pl.BlockSpec((BQ, V7X_LANES), lambda h, p, qi, kj: (qi[p], q_blk + h)),
            pl.BlockSpec((2 * DIFF_DH, BK), lambda h, p, qi, kj: (h, kj[p])),
            pl.BlockSpec((BK, V7X_LANES), lambda h, p, qi, kj: (kj[p], v_blk + h)),
            pl.BlockSpec((4, DIFF_DH), lambda h, p, qi, kj: (0, 0)),
            pl.BlockSpec(memory_space=pltpu.SMEM),
        ],
        out_specs=pl.BlockSpec((BQ, V7X_LANES), lambda h, p, qi, kj: (qi[p], h)),
        scratch_shapes=[
            pltpu.VMEM((BQ, V7X_LANES), BF16),
            pltpu.VMEM((BQ, V7X_LANES), BF16),
            pltpu.VMEM((BK, 2 * DIFF_DV), BF16),
            pltpu.VMEM((2, BQ, 1), F32),
            pltpu.VMEM((2, BQ, 2 * DIFF_DV), F32),
            pltpu.VMEM((BK, BK), F32),
            pltpu.VMEM((SR, BK), F32),
        ],
    )
    return pl.pallas_call(
        _diff_attn_kernel,
        grid_spec=grid_spec,
        out_shape=jax.ShapeDtypeStruct((s, D_VW), BF16),
        compiler_params=_params(("arbitrary", "arbitrary")),
        name="diff_attn",
    )(qi, kj, proj, k_t, proj, lam_vecs, table_flat)


def _out_ln_kernel(hm_ref, hd_ref, x_ref, g0_ref, b0_ref, wo1_ref, wo2_ref, g1_ref, b1_ref, wr_ref,
                   h1_ref, h1p_ref, logit_ref):
    mix = _dot(hm_ref[...], wo1_ref[...]) + _dot(hd_ref[...], wo2_ref[...])
    h0 = _layer_norm(x_ref[...], g0_ref[...], b0_ref[...])
    h1 = _layer_norm(ALPHA * h0 + mix, g1_ref[...], b1_ref[...])
    h1_ref[...] = h1
    _store_row_tiles(h1p_ref, _pack_bf16_pairs(h1))
    logit_ref[...] = _dot_nt(wr_ref[...], h1.astype(BF16))


def _out_ln(hm, hd, x2, g0, b0, wo1, wo2, g1, b1, wr):
    s, d = x2.shape
    tm = min(OUT_TM, s)
    row = lambda i: (i, 0)
    fixed = lambda i: (0, 0)
    return pl.pallas_call(
        _out_ln_kernel,
        grid=(s // tm,),
        in_specs=[
            pl.BlockSpec((tm, M_V), row), pl.BlockSpec((tm, D_VW), row), pl.BlockSpec((tm, d), row),
            pl.BlockSpec((1, d), fixed), pl.BlockSpec((1, d), fixed),
            pl.BlockSpec((M_V, d), fixed), pl.BlockSpec((D_VW, d), fixed),
            pl.BlockSpec((1, d), fixed), pl.BlockSpec((1, d), fixed),
            pl.BlockSpec((N_EXPERTS, d), fixed),
        ],
        out_specs=[pl.BlockSpec((tm, d), row), pl.BlockSpec((tm * ROW_TILE, V7X_LANES), row),
                   pl.BlockSpec((N_EXPERTS, tm), lambda i: (0, i))],
        out_shape=[jax.ShapeDtypeStruct((s, d), F32), jax.ShapeDtypeStruct((s * ROW_TILE, V7X_LANES), jnp.uint32),
                   jax.ShapeDtypeStruct((N_EXPERTS, s), F32)],
        compiler_params=_params(("arbitrary",)),
        name="out_ln",
    )(hm, hd, x2, g0, b0, wo1, wo2, g1, b1, wr)


def _route_kernel(logit_ref, rb_ref, topi_ref, pos_ref, topw_ref, cnt_ref, carry_ref):
    n_e, tm = logit_ref.shape
    gsz = n_e // N_GROUPS
    ninf = -jnp.inf

    @pl.when(pl.program_id(0) == 0)
    def _():
        carry_ref[...] = jnp.zeros_like(carry_ref)

    sc = jax.nn.sigmoid(logit_ref[...])
    sel = sc + rb_ref[...]

    sel3 = sel.reshape(N_GROUPS, gsz, tm)
    in_grp = lax.broadcasted_iota(jnp.int32, sel3.shape, 1)
    m1 = jnp.max(sel3, 1, keepdims=True)
    i1 = jnp.min(jnp.where(sel3 == m1, in_grp, gsz), 1, keepdims=True)
    m2 = jnp.max(jnp.where(in_grp == i1, ninf, sel3), 1, keepdims=True)
    gscore = (m1 + m2).reshape(N_GROUPS, tm)
    gid = lax.broadcasted_iota(jnp.int32, gscore.shape, 0)
    beaten = jnp.zeros(gscore.shape, F32)
    for o in range(1, N_GROUPS):
        other = pltpu.roll(gscore, o, 0)
        wins = (other > gscore) | ((other == gscore) & (gid >= o))
        beaten = beaten + jnp.where(wins, 1.0, 0.0)
    keep = jnp.where(beaten < TOPK_GROUPS, 1.0, 0.0)
    keep3 = jnp.broadcast_to(keep.reshape(N_GROUPS, 1, tm), sel3.shape)
    masked = jnp.where(keep3 > 0.5, sel3, ninf).reshape(n_e, tm)

    eid = lax.broadcasted_iota(jnp.int32, (n_e, tm), 0)
    onehots, idxs, ws = [], [], []
    for _ in range(TOP_K):
        m = jnp.max(masked, 0, keepdims=True)
        idx = jnp.min(jnp.where(masked == m, eid, n_e), 0, keepdims=True)
        oh = eid == idx
        onehots.append(oh)
        idxs.append(idx)
        ws.append(jnp.sum(jnp.where(oh, sc, 0.0), 0, keepdims=True))
        masked = jnp.where(oh, ninf, masked)
    wsum = ws[0]
    for w in ws[1:]:
        wsum = wsum + w

    chosen = jnp.zeros((n_e, tm), F32)
    for oh in onehots:
        chosen = jnp.where(oh, 1.0, chosen)
    r_i = lax.broadcasted_iota(jnp.int32, (tm, tm), 0)
    c_i = lax.broadcasted_iota(jnp.int32, (tm, tm), 1)
    before = jnp.where(r_i < c_i, 1.0, 0.0).astype(BF16)
    rank = _dot(chosen.astype(BF16), before) + carry_ref[...]
    carry_new = carry_ref[...] + jnp.sum(chosen, 1, keepdims=True)
    carry_ref[...] = carry_new
    cnt_ref[...] = carry_new

    kid = lax.broadcasted_iota(jnp.int32, (TOP_K, tm), 0)
    topi = jnp.zeros((TOP_K, tm), jnp.int32)
    pos = jnp.zeros((TOP_K, tm), jnp.int32)
    topw = jnp.zeros((TOP_K, tm), F32)
    for k in range(TOP_K):
        pk = jnp.sum(jnp.where(onehots[k], rank, 0.0), 0, keepdims=True)
        topi = jnp.where(kid == k, idxs[k], topi)
        pos = jnp.where(kid == k, pk.astype(jnp.int32), pos)
        topw = jnp.where(kid == k, ws[k] / wsum * ROUTED_SCALE, topw)
    topi_ref[...] = topi
    pos_ref[...] = pos
    topw_ref[...] = topw


def _route(logits_t, router_bias_col):
    n_e, t = logits_t.shape
    tm = min(ROUTE_TM, t)
    col = lambda i: (0, i)
    fixed = lambda i: (0, 0)
    return pl.pallas_call(
        _route_kernel,
        grid=(t // tm,),
        in_specs=[pl.BlockSpec((n_e, tm), col), pl.BlockSpec((n_e, 1), fixed)],
        out_specs=[pl.BlockSpec((TOP_K, tm), col), pl.BlockSpec((TOP_K, tm), col),
                   pl.BlockSpec((TOP_K, tm), col), pl.BlockSpec((n_e, 1), fixed)],
        out_shape=[jax.ShapeDtypeStruct((TOP_K, t), jnp.int32), jax.ShapeDtypeStruct((TOP_K, t), jnp.int32),
                   jax.ShapeDtypeStruct((TOP_K, t), F32), jax.ShapeDtypeStruct((n_e, 1), F32)],
        scratch_shapes=[pltpu.VMEM((n_e, 1), F32)],
        compiler_params=_params(("arbitrary",)),
        name="route",
    )(logits_t, router_bias_col)


def _plan_kernel(cnt_ref, cntc_ref, topi_ref, pos_ref, off_ref, vexp_ref, vblk_ref, vcnt_ref, nvis_ref, dest_ref):
    nv = vexp_ref.shape[0]
    tr = float(MOE_TR)
    cnt = jnp.broadcast_to(cnt_ref[...], (V7X_SUBLANES, N_EXPERTS))
    r_i = lax.broadcasted_iota(jnp.int32, (N_EXPERTS, N_EXPERTS), 0)
    c_i = lax.broadcasted_iota(jnp.int32, (N_EXPERTS, N_EXPERTS), 1)
    upper = jnp.where(r_i <= c_i, 1.0, 0.0).astype(BF16)

    def cumsum_lanes(a):
        a1, a2, a3 = _split3(a)
        return _dot(a1, upper) + _dot(a2, upper) + _dot(a3, upper)

    end = cumsum_lanes(cnt)
    start = end - cnt
    first_blk = jnp.floor(start / tr)
    last_blk = jnp.floor((end - 1.0) / tr)
    nvis_e = jnp.where(cnt > 0.0, last_blk - first_blk + 1.0, 0.0)
    vend = cumsum_lanes(nvis_e)
    vstart = vend - nvis_e
    off_ref[...] = start[0:1, :].astype(jnp.int32)
    nvis_ref[...] = vend[0:1, N_EXPERTS - 1:N_EXPERTS].astype(jnp.int32)

    v = lax.broadcasted_iota(jnp.int32, (nv, N_EXPERTS), 0).astype(F32)
    ve = jnp.broadcast_to(vend[0:1, :], (nv, N_EXPERTS))
    expert = jnp.sum(jnp.where(ve <= v, 1.0, 0.0), -1, keepdims=True)
    expert = jnp.minimum(expert, N_EXPERTS - 1.0)
    lane = lax.broadcasted_iota(jnp.int32, (nv, N_EXPERTS), 1).astype(F32)
    mine = lane == expert
    fb = jnp.sum(jnp.where(mine, jnp.broadcast_to(first_blk[0:1, :], (nv, N_EXPERTS)), 0.0), -1, keepdims=True)
    vs = jnp.sum(jnp.where(mine, jnp.broadcast_to(vstart[0:1, :], (nv, N_EXPERTS)), 0.0), -1, keepdims=True)
    vc = jnp.sum(jnp.where(mine, jnp.broadcast_to(nvis_e[0:1, :], (nv, N_EXPERTS)), 0.0), -1, keepdims=True)
    vexp_ref[...] = expert.astype(jnp.int32)
    vblk_ref[...] = (fb + (v[:, 0:1] - vs)).astype(jnp.int32)
    vcnt_ref[...] = vc.astype(jnp.int32)

    lower = jnp.where(c_i < r_i, 1.0, 0.0).astype(BF16)
    c1, c2, c3 = _split3(jnp.broadcast_to(cntc_ref[...], (N_EXPERTS, V7X_LANES)))
    start_col = (_dot(lower, c1) + _dot(lower, c2) + _dot(lower, c3))[:, 0:1]
    n_tok = topi_ref.shape[1]
    tb = min(PLAN_TB, n_tok)
    eid = lax.broadcasted_iota(jnp.int32, (N_EXPERTS, tb), 0)
    kid = lax.broadcasted_iota(jnp.int32, (TOP_K, tb), 0)

    def dest_block(i, carry):
        cols = pl.ds(pl.multiple_of(i * tb, tb), tb)
        ti = topi_ref[:, cols]
        first_row = jnp.zeros((TOP_K, tb), F32)
        for k in range(TOP_K):
            fr = jnp.sum(jnp.where(eid == ti[k:k + 1, :], start_col, 0.0), 0, keepdims=True)
            first_row = jnp.where(kid == k, fr, first_row)
        dest_ref[:, cols] = first_row.astype(jnp.int32) + pos_ref[:, cols]
        return carry

    lax.fori_loop(0, n_tok // tb, dest_block, 0)


def _plan(counts_row, counts_col, topi_t, pos_t):
    n_rows = topi_t.shape[0] * topi_t.shape[1]
    nv = n_rows // MOE_TR + N_EXPERTS
    return pl.pallas_call(
        _plan_kernel,
        out_shape=[jax.ShapeDtypeStruct((1, N_EXPERTS), jnp.int32), jax.ShapeDtypeStruct((nv, 1), jnp.int32),
                   jax.ShapeDtypeStruct((nv, 1), jnp.int32), jax.ShapeDtypeStruct((nv, 1), jnp.int32),
                   jax.ShapeDtypeStruct((1, 1), jnp.int32), jax.ShapeDtypeStruct(topi_t.shape, jnp.int32)],
        compiler_params=pltpu.CompilerParams(vmem_limit_bytes=V7X_VMEM_LIMIT_BYTES),
        name="plan",
    )(counts_row, counts_col, topi_t, pos_t)


def _row_tile(ref, r):
    return ref.at[pl.ds(pl.multiple_of(r * ROW_TILE, ROW_TILE), ROW_TILE)]


def _dispatch_kernel(dest_ref, x_ref, xs_ref, sem):
    tm = x_ref.shape[0] // ROW_TILE

    def issue(t, carry):
        for k in range(TOP_K):
            d = dest_ref[t * TOP_K + k]
            pltpu.make_async_copy(_row_tile(x_ref, t), _row_tile(xs_ref, d), sem).start(priority=k % 2)
        return carry

    lax.fori_loop(0, tm, issue, 0)
    for _ in range(TOP_K):
        pltpu.make_async_copy(x_ref, xs_ref.at[pl.ds(0, tm * ROW_TILE)], sem).wait()


def _dispatch(dest_flat, rows):
    t = rows.shape[0] // ROW_TILE
    tm = min(DISPATCH_TM, t)
    return pl.pallas_call(
        _dispatch_kernel,
        grid=(t // tm,),
        in_specs=[
            pl.BlockSpec((tm * TOP_K,), lambda i: (i,), memory_space=pltpu.SMEM),
            pl.BlockSpec((tm * ROW_TILE, V7X_LANES), lambda i: (i, 0)),
        ],
        out_specs=pl.BlockSpec(memory_space=pl.ANY),
        out_shape=jax.ShapeDtypeStruct((t * TOP_K * ROW_TILE, V7X_LANES), rows.dtype),
        scratch_shapes=[pltpu.SemaphoreType.DMA(())],
        compiler_params=_params(("arbitrary",)),
        name="dispatch",
    )(dest_flat, rows)


def _experts_kernel(n_shared_steps, vexp_ref, vblk_ref, vcnt_ref, off_ref, nvis_ref, xs_ref, wg_hbm, wu_hbm,
                    wd_hbm, hp_ref, sg_ref, su_ref, sd_ref, ys_ref, ysh_ref,
                    wg_buf, wu_buf, wd_buf, acc_ref, slot_ref, sem):
    v = pl.program_id(0)
    tr = xs_ref.shape[0] // ROW_TILE
    nvis = nvis_ref[0]

    @pl.when(v < n_shared_steps)
    def _():
        hb = _unpack_bf16_pairs(_load_row_tiles(hp_ref, hp_ref.shape[0] // ROW_TILE, ROW_TILE)).astype(BF16)
        gate = _dot(hb, sg_ref[...])
        up = _dot(hb, su_ref[...])
        ysh_ref[...] = _dot((gate * jax.nn.sigmoid(gate) * up).astype(BF16), sd_ref[...])

    half = wd_buf.shape[1] // 2

    def weight_copies(e, slot):
        return ((pltpu.make_async_copy(wg_hbm.at[e], wg_buf.at[slot], sem.at[slot, 0]), 1),
                (pltpu.make_async_copy(wu_hbm.at[e], wu_buf.at[slot], sem.at[slot, 1]), 1),
                (pltpu.make_async_copy(wd_hbm.at[e, pl.ds(0, half)], wd_buf.at[slot, pl.ds(0, half)],
                                       sem.at[slot, 2]), 1),
                (pltpu.make_async_copy(wd_hbm.at[e, pl.ds(half, half)], wd_buf.at[slot, pl.ds(half, half)],
                                       sem.at[slot, 3]), 1))

    @pl.when(v < nvis)
    def _():
        e = vexp_ref[v]
        blk = vblk_ref[v]
        first_of_expert = jnp.logical_or(v == 0, vexp_ref[jnp.maximum(v - 1, 0)] != e)

        @pl.when(v == 0)
        def _():
            slot_ref[0] = 0
            for c, queue in weight_copies(e, 0):
                c.start(priority=queue)

        @pl.when(jnp.logical_and(first_of_expert, v > 0))
        def _():
            slot_ref[0] = 1 - slot_ref[0]

        slot = slot_ref[0]

        @pl.when(first_of_expert)
        def _():
            nxt = v + vcnt_ref[v]

            @pl.when(nxt < nvis)
            def _():
                for c, queue in weight_copies(vexp_ref[jnp.minimum(nxt, vexp_ref.shape[0] - 1)], 1 - slot):
                    c.start(priority=queue)

            for c, _ in weight_copies(e, slot):
                c.wait()

        lo = off_ref[e] - blk * tr
        hi = off_ref[e + 1] - blk * tr
        first_of_block = jnp.logical_or(v == 0, vblk_ref[jnp.maximum(v - 1, 0)] != blk)

        @pl.when(first_of_block)
        def _():
            acc_ref[...] = jnp.zeros_like(acc_ref)

        def swiglu_rows(r0, n):
            row = r0 + lax.broadcasted_iota(jnp.int32, (n, 1), 0)
            mine = (row >= lo) & (row < hi)
            xb = _unpack_bf16_pairs(_load_row_tiles(xs_ref, n, ROW_TILE, r0)).astype(BF16)
            gate = _dot(xb, wg_buf[slot].astype(BF16))
            up = _dot(xb, wu_buf[slot].astype(BF16))
            act = (gate * jax.nn.sigmoid(gate) * up).astype(BF16)
            acc_ref[r0:r0 + n, :] += jnp.where(mine, _dot(act, wd_buf[slot].astype(BF16)), 0.0)

        hr = tr // 2
        top_only = hi <= hr
        bottom_only = lo >= hr

        @pl.when(top_only)
        def _():
            swiglu_rows(0, hr)

        @pl.when(bottom_only)
        def _():
            swiglu_rows(hr, hr)

        @pl.when(jnp.logical_not(jnp.logical_or(top_only, bottom_only)))
        def _():
            swiglu_rows(0, tr)

        last_of_block = jnp.logical_or(v + 1 >= nvis, vblk_ref[jnp.minimum(v + 1, vblk_ref.shape[0] - 1)] != blk)

        @pl.when(last_of_block)
        def _():
            _store_row_tiles(ys_ref, _pack_bf16_pairs(acc_ref[...]))


def _experts(vexp, vblk, vcnt, off_ext, nvis, xs, w_gate, w_up, w_down, h1p, sg, su, sd):
    n_rows = xs.shape[0] // ROW_TILE
    d = D_MODEL
    nv = vexp.shape[0]
    t = h1p.shape[0] // ROW_TILE
    tm = min(SHARED_TM, t)
    n_shared_steps = t // tm
    assert n_shared_steps <= n_rows // MOE_TR

    def block_of(v, ve, vb, vc, off, nvis):
        return (vb[jnp.minimum(v, nvis[0] - 1)], 0)

    def shared_block(v, ve, vb, vc, off, nvis):
        return (jnp.minimum(v, n_shared_steps - 1), 0)

    fixed = lambda v, ve, vb, vc, off, nvis: (0, 0)
    resident = dict(pipeline_mode=pl.Buffered(1))
    grid_spec = pltpu.PrefetchScalarGridSpec(
        num_scalar_prefetch=5,
        grid=(nv,),
        in_specs=[
            pl.BlockSpec((MOE_TR * ROW_TILE, V7X_LANES), block_of),
            pl.BlockSpec(memory_space=pl.ANY),
            pl.BlockSpec(memory_space=pl.ANY),
            pl.BlockSpec(memory_space=pl.ANY),
            pl.BlockSpec((tm * ROW_TILE, V7X_LANES), shared_block),
            pl.BlockSpec((d, D_EXPERT), fixed, **resident),
            pl.BlockSpec((d, D_EXPERT), fixed, **resident),
            pl.BlockSpec((D_EXPERT, d), fixed, **resident),
        ],
        out_specs=[pl.BlockSpec((MOE_TR * ROW_TILE, V7X_LANES), block_of), pl.BlockSpec((tm, d), shared_block)],
        scratch_shapes=[
            pltpu.VMEM((2, d, D_EXPERT), F32),
            pltpu.VMEM((2, d, D_EXPERT), F32),
            pltpu.VMEM((2, D_EXPERT, d), F32),
            pltpu.VMEM((MOE_TR, d), F32),
            pltpu.SMEM((1,), jnp.int32),
            pltpu.SemaphoreType.DMA((2, 4)),
        ],
    )
    return pl.pallas_call(
        functools.partial(_experts_kernel, n_shared_steps),
        grid_spec=grid_spec,
        out_shape=[jax.ShapeDtypeStruct(xs.shape, jnp.uint32), jax.ShapeDtypeStruct((t, d), F32)],
        compiler_params=_params(("arbitrary",)),
        name="experts",
    )(vexp, vblk, vcnt, off_ext, nvis, xs, w_gate, w_up, w_down, h1p, sg, su, sd)


def _combine_kernel(dest_ref, dest_next_ref, topw_ref, h1_ref, ysh_ref, ys_ref, g2_ref, b2_ref, o_ref,
                    buf_ref, sem):
    i = pl.program_id(0)
    tm = h1_ref.shape[0]
    slot = lax.rem(i, 2)

    def issue_block(d_ref, s):
        def issue(t, carry):
            for k in range(TOP_K):
                d = d_ref[t * TOP_K + k]
                pltpu.make_async_copy(_row_tile(ys_ref, d), _row_tile(buf_ref.at[s, k], t),
                                      sem.at[s]).start(priority=k % 2)
            return carry

        lax.fori_loop(0, tm, issue, 0)

    @pl.when(i == 0)
    def _():
        issue_block(dest_ref, 0)

    @pl.when(i + 1 < pl.num_programs(0))
    def _():
        issue_block(dest_next_ref, 1 - slot)

    y = ALPHA * h1_ref[...] + ysh_ref[...]
    for k in range(TOP_K):
        pltpu.make_async_copy(ys_ref.at[pl.ds(0, tm * ROW_TILE)], buf_ref.at[slot, k], sem.at[slot]).wait()
    w = topw_ref[...]
    for k in range(TOP_K):
        y = y + _unpack_bf16_pairs(_load_row_tiles(buf_ref.at[slot, k], tm, ROW_TILE)) * w[:, k:k + 1]
    o_ref[...] = _layer_norm(y, g2_ref[...], b2_ref[...])


def _combine(dest_flat, topw, h1, ysh, ys, g2, b2):
    t, d = h1.shape
    tm = min(MOVE_TM, t)
    nb = t // tm
    row = lambda i: (i, 0)
    fixed = lambda i: (0, 0)
    return pl.pallas_call(
        _combine_kernel,
        grid=(nb,),
        in_specs=[
            pl.BlockSpec((tm * TOP_K,), lambda i: (i,), memory_space=pltpu.SMEM),
            pl.BlockSpec((tm * TOP_K,), lambda i: (jnp.minimum(i + 1, nb - 1),), memory_space=pltpu.SMEM),
            pl.BlockSpec((tm, TOP_K), row),
            pl.BlockSpec((tm, d), row),
            pl.BlockSpec((tm, d), row),
            pl.BlockSpec(memory_space=pl.ANY),
            pl.BlockSpec((1, d), fixed), pl.BlockSpec((1, d), fixed),
        ],
        out_specs=pl.BlockSpec((tm, d), row),
        out_shape=jax.ShapeDtypeStruct((t, d), F32),
        scratch_shapes=[pltpu.VMEM((2, TOP_K, tm * ROW_TILE, V7X_LANES), ys.dtype), pltpu.SemaphoreType.DMA((2,))],
        compiler_params=_params(("arbitrary",)),
        name="combine",
    )(dest_flat, dest_flat, topw, h1, ysh, ys, g2, b2)


def kernel(x, ln_emb_g, ln_emb_b, w_in, conv_w, conv_b, b_igate, b_fgate, mlstm_norm_g, lambda_q1, lambda_k1,
           lambda_q2, lambda_k2, rel_bias, w_out, ln1_g, ln1_b, w_router, router_bias, w_gate, w_up, w_down,
           ws_gate, ws_up, ws_down, ln2_g, ln2_b):
    bsz, s, d = x.shape
    assert bsz == 1 and w_in.shape[0] == DEPTH == 1
    x2 = x.reshape(s, d)
    row = lambda a: a.reshape(1, -1).astype(F32)

    w = w_in[0]
    c0 = 2 * M_QK + 2 * M_V
    c1 = c0 + 2 * MLSTM_HEADS
    c2 = c1 + D_QK
    c3 = c2 + D_QK
    w_main = jnp.concatenate([w[:, :c0], w[:, c1:c2], w[:, c3:]], axis=1).astype(BF16)
    w_k_t = w[:, c2:c3].astype(BF16)
    wi, wf = w[:, c0:c0 + MLSTM_HEADS], w[:, c0 + MLSTM_HEADS:c1]
    w_gates = jnp.zeros((d, MLSTM_HEADS, V7X_LANES), F32).at[:, :, 0].set(wi).at[:, :, 1].set(wf)
    w_gates = w_gates.reshape(d, MLSTM_HEADS * V7X_LANES).astype(BF16)
    w_gates_t = jnp.zeros((MLSTM_HEADS, V7X_SUBLANES, d), F32).at[:, 0].set(wi.T).at[:, 1].set(wf.T)
    w_gates_t = w_gates_t.reshape(MLSTM_HEADS * V7X_SUBLANES, d).astype(BF16)

    proj, gates, gates_t, k_t = _ln_proj(x2, row(ln_emb_g), row(ln_emb_b), w_main, w_gates, w_gates_t, w_k_t)

    hm = _mlstm(proj, gates, gates_t, conv_w[0], row(conv_b[0]), b_igate[0].astype(F32),
                b_fgate[0].astype(F32), row(mlstm_norm_g[0]))

    lam_vecs = jnp.stack([lambda_q1[0], lambda_k1[0], lambda_q2[0], lambda_k2[0]]).astype(F32)
    hd = _diff_attn(proj, k_t, lam_vecs, rel_bias.astype(F32).reshape(-1))

    wo = w_out[0].astype(BF16)
    h1, h1p, logits_t = _out_ln(hm, hd, x2, row(ln_emb_g), row(ln_emb_b), wo[:M_V], wo[M_V:], row(ln1_g[0]),
                           row(ln1_b[0]), w_router[0].T.astype(BF16))

    topi_t, pos_t, topw_t, counts = _route(logits_t, router_bias[0].reshape(-1, 1).astype(F32))
    off, vexp, vblk, vcnt, nvis, dest_t = _plan(counts.reshape(1, -1), counts, topi_t, pos_t)
    off_ext = jnp.concatenate([off.reshape(-1), jnp.full((1,), s * TOP_K, jnp.int32)])
    dest_flat = dest_t.T.reshape(-1)

    xs = _dispatch(dest_flat, h1p)
    ys, ysh = _experts(vexp.reshape(-1), vblk.reshape(-1), vcnt.reshape(-1), off_ext, nvis.reshape(-1), xs,
                       w_gate[0], w_up[0], w_down[0], h1p, ws_gate[0].astype(BF16), ws_up[0].astype(BF16),
                       ws_down[0].astype(BF16))
    out = _combine(dest_flat, topw_t.T, h1, ysh, ys, row(ln2_g[0]), row(ln2_b[0]))
    return out.reshape(bsz, s, d)
```
